```python
import jax
import jax.numpy as jnp
from jax import lax
import numpy as np

D_MODEL = 2048
BATCH = 4
SEQ = 4096
DEPTH = 2

GRID_W = 64
CTX_LEN = 256
EPS = 1e-6
F32 = jnp.float32

DN_HEADS = 8
DN_DK = 128
DN_DV = 128
DN_W = DN_HEADS * DN_DK
DN_CHUNK = 64
CONV_W = 4
LRU_WIDTH = 1024
LRU_BLOCKS = 8
LRU_BW = LRU_WIDTH // LRU_BLOCKS
LRU_C = 8.0
ATT_HEADS = 8
ATT_KV_HEADS = 2
ATT_HD = 128
ATT_Q_W = ATT_HEADS * ATT_HD
ATT_KV_W = ATT_KV_HEADS * ATT_HD
ATT_WINDOW = 128
ATT_BLOCK = 128
ROPE_THETA = 10000.0
N_GROUPS = 8
EXPERTS_PER_GROUP = 8
N_EXPERTS = N_GROUPS * EXPERTS_PER_GROUP
TOP_K = 2
D_EXPERT = 512
MOE_BLOCK = 128
N_BRANCH = 3
IN_SPLITS = (DN_W, DN_W, DN_W, DN_W,
             DN_HEADS, DN_HEADS, DN_HEADS, DN_HEADS,
             LRU_WIDTH, LRU_WIDTH,
             ATT_Q_W, ATT_KV_W, ATT_KV_W,
             N_BRANCH * D_MODEL)
IN_W = sum(IN_SPLITS)

kernel_name = 'hybrid_deltanet_rglru_swa_hmoe_dit'


def rmsnorm(x, g):
    xf = x.astype(F32)
    y = xf * lax.rsqrt(jnp.mean(xf * xf, axis=-1, keepdims=True) + EPS)
    return (y * g.astype(F32)).astype(x.dtype)


def l2norm(x):
    return x * lax.rsqrt(jnp.sum(x * x, axis=-1, keepdims=True) + EPS)


def modulate(h, shift, scale):
    return h * (1 + scale) + shift


def rev_segments(x, n_ctx):
    return jnp.concatenate([jnp.flip(x[:, :n_ctx], axis=1), jnp.flip(x[:, n_ctx:], axis=1)], axis=1)


def dwconv_centred(x, w):
    n = x.shape[1]
    lo = (CONV_W - 1) // 2
    xp = jnp.pad(x, ((0, 0), (lo, CONV_W - 1 - lo), (0, 0)))
    y = xp[:, 0:n] * w[0]
    for j in range(1, CONV_W):
        y = y + xp[:, j:j + n] * w[j]
    return y


def segment_conv(x, w, n_ctx):
    return jnp.concatenate([dwconv_centred(x[:, :n_ctx], w), dwconv_centred(x[:, n_ctx:], w)], axis=1)


def split_in(p):
    offs = []
    acc = 0
    for s in IN_SPLITS[:-1]:
        acc += s
        offs.append(acc)
    return jnp.split(p, offs, axis=-1)


def chunk_gated_delta(q, k, v, g, beta):
    b, n, h, dk = k.shape
    dv = v.shape[-1]
    nc = n // DN_CHUNK

    def to_chunks(t):
        t = t.reshape((b, nc, DN_CHUNK, h) + t.shape[3:])
        return jnp.moveaxis(t, 3, 1)

    qc, kc, vc, bc = to_chunks(q), to_chunks(k), to_chunks(v), to_chunks(beta)
    gc = jnp.cumsum(to_chunks(g), axis=-1)
    idx = jnp.arange(DN_CHUNK)
    lower = idx[:, None] >= idx[None, :]
    strict = idx[:, None] > idx[None, :]
    decay = jnp.exp(jnp.where(lower, gc[..., :, None] - gc[..., None, :], -jnp.inf))
    kb = kc * bc[..., None]
    a_kk = jnp.where(strict, jnp.einsum('bhncd,bhnsd->bhncs', kb, kc) * decay, 0.0)
    rhs = jnp.concatenate([vc * bc[..., None], kb * jnp.exp(gc)[..., None]], axis=-1)
    sol = lax.linalg.triangular_solve(a_kk + jnp.eye(DN_CHUNK, dtype=F32), rhs,
                                      left_side=True, lower=True, unit_diagonal=True)
    u, w = sol[..., :dv], sol[..., dv:]
    a_qk = jnp.einsum('bhncd,bhnsd->bhncs', qc, kc) * decay
    q_dec = qc * jnp.exp(gc)[..., None]
    k_dec = kc * jnp.exp(gc[..., -1:] - gc)[..., None]
    g_tot = jnp.exp(gc[..., -1])

    def step(state, inp):
        q_i, k_i, u_i, w_i, a_i, d_i = inp
        v_new = u_i - jnp.einsum('bhck,bhkv->bhcv', w_i, state)
        o_i = jnp.einsum('bhck,bhkv->bhcv', q_i, state) + jnp.einsum('bhcs,bhsv->bhcv', a_i, v_new)
        state = state * d_i[..., None, None] + jnp.einsum('bhck,bhcv->bhkv', k_i, v_new)
        return state, o_i

    xs = tuple(jnp.moveaxis(t, 2, 0) for t in (q_dec, k_dec, u, w, a_qk, g_tot))
    _, o = lax.scan(step, jnp.zeros((b, h, dk, dv), F32), xs)
    return jnp.transpose(o, (1, 0, 3, 2, 4)).reshape(b, n, h, dv)


def gated_deltanet(q, k, v, z, beta_f, beta_b, alpha_f, alpha_b, conv_w, a_log, dt_bias, norm_g, n_ctx):
    b, n, _ = q.shape
    qkv = jax.nn.silu(segment_conv(jnp.concatenate([q, k, v], axis=-1), conv_w, n_ctx)).astype(F32)
    qh, kh, vh = [t.reshape(b, n, DN_HEADS, DN_DK) for t in jnp.split(qkv, 3, axis=-1)]
    qh = l2norm(qh) * (DN_DK ** -0.5)
    kh = l2norm(kh)
    bf = jax.nn.sigmoid(beta_f.astype(F32))
    bb = jax.nn.sigmoid(beta_b.astype(F32))
    gf = -jnp.exp(a_log[0].astype(F32)) * jax.nn.softplus(alpha_f.astype(F32) + dt_bias[0].astype(F32))
    gb = -jnp.exp(a_log[1].astype(F32)) * jax.nn.softplus(alpha_b.astype(F32) + dt_bias[1].astype(F32))
    rs = lambda t: rev_segments(t, n_ctx)
    o_f = chunk_gated_delta(qh, kh, vh, gf, bf)
    o_b = rs(chunk_gated_delta(rs(qh), rs(kh), rs(vh), rs(gb), rs(bb)))
    o = rmsnorm(o_f + o_b, norm_g) * jax.nn.silu(z.astype(F32).reshape(b, n, DN_HEADS, DN_DV))
    return o.reshape(b, n, DN_W).astype(q.dtype)


def linear_combine(left, right):
    a_l, b_l = left
    a_r, b_r = right
    return a_l * a_r, a_r * b_l + b_r


def rg_lru_scan(xc, wa, ba, wx, bx, lam):
    b, n, width = xc.shape
    xb = xc.reshape(b, n, LRU_BLOCKS, LRU_BW)
    r = jax.nn.sigmoid(jnp.einsum('blkc,kcd->blkd', xb, wa.astype(F32)).reshape(b, n, width) + ba.astype(F32))
    i = jax.nn.sigmoid(jnp.einsum('blkc,kcd->blkd', xb, wx.astype(F32)).reshape(b, n, width) + bx.astype(F32))
    log_a = -LRU_C * r * jax.nn.softplus(-lam.astype(F32))
    a = jnp.exp(log_a)
    u = jnp.sqrt(-jnp.expm1(2.0 * log_a)) * (i * xc)
    _, hs = lax.associative_scan(linear_combine, (a, u), axis=1)
    return hs


def rg_lru_branch(xr, xg, conv_w, conv_b, wa, ba, wx, bx, lam, n_ctx):
    xc = (segment_conv(xr, conv_w, n_ctx) + conv_b).astype(F32)
    rs = lambda t: rev_segments(t, n_ctx)
    hs = rg_lru_scan(xc, wa[0], ba[0], wx[0], bx[0], lam[0]) + rs(rg_lru_scan(rs(xc), wa[1], ba[1], wx[1], bx[1], lam[1]))
    return (jax.nn.gelu(xg.astype(F32)) * hs).astype(xr.dtype)


def heads(t, n_heads):
    return t.reshape(t.shape[:2] + (n_heads, ATT_HD))


def qk_norm(t, n_heads, g):
    return rmsnorm(heads(t, n_heads), g).astype(F32)


def axial_angles(s):
    rows = s // GRID_W
    row = jnp.repeat(jnp.arange(rows, dtype=F32), GRID_W)
    col = jnp.tile(jnp.arange(GRID_W, dtype=F32), rows)
    half = ATT_HD // 2
    inv = ROPE_THETA ** (-jnp.arange(0, half, 2, dtype=F32) / half)
    return row[:, None] * inv, col[:, None] * inv


def rotate_half(xa, ang):
    m = ang.shape[-1]
    cos = jnp.cos(ang)[None, :, None, :]
    sin = jnp.sin(ang)[None, :, None, :]
    x1, x2 = xa[..., :m], xa[..., m:]
    return jnp.concatenate([x1 * cos - x2 * sin, x2 * cos + x1 * sin], axis=-1)


def rope_axial(x, ang_row, ang_col):
    half = ATT_HD // 2
    return jnp.concatenate([rotate_half(x[..., :half], ang_row), rotate_half(x[..., half:], ang_col)], axis=-1)


def sink_softmax(logits, sink):
    m = jnp.maximum(jnp.max(logits, axis=-1, keepdims=True), sink)
    e = jnp.exp(logits - m)
    return e / (jnp.sum(e, axis=-1, keepdims=True) + jnp.exp(sink - m))


def latent_window_attention(q, k, v, k_ctx, v_ctx, qn_g, kn_g, sink):
    b, s, _ = q.shape
    grp = ATT_HEADS // ATT_KV_HEADS
    ang_row, ang_col = axial_angles(s)
    qh = rope_axial(qk_norm(q, ATT_HEADS, qn_g), ang_row, ang_col) * (ATT_HD ** -0.5)
    kh = rope_axial(qk_norm(k, ATT_KV_HEADS, kn_g), ang_row, ang_col)
    vh = heads(v, ATT_KV_HEADS).astype(F32)
    kc = qk_norm(k_ctx, ATT_KV_HEADS, kn_g)
    vc = heads(v_ctx, ATT_KV_HEADS).astype(F32)
    nb = s // ATT_BLOCK
    qb = qh.reshape(b, nb, ATT_BLOCK, ATT_KV_HEADS, grp, ATT_HD)
    pad = ((0, 0), (ATT_BLOCK, ATT_BLOCK), (0, 0), (0, 0))
    kp, vp = jnp.pad(kh, pad), jnp.pad(vh, pad)
    band = lambda t: jnp.concatenate(
        [t[:, j * ATT_BLOCK:j * ATT_BLOCK + s].reshape(b, nb, ATT_BLOCK, ATT_KV_HEADS, ATT_HD) for j in range(3)], axis=2)
    kw, vw = band(kp), band(vp)
    blk = jnp.arange(nb)[:, None, None] * ATT_BLOCK
    qpos = blk + jnp.arange(ATT_BLOCK)[None, :, None]
    kpos = blk - ATT_BLOCK + jnp.arange(3 * ATT_BLOCK)[None, None, :]
    valid = (jnp.abs(qpos - kpos) <= ATT_WINDOW) & (kpos >= 0) & (kpos < s)
    s_loc = jnp.einsum('bnqhgd,bnkhd->bnhgqk', qb, kw)
    s_loc = jnp.where(valid[None, :, None, None], s_loc, -jnp.inf)
    s_ctx = jnp.einsum('bnqhgd,bchd->bnhgqc', qb, kc)
    p = sink_softmax(jnp.concatenate([s_loc, s_ctx], axis=-1),
                     sink.astype(F32).reshape(ATT_KV_HEADS, grp)[None, None, :, :, None, None])
    nloc = 3 * ATT_BLOCK
    o = jnp.einsum('bnhgqk,bnkhd->bnqhgd', p[..., :nloc], vw) + jnp.einsum('bnhgqc,bchd->bnqhgd', p[..., nloc:], vc)
    return o.reshape(b, s, ATT_Q_W).astype(q.dtype)


def context_attention(q_ctx, k_ctx, v_ctx, qn_g, kn_g, sink):
    b, n, _ = q_ctx.shape
    grp = ATT_HEADS // ATT_KV_HEADS
    qh = (qk_norm(q_ctx, ATT_HEADS, qn_g) * (ATT_HD ** -0.5)).reshape(b, n, ATT_KV_HEADS, grp, ATT_HD)
    kh = qk_norm(k_ctx, ATT_KV_HEADS, kn_g)
    vh = heads(v_ctx, ATT_KV_HEADS).astype(F32)
    logits = jnp.einsum('bqhgd,bchd->bhgqc', qh, kh)
    p = sink_softmax(logits, sink.astype(F32).reshape(ATT_KV_HEADS, grp)[None, :, :, None, None])
    o = jnp.einsum('bhgqc,bchd->bqhgd', p, vh)
    return o.reshape(b, n, ATT_Q_W).astype(q_ctx.dtype)


def hybrid_mixer(h, n_ctx, latent_only, w_in, dn_conv, dn_a_log, dn_dt_bias, dn_norm_g, dn_out,
                 lru_conv, lru_conv_b, lru_wa, lru_ba, lru_wx, lru_bx, lru_lambda, lru_out,
                 att_qn_g, att_kn_g, att_sink, att_out, w_o):
    (dq, dk, dv, dz, dbf, dbb, daf, dab, lx, lg, aq, ak, av, gates) = split_in(h @ w_in)
    y_a = gated_deltanet(dq, dk, dv, dz, dbf, dbb, daf, dab, dn_conv, dn_a_log, dn_dt_bias, dn_norm_g, n_ctx)
    y_b = rg_lru_branch(lx, lg, lru_conv, lru_conv_b, lru_wa, lru_ba, lru_wx, lru_bx, lru_lambda, n_ctx)
    y_c = latent_window_attention(aq[:, n_ctx:], ak[:, n_ctx:], av[:, n_ctx:], ak[:, :n_ctx], av[:, :n_ctx],
                                  att_qn_g, att_kn_g, att_sink)
    if latent_only:
        y_a, y_b, gates = y_a[:, n_ctx:], y_b[:, n_ctx:], gates[:, n_ctx:]
    else:
        y_c = jnp.concatenate([context_attention(aq[:, :n_ctx], ak[:, :n_ctx], av[:, :n_ctx],
                                                 att_qn_g, att_kn_g, att_sink), y_c], axis=1)
    g_a, g_b, g_c = jnp.split(jax.nn.sigmoid(gates.astype(F32)).astype(h.dtype), N_BRANCH, axis=-1)
    merged = g_a * (y_a @ dn_out) + g_b * (y_b @ lru_out) + g_c * (y_c @ att_out)
    return merged @ w_o


def hier_moe(h, w_grp, b_grp, w_exp, b_exp, w_gate, w_up, w_down):
    shape = h.shape
    t = h.reshape(-1, shape[-1])
    n = t.shape[0]
    grp_logits = (t @ w_grp + b_grp).astype(F32)
    g_idx = jnp.argmax(grp_logits, axis=-1)
    p_grp = jnp.max(jax.nn.softmax(grp_logits, axis=-1), axis=-1, keepdims=True)
    exp_logits = (t @ w_exp + b_exp).astype(F32).reshape(n, N_GROUPS, EXPERTS_PER_GROUP)
    in_grp = jnp.einsum('ng,nge->ne', jax.nn.one_hot(g_idx, N_GROUPS, dtype=F32), exp_logits)
    top_val, top_idx = lax.top_k(in_grp, TOP_K)
    wts = (p_grp * jax.nn.softmax(top_val, axis=-1)).reshape(-1)
    eid = (g_idx[:, None] * EXPERTS_PER_GROUP + top_idx).reshape(-1).astype(jnp.int32)
    tok = jnp.repeat(jnp.arange(n, dtype=jnp.int32), TOP_K)
    n_assign = n * TOP_K
    order = jnp.argsort(eid)
    e_sorted = eid[order]
    counts = jnp.bincount(eid, length=N_EXPERTS)
    padded = (counts + MOE_BLOCK - 1) // MOE_BLOCK * MOE_BLOCK
    pad_end = jnp.cumsum(padded)
    pad_start = pad_end - padded
    start = jnp.cumsum(counts) - counts
    dest = pad_start[e_sorted] + (jnp.arange(n_assign) - start[e_sorted])
    n_blocks = (n_assign + N_EXPERTS * (MOE_BLOCK - 1) + MOE_BLOCK - 1) // MOE_BLOCK
    n_rows = n_blocks * MOE_BLOCK
    row_tok = jnp.full((n_rows,), n, jnp.int32).at[dest].set(tok[order])
    row_w = jnp.zeros((n_rows,), F32).at[dest].set(wts[order])
    blk_expert = jnp.minimum(jnp.searchsorted(pad_end, jnp.arange(n_blocks) * MOE_BLOCK, side='right'), N_EXPERTS - 1)
    t_pad = jnp.concatenate([t, jnp.zeros((1, shape[-1]), t.dtype)], axis=0)
    xs = t_pad[row_tok].reshape(n_blocks, MOE_BLOCK, shape[-1])

    def expert_block(args):
        xb, e = args
        hid = jax.nn.silu(xb @ w_gate[e]) * (xb @ w_up[e])
        return hid @ w_down[e]

    ys = lax.map(expert_block, (xs, blk_expert)).reshape(n_rows, shape[-1])
    out = jax.ops.segment_sum(ys * row_w[:, None].astype(ys.dtype), row_tok, num_segments=n + 1)[:n]
    return out.reshape(shape)


def setup_inputs(seed: int = 0) -> dict:
    key = jax.random.key(seed)
    ks = iter(jax.random.split(key, 48))
    nrm = lambda shape, scale: jax.random.normal(next(ks), shape, F32) * scale
    D = D_MODEL
    x = nrm((BATCH, SEQ, D), 1.0)
    c = nrm((BATCH, D), 1.0)
    ctx = nrm((BATCH, CTX_LEN, D), 1.0)
    c_ctx = nrm((D,), 1.0)
    mod_w = nrm((DEPTH, D, 6 * D), 0.5 * D ** -0.5)
    mod_b = nrm((DEPTH, 6 * D), 0.02)
    norm1_g = 1.0 + nrm((DEPTH, D), 0.02)
    norm2_g = 1.0 + nrm((DEPTH, D), 0.02)
    w_in = nrm((DEPTH, D, IN_W), D ** -0.5)
    dn_conv = nrm((DEPTH, CONV_W, 3 * DN_W), CONV_W ** -0.5)
    dn_a_log = jnp.log(jax.random.uniform(next(ks), (DEPTH, 2, DN_HEADS), F32, 1.0, 16.0))
    dt = jnp.exp(jax.random.uniform(next(ks), (DEPTH, 2, DN_HEADS), F32, np.log(1e-3), np.log(1e-1)))
    dn_dt_bias = dt + jnp.log(-jnp.expm1(-dt))
    dn_norm_g = 1.0 + nrm((DEPTH, DN_DV), 0.02)
    dn_out = nrm((DEPTH, DN_W, D), DN_W ** -0.5)
    lru_conv = nrm((DEPTH, CONV_W, LRU_WIDTH), CONV_W ** -0.5)
    lru_conv_b = nrm((DEPTH, LRU_WIDTH), 0.02)
    lru_wa = nrm((DEPTH, 2, LRU_BLOCKS, LRU_BW, LRU_BW), LRU_BW ** -0.5)
    lru_ba = nrm((DEPTH, 2, LRU_WIDTH), 0.02)
    lru_wx = nrm((DEPTH, 2, LRU_BLOCKS, LRU_BW, LRU_BW), LRU_BW ** -0.5)
    lru_bx = nrm((DEPTH, 2, LRU_WIDTH), 0.02)
    a_c = jax.random.uniform(next(ks), (DEPTH, 2, LRU_WIDTH), F32, 0.9, 0.999) ** (1.0 / LRU_C)
    lru_lambda = jnp.log(a_c) - jnp.log1p(-a_c)
    lru_out = nrm((DEPTH, LRU_WIDTH, D), LRU_WIDTH ** -0.5)
    att_qn_g = 1.0 + nrm((DEPTH, ATT_HD), 0.02)
    att_kn_g = 1.0 + nrm((DEPTH, ATT_HD), 0.02)
    att_sink = nrm((DEPTH, ATT_HEADS), 0.5)
    att_out = nrm((DEPTH, ATT_Q_W, D), ATT_Q_W ** -0.5)
    w_o = nrm((DEPTH, D, D), D ** -0.5)
    moe_w_grp = nrm((DEPTH, D, N_GROUPS), D ** -0.5)
    moe_b_grp = nrm((DEPTH, N_GROUPS), 0.01)
    moe_w_exp = nrm((DEPTH, D, N_EXPERTS), D ** -0.5)
    moe_b_exp = nrm((DEPTH, N_EXPERTS), 0.01)
    moe_w_gate = nrm((DEPTH, N_EXPERTS, D, D_EXPERT), D ** -0.5)
    moe_w_up = nrm((DEPTH, N_EXPERTS, D, D_EXPERT), D ** -0.5)
    moe_w_down = nrm((DEPTH, N_EXPERTS, D_EXPERT, D), D_EXPERT ** -0.5)
    return {'x': x, 'c': c, 'ctx': ctx, 'c_ctx': c_ctx, 'mod_w': mod_w, 'mod_b': mod_b,
            'norm1_g': norm1_g, 'norm2_g': norm2_g, 'w_in': w_in, 'dn_conv': dn_conv,
            'dn_a_log': dn_a_log, 'dn_dt_bias': dn_dt_bias, 'dn_norm_g': dn_norm_g, 'dn_out': dn_out,
            'lru_conv': lru_conv, 'lru_conv_b': lru_conv_b, 'lru_wa': lru_wa, 'lru_ba': lru_ba,
            'lru_wx': lru_wx, 'lru_bx': lru_bx, 'lru_lambda': lru_lambda, 'lru_out': lru_out,
            'att_qn_g': att_qn_g, 'att_kn_g': att_kn_g, 'att_sink': att_sink, 'att_out': att_out,
            'w_o': w_o, 'moe_w_grp': moe_w_grp, 'moe_b_grp': moe_b_grp, 'moe_w_exp': moe_w_exp,
            'moe_b_exp': moe_b_exp, 'moe_w_gate': moe_w_gate, 'moe_w_up': moe_w_up, 'moe_w_down': moe_w_down}


def reference(x, c, ctx, c_ctx, mod_w, mod_b, norm1_g, norm2_g, w_in, dn_conv, dn_a_log, dn_dt_bias,
              dn_norm_g, dn_out, lru_conv, lru_conv_b, lru_wa, lru_ba, lru_wx, lru_bx, lru_lambda, lru_out,
              att_qn_g, att_kn_g, att_sink, att_out, w_o, moe_w_grp, moe_b_grp, moe_w_exp, moe_b_exp,
              moe_w_gate, moe_w_up, moe_w_down):
    n_ctx = ctx.shape[1]
    zc = ctx
    for l in range(DEPTH):
        last = l == DEPTH - 1
        mx = [m[:, None, :] for m in jnp.split(jax.nn.silu(c) @ mod_w[l] + mod_b[l], 6, axis=-1)]
        mc = jnp.split(jax.nn.silu(c_ctx) @ mod_w[l] + mod_b[l], 6, axis=-1)
        h = jnp.concatenate([modulate(rmsnorm(zc, norm1_g[l]), mc[0], mc[1]),
                             modulate(rmsnorm(x, norm1_g[l]), mx[0], mx[1])], axis=1)
        mix = hybrid_mixer(h, n_ctx, last, w_in[l], dn_conv[l], dn_a_log[l], dn_dt_bias[l], dn_norm_g[l], dn_out[l],
                           lru_conv[l], lru_conv_b[l], lru_wa[l], lru_ba[l], lru_wx[l], lru_bx[l], lru_lambda[l],
                           lru_out[l], att_qn_g[l], att_kn_g[l], att_sink[l], att_out[l], w_o[l])
        moe_args = (moe_w_grp[l], moe_b_grp[l], moe_w_exp[l], moe_b_exp[l], moe_w_gate[l], moe_w_up[l], moe_w_down[l])
        if last:
            x = x + mx[2] * mix
            x = x + mx[5] * hier_moe(modulate(rmsnorm(x, norm2_g[l]), mx[3], mx[4]), *moe_args)
        else:
            zc = zc + mc[2] * mix[:, :n_ctx]
            x = x + mx[2] * mix[:, n_ctx:]
            h2 = jnp.concatenate([modulate(rmsnorm(zc, norm2_g[l]), mc[3], mc[4]),
                                  modulate(rmsnorm(x, norm2_g[l]), mx[3], mx[4])], axis=1)
            f = hier_moe(h2, *moe_args)
            zc = zc + mc[5] * f[:, :n_ctx]
            x = x + mx[5] * f[:, n_ctx:]
    return x
```

```python
import functools

import jax
import jax.numpy as jnp
from jax import lax
from jax.experimental import pallas as pl
from jax.experimental.pallas import tpu as pltpu

F32 = jnp.float32
BF16 = jnp.bfloat16

EPS = 1e-6
D_MODEL = 2048
N_HEADS = 8
HEAD_DIM = 128
MIX_W = N_HEADS * HEAD_DIM
KV_HEADS = 2
KV_W = KV_HEADS * HEAD_DIM
DN_CHUNK = 64
CONV_W = 4
LRU_C = 8.0
ATT_WINDOW = 128
GRID_W = 64
ROPE_THETA = 10000.0
N_GROUPS = 8
EXPERTS_PER_GROUP = 8
N_EXPERTS = N_GROUPS * EXPERTS_PER_GROUP
TOP_K = 2
D_EXPERT = 512
MOE_BLOCK = 128
ROW_BLK = 256
NEG_BIG = -1e30
VMEM_LIMIT = 56 * 1024 * 1024


def _cparams(*sem):
    return pltpu.CompilerParams(dimension_semantics=sem, vmem_limit_bytes=VMEM_LIMIT)


def _bdot(a, b):
    return jnp.dot(a.astype(BF16), b.astype(BF16), preferred_element_type=F32)


def _bdot_nt(a, b):
    return lax.dot_general(a.astype(BF16), b.astype(BF16), (((1,), (1,)), ((), ())),
                           preferred_element_type=F32)


def _dot01(m01, x):
    m = m01.astype(BF16)
    x0 = x.astype(BF16)
    r1 = x - x0.astype(F32)
    x1 = r1.astype(BF16)
    x2 = (r1 - x1.astype(F32)).astype(BF16)
    dot = lambda t: jnp.dot(m, t, preferred_element_type=F32)
    return dot(x0) + dot(x1) + dot(x2)


def _silu(x):
    return x * jax.nn.sigmoid(x)


def _softplus(x):
    return jnp.maximum(x, 0.0) + jnp.log1p(jnp.exp(-jnp.abs(x)))


class _Rows:
    def __init__(self, batch, seq, n_ctx):
        assert seq % ROW_BLK == 0 and n_ctx % ROW_BLK == 0
        self.batch, self.seq, self.n_ctx = batch, seq, n_ctx
        self.nlat = seq // ROW_BLK
        self.nctx = n_ctx // ROW_BLK
        self.n_lat_rows = batch * seq
        self.n_rows = batch * (seq + n_ctx)
        self.steps = self.nlat + self.nctx

    def seq_block(self, b, t, reverse):
        if reverse:
            jc, jl = self.nctx - 1 - t, self.nlat - 1 - (t - self.nctx)
        else:
            jc, jl = t, t - self.nctx
        return jnp.where(t < self.nctx, self.batch * self.nlat + b * self.nctx + jc, b * self.nlat + jl)

    def mod_index(self, tile, tm):
        r0 = tile * tm
        return jnp.where(r0 < self.n_lat_rows, 1 + r0 // self.seq, 0)

    def tile_rows(self, cap):
        tm = cap
        while self.seq % tm or (self.batch * self.n_ctx) % tm:
            tm //= 2
        return tm


def _modtab_kernel(c_ref, w_ref, b_ref, o_ref):
    o_ref[0] = _bdot(_silu(c_ref[...]), w_ref[0]) + b_ref[0]


def _mod_tables(cc, mod_w, mod_b):
    depth, d, n = mod_w.shape
    tn = 1024
    return pl.pallas_call(
        _modtab_kernel,
        grid=(depth, n // tn),
        in_specs=[pl.BlockSpec((8, d), lambda l, j: (0, 0)),
                  pl.BlockSpec((1, d, tn), lambda l, j: (l, 0, j)),
                  pl.BlockSpec((1, 1, tn), lambda l, j: (l, 0, j))],
        out_specs=pl.BlockSpec((1, 8, tn), lambda l, j: (l, 0, j)),
        out_shape=jax.ShapeDtypeStruct((depth, 8, n), F32),
        compiler_params=_cparams("parallel", "parallel"),
        name="mod_tables",
    )(cc, mod_w, mod_b.reshape(depth, 1, n))


def _norm_mm_kernel(z_ref, g_ref, mod_ref, w_ref, o_ref, *rest, shift_row, scale_row, emit_h):
    h_scr = rest[-1]
    tm = z_ref.shape[0]
    sub = min(tm, 128)

    @pl.when(pl.program_id(1) == 0)
    def _():
        g = g_ref[...]
        m = mod_ref[0]
        scale1 = 1.0 + m[scale_row:scale_row + 1]
        shift = m[shift_row:shift_row + 1]

        def body(i, carry):
            r0 = pl.multiple_of(i * sub, sub)
            z = z_ref[pl.ds(r0, sub), :]
            y = z * lax.rsqrt(jnp.mean(z * z, axis=-1, keepdims=True) + EPS) * g
            h_scr[pl.ds(r0, sub), :] = (y * scale1 + shift).astype(BF16)
            return carry

        lax.fori_loop(0, tm // sub, body, 0)

    h = h_scr[...]
    o_ref[...] = jnp.dot(h, w_ref[...].astype(BF16), preferred_element_type=F32)
    if emit_h:
        @pl.when(pl.program_id(1) == 0)
        def _():
            rest[0][...] = h


def _norm_matmul(rows, z, g, modtab, w, *, shift_row, scale_row, col_lo, n_cols, tn, tm_cap=1024,
                 n_rows=None, emit_h=False, name="norm_matmul"):
    n_rows = rows.n_rows if n_rows is None else n_rows
    d = z.shape[1]
    tm = rows.tile_rows(tm_cap)
    assert n_rows % tm == 0 and n_cols % tn == 0 and col_lo % tn == 0
    c0 = col_lo // tn
    out_shape = [jax.ShapeDtypeStruct((n_rows, n_cols), F32)]
    out_specs = [pl.BlockSpec((tm, tn), lambda i, j: (i, j))]
    if emit_h:
        out_shape.append(jax.ShapeDtypeStruct((n_rows, d), BF16))
        out_specs.append(pl.BlockSpec((tm, d), lambda i, j: (i, 0)))
    res = pl.pallas_call(
        functools.partial(_norm_mm_kernel, shift_row=shift_row, scale_row=scale_row, emit_h=emit_h),
        grid=(n_rows // tm, n_cols // tn),
        in_specs=[pl.BlockSpec((tm, d), lambda i, j: (i, 0)),
                  pl.BlockSpec((1, d), lambda i, j: (0, 0)),
                  pl.BlockSpec((1, 8, d), lambda i, j: (rows.mod_index(i, tm), 0, 0)),
                  pl.BlockSpec((d, tn), lambda i, j: (0, c0 + j))],
        out_specs=out_specs,
        out_shape=out_shape,
        scratch_shapes=[pltpu.VMEM((tm, d), BF16)],
        compiler_params=_cparams("parallel", "arbitrary"),
        name=name,
    )(z, g.reshape(1, d), modtab, w)
    return res if emit_h else res[0]


def _halo_flags(rows, blk):
    is_lat = blk < rows.batch * rows.nlat
    j = jnp.where(is_lat, blk % rows.nlat, (blk - rows.batch * rows.nlat) % rows.nctx)
    nseg = jnp.where(is_lat, rows.nlat, rows.nctx)
    return (j != 0).astype(F32), (j != nseg - 1).astype(F32)


def _conv4(prev8, cur, next8, w, use_prev, use_next):
    n = cur.shape[0]
    x = jnp.concatenate([prev8 * use_prev, cur, next8 * use_next], axis=0)
    tot = n + 16
    y = w[1:2] * cur
    y = y + w[0:1] * pltpu.roll(x, 1, 0)[8:8 + n]
    y = y + w[2:3] * pltpu.roll(x, tot - 1, 0)[8:8 + n]
    y = y + w[3:4] * pltpu.roll(x, tot - 2, 0)[8:8 + n]
    return y


def _halo_specs(rows, width, col_blk, blk_of):
    per = ROW_BLK // 8
    last8 = rows.n_rows // 8 - 1
    return [
        pl.BlockSpec((8, width), lambda *g: (jnp.maximum(blk_of(*g) * per - 1, 0), col_blk(*g))),
        pl.BlockSpec((ROW_BLK, width), lambda *g: (blk_of(*g), col_blk(*g))),
        pl.BlockSpec((8, width), lambda *g: (jnp.minimum((blk_of(*g) + 1) * per, last8), col_blk(*g))),
    ]


def _dn_prep_kernel(prev_ref, cur_ref, next_ref, w_ref, o_ref, *, rows):
    blk = pl.program_id(0)
    kind = pl.program_id(1)
    use_prev, use_next = _halo_flags(rows, blk)
    y = _silu(_conv4(prev_ref[...], cur_ref[...], next_ref[...], w_ref[...], use_prev, use_next))
    q_scale = jnp.where(kind == 0, HEAD_DIM ** -0.5, 1.0)
    for h in range(N_HEADS):
        sl = slice(h * HEAD_DIM, (h + 1) * HEAD_DIM)
        yh = y[:, sl]
        inv = lax.rsqrt(jnp.sum(yh * yh, axis=-1, keepdims=True) + EPS)
        o_ref[:, sl] = yh * (jnp.where(kind == 2, 1.0, inv) * q_scale)


def _dn_prep(rows, p_a, conv_w):
    return pl.pallas_call(
        functools.partial(_dn_prep_kernel, rows=rows),
        grid=(rows.n_rows // ROW_BLK, 3),
        in_specs=_halo_specs(rows, MIX_W, lambda i, j: j, lambda i, j: i)
        + [pl.BlockSpec((CONV_W, MIX_W), lambda i, j: (0, j))],
        out_specs=pl.BlockSpec((ROW_BLK, MIX_W), lambda i, j: (i, j)),
        out_shape=jax.ShapeDtypeStruct((rows.n_rows, 3 * MIX_W), F32),
        compiler_params=_cparams("parallel", "parallel"),
        name="dn_prep",
    )(p_a, p_a, p_a, conv_w)


def _dn_scan_kernel(q_ref, k_ref, v_ref, beta_ref, alpha_ref, par_ref, o_ref,
                    s_scr, u_scr, w_scr, qd_scr, kdt_scr, aqk_scr, egt_scr, *, reverse):
    n = ROW_BLK
    nchunk = n // DN_CHUNK

    @pl.when(pl.program_id(1) == 0)
    def _():
        s_scr[...] = jnp.zeros_like(s_scr)

    row = lax.broadcasted_iota(jnp.int32, (n, n), 0)
    col = lax.broadcasted_iota(jnp.int32, (n, n), 1)

    def same(s):
        sh = s.bit_length() - 1
        return (row >> sh) == (col >> sh)

    same_chunk = same(DN_CHUNK)
    if reverse:
        incl = same_chunk & (col >= row)
        strict = same_chunk & (col > row)
    else:
        incl = same_chunk & (col <= row)
        strict = same_chunk & (col < row)
    eye = (row == col).astype(F32)

    par = par_ref[0]
    beta_all = jax.nn.sigmoid(beta_ref[...])
    g_all = -jnp.exp(par[0:1]) * _softplus(alpha_ref[...] + par[1:2])
    gc_all = _dot01(incl.astype(F32), g_all)
    gt_all = _dot01(same_chunk.astype(F32), g_all)
    gc_t = gc_all.T
    egc_all = jnp.exp(gc_all)
    ekd_all = jnp.exp(gt_all - gc_all)
    egt_scr[...] = jnp.exp(gt_all)

    for h in range(N_HEADS):
        sl = slice(h * HEAD_DIM, (h + 1) * HEAD_DIM)
        q, k, v = q_ref[:, sl], k_ref[:, sl], v_ref[:, sl]
        bcol = beta_all[:, h:h + 1]
        egc = egc_all[:, h:h + 1]
        dec = jnp.exp(jnp.where(incl, gc_all[:, h:h + 1] - gc_t[h:h + 1, :], NEG_BIG))
        kb = k * bcol
        a = jnp.where(strict, _bdot_nt(kb, k) * dec, 0.0)
        d1 = jnp.where(same(8), a, 0.0)
        d2 = _bdot(d1, d1)
        d4 = _bdot(d2, d2)
        x = _bdot(_bdot(eye - d1, eye + d2), eye + d4)
        for s in (8, 16, 32):
            off = jnp.where(same(2 * s) & jnp.logical_not(same(s)), a, 0.0)
            x = x - _bdot(x, _bdot(off, x))
        u_scr[h] = _bdot(x, v * bcol)
        w_scr[h] = _bdot(x, kb * egc)
        qd_scr[h] = q * egc
        kd = k * ekd_all[:, h:h + 1]
        for c in range(nchunk):
            cs = slice(c * DN_CHUNK, (c + 1) * DN_CHUNK)
            kdt_scr[h, c] = kd[cs].T.astype(BF16)
            aqk_scr[h, cs, :] = _bdot_nt(q[cs], k[cs]) * dec[cs, cs]

    def chunk_step(ci, carry):
        c = nchunk - 1 - ci if reverse else ci
        r0 = pl.multiple_of(c * DN_CHUNK, DN_CHUNK)
        eg = egt_scr[pl.ds(r0, 1), :]
        for h in range(N_HEADS):
            s_h = s_scr[h]
            v_new = u_scr[h, pl.ds(r0, DN_CHUNK), :] - _bdot(w_scr[h, pl.ds(r0, DN_CHUNK), :], s_h)
            o = _bdot(qd_scr[h, pl.ds(r0, DN_CHUNK), :], s_h) + _bdot(aqk_scr[h, pl.ds(r0, DN_CHUNK), :], v_new)
            o_ref[pl.ds(r0, DN_CHUNK), h * HEAD_DIM:(h + 1) * HEAD_DIM] = o
            s_scr[h] = s_h * eg[:, h:h + 1] + jnp.dot(kdt_scr[h, c], v_new.astype(BF16),
                                                      preferred_element_type=F32)
        return carry

    lax.fori_loop(0, nchunk, chunk_step, 0)


def _dn_scan(rows, qkv, p_s, par, *, reverse):
    d = 1 if reverse else 0
    blk = lambda b, t: rows.seq_block(b, t, reverse)
    nchunk = ROW_BLK // DN_CHUNK
    return pl.pallas_call(
        functools.partial(_dn_scan_kernel, reverse=reverse),
        grid=(rows.batch, rows.steps),
        in_specs=[pl.BlockSpec((ROW_BLK, MIX_W), lambda b, t: (blk(b, t), 0)),
                  pl.BlockSpec((ROW_BLK, MIX_W), lambda b, t: (blk(b, t), 1)),
                  pl.BlockSpec((ROW_BLK, MIX_W), lambda b, t: (blk(b, t), 2)),
                  pl.BlockSpec((ROW_BLK, 128), lambda b, t: (blk(b, t), d)),
                  pl.BlockSpec((ROW_BLK, 128), lambda b, t: (blk(b, t), 2 + d)),
                  pl.BlockSpec((1, 8, 128), lambda b, t: (d, 0, 0))],
        out_specs=pl.BlockSpec((ROW_BLK, MIX_W), lambda b, t: (blk(b, t), 0)),
        out_shape=jax.ShapeDtypeStruct((rows.n_rows, MIX_W), F32),
        scratch_shapes=[pltpu.VMEM((N_HEADS, HEAD_DIM, HEAD_DIM), F32),
                        pltpu.VMEM((N_HEADS, ROW_BLK, HEAD_DIM), F32),
                        pltpu.VMEM((N_HEADS, ROW_BLK, HEAD_DIM), F32),
                        pltpu.VMEM((N_HEADS, ROW_BLK, HEAD_DIM), F32),
                        pltpu.VMEM((N_HEADS, nchunk, HEAD_DIM, DN_CHUNK), BF16),
                        pltpu.VMEM((N_HEADS, ROW_BLK, DN_CHUNK), F32),
                        pltpu.VMEM((ROW_BLK, 128), F32)],
        compiler_params=_cparams("parallel", "arbitrary"),
        name="dn_scan_bwd" if reverse else "dn_scan_fwd",
    )(qkv, qkv, qkv, p_s, p_s, par)


def _lru_kernel(prev_ref, cur_ref, next_ref, cw_ref, cb_ref, wa_ref, wx_ref, ba_ref, bx_ref, lam_ref,
                o_ref, h_scr, a_scr, u_scr, *, rows, reverse):
    b, t = pl.program_id(0), pl.program_id(1)

    @pl.when(t == 0)
    def _():
        h_scr[...] = jnp.zeros_like(h_scr)

    use_prev, use_next = _halo_flags(rows, rows.seq_block(b, t, reverse))
    xc = _conv4(prev_ref[...], cur_ref[...], next_ref[...], cw_ref[...], use_prev, use_next) + cb_ref[...]
    for kb in range(N_HEADS):
        sl = slice(kb * HEAD_DIM, (kb + 1) * HEAD_DIM)
        xb = xc[:, sl]
        r = jax.nn.sigmoid(_bdot(xb, wa_ref[0, kb]) + ba_ref[0, :, sl])
        i = jax.nn.sigmoid(_bdot(xb, wx_ref[0, kb]) + bx_ref[0, :, sl])
        log_a = -LRU_C * r * _softplus(-lam_ref[0, :, sl])
        a_scr[:, sl] = jnp.exp(log_a)
        u_scr[:, sl] = jnp.sqrt(1.0 - jnp.exp(2.0 * log_a)) * (i * xb)

    ngrp = ROW_BLK // 8
    sub = lax.broadcasted_iota(jnp.int32, (8, MIX_W), 0)

    def group_step(gi, h_prev):
        g = ngrp - 1 - gi if reverse else gi
        r0 = pl.multiple_of(g * 8, 8)
        a = a_scr[pl.ds(r0, 8), :]
        u = u_scr[pl.ds(r0, 8), :]
        for dist in (1, 2, 4):
            if reverse:
                a_sh, u_sh, m = pltpu.roll(a, 8 - dist, 0), pltpu.roll(u, 8 - dist, 0), sub < 8 - dist
            else:
                a_sh, u_sh, m = pltpu.roll(a, dist, 0), pltpu.roll(u, dist, 0), sub >= dist
            u = jnp.where(m, a * u_sh + u, u)
            a = jnp.where(m, a * a_sh, a)
        h = u + a * h_prev
        o_ref[pl.ds(r0, 8), :] = h
        return h[0:1] if reverse else h[7:8]

    h_scr[...] = lax.fori_loop(0, ngrp, group_step, h_scr[...])


def _lru_scan(rows, p_b, cw, cb, wa, wx, ba, bx, lam, *, reverse):
    d = 1 if reverse else 0
    blk = lambda b, t: rows.seq_block(b, t, reverse)
    vec = lambda: pl.BlockSpec((1, 1, MIX_W), lambda b, t: (d, 0, 0))
    mat = lambda: pl.BlockSpec((1, N_HEADS, HEAD_DIM, HEAD_DIM), lambda b, t: (d, 0, 0, 0))
    return pl.pallas_call(
        functools.partial(_lru_kernel, rows=rows, reverse=reverse),
        grid=(rows.batch, rows.steps),
        in_specs=_halo_specs(rows, MIX_W, lambda b, t: 0, blk)
        + [pl.BlockSpec((CONV_W, MIX_W), lambda b, t: (0, 0)),
           pl.BlockSpec((1, MIX_W), lambda b, t: (0, 0)),
           mat(), mat(), vec(), vec(), vec()],
        out_specs=pl.BlockSpec((ROW_BLK, MIX_W), lambda b, t: (blk(b, t), 0)),
        out_shape=jax.ShapeDtypeStruct((rows.n_rows, MIX_W), F32),
        scratch_shapes=[pltpu.VMEM((1, MIX_W), F32),
                        pltpu.VMEM((ROW_BLK, MIX_W), F32),
                        pltpu.VMEM((ROW_BLK, MIX_W), F32)],
        compiler_params=_cparams("parallel", "arbitrary"),
        name="lru_bwd" if reverse else "lru_fwd",
    )(p_b, p_b, p_b, cw, cb.reshape(1, MIX_W), wa, wx,
      ba.reshape(2, 1, MIX_W), bx.reshape(2, 1, MIX_W), lam.reshape(2, 1, MIX_W))


def _rope_tables(seq, n_ctx):
    half = HEAD_DIM // 2
    pos = jnp.arange(seq)
    inv = ROPE_THETA ** (-jnp.arange(0, half, 2, dtype=F32) / half)
    ang_r = (pos // GRID_W).astype(F32)[:, None] * inv
    ang_c = (pos % GRID_W).astype(F32)[:, None] * inv
    cos = jnp.concatenate([jnp.cos(ang_r)] * 2 + [jnp.cos(ang_c)] * 2, axis=-1)
    sin = jnp.concatenate([-jnp.sin(ang_r), jnp.sin(ang_r), -jnp.sin(ang_c), jnp.sin(ang_c)], axis=-1)
    cos = jnp.concatenate([cos, jnp.ones((n_ctx, HEAD_DIM), F32)], axis=0)
    sin = jnp.concatenate([sin, jnp.zeros((n_ctx, HEAD_DIM), F32)], axis=0)
    return cos, sin


def _qk_prep_kernel(q_ref, k_ref, cos_ref, sin_ref, qg_ref, kg_ref, qo_ref, ko_ref):
    cos, sin = cos_ref[...], sin_ref[...]
    lane = lax.broadcasted_iota(jnp.int32, cos.shape, 1)
    first = (lane & (HEAD_DIM // 2 - 1)) < (HEAD_DIM // 4)

    def norm_rope(x, g):
        y = x * lax.rsqrt(jnp.mean(x * x, axis=-1, keepdims=True) + EPS) * g
        partner = jnp.where(first, pltpu.roll(y, HEAD_DIM - HEAD_DIM // 4, 1), pltpu.roll(y, HEAD_DIM // 4, 1))
        return y * cos + partner * sin

    for h in range(N_HEADS):
        sl = slice(h * HEAD_DIM, (h + 1) * HEAD_DIM)
        qo_ref[:, sl] = (norm_rope(q_ref[:, sl], qg_ref[...]) * (HEAD_DIM ** -0.5)).astype(BF16)
    for h in range(KV_HEADS):
        sl = slice(h * HEAD_DIM, (h + 1) * HEAD_DIM)
        ko_ref[:, sl] = norm_rope(k_ref[:, sl], kg_ref[...]).astype(BF16)


def _qk_prep(rows, p_b, cos, sin, qn_g, kn_g, q_col, k_col):
    def tab(i):
        return jnp.where(i < rows.batch * rows.nlat, i % rows.nlat,
                         rows.nlat + (i - rows.batch * rows.nlat) % rows.nctx)

    return pl.pallas_call(
        _qk_prep_kernel,
        grid=(rows.n_rows // ROW_BLK,),
        in_specs=[pl.BlockSpec((ROW_BLK, MIX_W), lambda i: (i, q_col // MIX_W)),
                  pl.BlockSpec((ROW_BLK, KV_W), lambda i: (i, k_col // KV_W)),
                  pl.BlockSpec((ROW_BLK, HEAD_DIM), lambda i: (tab(i), 0)),
                  pl.BlockSpec((ROW_BLK, HEAD_DIM), lambda i: (tab(i), 0)),
                  pl.BlockSpec((1, HEAD_DIM), lambda i: (0, 0)),
                  pl.BlockSpec((1, HEAD_DIM), lambda i: (0, 0))],
        out_specs=[pl.BlockSpec((ROW_BLK, MIX_W), lambda i: (i, 0)),
                   pl.BlockSpec((ROW_BLK, KV_W), lambda i: (i, 0))],
        out_shape=[jax.ShapeDtypeStruct((rows.n_rows, MIX_W), BF16),
                   jax.ShapeDtypeStruct((rows.n_rows, KV_W), BF16)],
        compiler_params=_cparams("parallel"),
        name="qk_prep",
    )(p_b, p_b, cos, sin, qn_g.reshape(1, HEAD_DIM), kn_g.reshape(1, HEAD_DIM))


def _attn_core(q, keys, vals, sink_ref, valid, o_ref):
    grp = N_HEADS // KV_HEADS
    nq = q.shape[0]
    for kvh in range(KV_HEADS):
        sl = slice(kvh * HEAD_DIM, (kvh + 1) * HEAD_DIM)
        kk = jnp.concatenate([t[:, sl] for t in keys], axis=0)
        vv = jnp.concatenate([t[:, sl].astype(BF16) for t in vals], axis=0)
        q4 = jnp.concatenate([q[:, (kvh * grp + g) * HEAD_DIM:(kvh * grp + g + 1) * HEAD_DIM]
                              for g in range(grp)], axis=0)
        s = _bdot_nt(q4, kk)
        if valid is not None:
            s = jnp.where(valid, s, NEG_BIG)
        sink = jnp.concatenate([jnp.broadcast_to(sink_ref[kvh * grp + g:kvh * grp + g + 1, 0:1], (nq, 1))
                                for g in range(grp)], axis=0)
        m = jnp.maximum(jnp.max(s, axis=-1, keepdims=True), sink)
        e = jnp.exp(s - m)
        p = e / (jnp.sum(e, axis=-1, keepdims=True) + jnp.exp(sink - m))
        o = _bdot(p, vv)
        for g in range(grp):
            hq = kvh * grp + g
            o_ref[:, hq * HEAD_DIM:(hq + 1) * HEAD_DIM] = o[g * nq:(g + 1) * nq].astype(BF16)


def _attn_latent_kernel(q_ref, k0_ref, k1_ref, k2_ref, kc_ref, v0_ref, v1_ref, v2_ref, vc_ref, sink_ref, o_ref,
                        *, seq):
    n = pl.program_id(1)
    blk = q_ref.shape[0]
    n_loc = 3 * blk
    n_ctx = kc_ref.shape[0]
    grp = N_HEADS // KV_HEADS
    qi = lax.broadcasted_iota(jnp.int32, (grp * blk, n_loc + n_ctx), 0) & (blk - 1)
    kj = lax.broadcasted_iota(jnp.int32, (grp * blk, n_loc + n_ctx), 1)
    rel = kj - blk
    kpos = n * blk + rel
    local_ok = (jnp.abs(qi - rel) <= ATT_WINDOW) & (kpos >= 0) & (kpos < seq)
    valid = local_ok | (kj >= n_loc)
    _attn_core(q_ref[...], [k0_ref[...], k1_ref[...], k2_ref[...], kc_ref[...]],
               [v0_ref[...], v1_ref[...], v2_ref[...], vc_ref[...]], sink_ref, valid, o_ref)


def _attn_ctx_kernel(q_ref, kc_ref, vc_ref, sink_ref, o_ref):
    _attn_core(q_ref[...], [kc_ref[...]], [vc_ref[...]], sink_ref, None, o_ref)


def _attention(rows, qn, kn, p_b, v_col, sink_b, *, with_ctx_queries):
    blk = 128
    nb = rows.seq // blk
    vcb = v_col // KV_W
    ctx_blk0 = rows.n_lat_rows // rows.n_ctx
    assert rows.n_lat_rows % rows.n_ctx == 0
    kspec = lambda off: pl.BlockSpec((blk, KV_W), lambda b, n: (b * nb + jnp.clip(n + off, 0, nb - 1), 0))
    vspec = lambda off: pl.BlockSpec((blk, KV_W), lambda b, n: (b * nb + jnp.clip(n + off, 0, nb - 1), vcb))
    n_out = rows.n_rows if with_ctx_queries else rows.n_lat_rows
    y_lat = pl.pallas_call(
        functools.partial(_attn_latent_kernel, seq=rows.seq),
        grid=(rows.batch, nb),
        in_specs=[pl.BlockSpec((blk, MIX_W), lambda b, n: (b * nb + n, 0)),
                  kspec(-1), kspec(0), kspec(1),
                  pl.BlockSpec((rows.n_ctx, KV_W), lambda b, n: (ctx_blk0 + b, 0)),
                  vspec(-1), vspec(0), vspec(1),
                  pl.BlockSpec((rows.n_ctx, KV_W), lambda b, n: (ctx_blk0 + b, vcb)),
                  pl.BlockSpec((8, 128), lambda b, n: (0, 0))],
        out_specs=pl.BlockSpec((blk, MIX_W), lambda b, n: (b * nb + n, 0)),
        out_shape=jax.ShapeDtypeStruct((n_out, MIX_W), BF16),
        compiler_params=_cparams("parallel", "parallel"),
        name="attn_latent",
    )(qn, kn, kn, kn, kn, p_b, p_b, p_b, p_b, sink_b)
    if not with_ctx_queries:
        return y_lat
    ncb = rows.n_ctx // blk
    q0 = rows.n_lat_rows // blk
    return pl.pallas_call(
        lambda q_ref, kc_ref, vc_ref, sink_ref, y_in_ref, o_ref: _attn_ctx_kernel(q_ref, kc_ref, vc_ref, sink_ref, o_ref),
        grid=(rows.batch, ncb),
        in_specs=[pl.BlockSpec((blk, MIX_W), lambda b, i: (q0 + b * ncb + i, 0)),
                  pl.BlockSpec((rows.n_ctx, KV_W), lambda b, i: (ctx_blk0 + b, 0)),
                  pl.BlockSpec((rows.n_ctx, KV_W), lambda b, i: (ctx_blk0 + b, vcb)),
                  pl.BlockSpec((8, 128), lambda b, i: (0, 0)),
                  pl.BlockSpec(memory_space=pl.ANY)],
        out_specs=pl.BlockSpec((blk, MIX_W), lambda b, i: (q0 + b * ncb + i, 0)),
        out_shape=jax.ShapeDtypeStruct((n_out, MIX_W), BF16),
        input_output_aliases={4: 0},
        compiler_params=_cparams("parallel", "parallel"),
        name="attn_ctx",
    )(qn, kn, p_b, sink_b, y_lat)


def _merge_kernel(of_ref, ob_ref, z_ref, hf_ref, hb_ref, lg_ref, yc_ref, ga_ref, gb_ref, gc_ref,
                  wa_ref, wb_ref, wc_ref, ng_ref, o_ref, ya_scr, yb_scr):
    tm = of_ref.shape[0]
    sub = min(tm, 128)

    @pl.when(pl.program_id(1) == 0)
    def _():
        ng = ng_ref[...]

        def body(i, carry):
            r0 = pl.multiple_of(i * sub, sub)
            rs = pl.ds(r0, sub)
            for h in range(N_HEADS):
                sl = slice(h * HEAD_DIM, (h + 1) * HEAD_DIM)
                o = of_ref[rs, sl] + ob_ref[rs, sl]
                y = o * lax.rsqrt(jnp.mean(o * o, axis=-1, keepdims=True) + EPS) * ng
                ya_scr[rs, sl] = (y * _silu(z_ref[rs, sl])).astype(BF16)
            yb_scr[rs, :] = (jax.nn.gelu(lg_ref[rs, :]) * (hf_ref[rs, :] + hb_ref[rs, :])).astype(BF16)
            return carry

        lax.fori_loop(0, tm // sub, body, 0)

    acc = jax.nn.sigmoid(ga_ref[...]) * _bdot(ya_scr[...], wa_ref[...])
    acc = acc + jax.nn.sigmoid(gb_ref[...]) * _bdot(yb_scr[...], wb_ref[...])
    acc = acc + jax.nn.sigmoid(gc_ref[...]) * _bdot(yc_ref[...], wc_ref[...])
    o_ref[...] = acc.astype(BF16)


def _merge(rows, o_f, o_b, p_a, hs_f, hs_b, p_b, y_c, dn_out, lru_out, att_out, dn_norm_g, *,
           z_col, lg_col, gate_col, n_rows):
    tm = rows.tile_rows(512)
    tn = 512
    assert n_rows % tm == 0 and gate_col % tn == 0
    row = lambda cb: pl.BlockSpec((tm, MIX_W), lambda i, j: (i, cb))
    gate = lambda br: pl.BlockSpec((tm, tn), lambda i, j: (i, (gate_col + br * D_MODEL) // tn + j))
    wgt = lambda: pl.BlockSpec((MIX_W, tn), lambda i, j: (0, j))
    return pl.pallas_call(
        _merge_kernel,
        grid=(n_rows // tm, D_MODEL // tn),
        in_specs=[row(0), row(0), row(z_col // MIX_W), row(0), row(0), row(lg_col // MIX_W), row(0),
                  gate(0), gate(1), gate(2), wgt(), wgt(), wgt(),
                  pl.BlockSpec((1, HEAD_DIM), lambda i, j: (0, 0))],
        out_specs=pl.BlockSpec((tm, tn), lambda i, j: (i, j)),
        out_shape=jax.ShapeDtypeStruct((n_rows, D_MODEL), BF16),
        scratch_shapes=[pltpu.VMEM((tm, MIX_W), BF16), pltpu.VMEM((tm, MIX_W), BF16)],
        compiler_params=_cparams("parallel", "arbitrary"),
        name="merge",
    )(o_f, o_b, p_a, hs_f, hs_b, p_b, y_c, p_b, p_b, p_b, dn_out, lru_out, att_out,
      dn_norm_g.reshape(1, HEAD_DIM))


def _mm_residual_kernel(a_ref, w_ref, z_ref, mod_ref, o_ref, *, gate_row):
    acc = _bdot(a_ref[...], w_ref[...])
    o_ref[...] = z_ref[...] + mod_ref[0][gate_row:gate_row + 1] * acc


def _mm_residual(rows, a, w, z, modtab, *, gate_row, n_rows):
    tm = rows.tile_rows(1024)
    tn = 512
    k, n = w.shape
    assert n_rows % tm == 0
    return pl.pallas_call(
        functools.partial(_mm_residual_kernel, gate_row=gate_row),
        grid=(n_rows // tm, n // tn),
        in_specs=[pl.BlockSpec((tm, k), lambda i, j: (i, 0)),
                  pl.BlockSpec((k, tn), lambda i, j: (0, j)),
                  pl.BlockSpec((tm, tn), lambda i, j: (i, j)),
                  pl.BlockSpec((1, 8, tn), lambda i, j: (rows.mod_index(i, tm), 0, j))],
        out_specs=pl.BlockSpec((tm, tn), lambda i, j: (i, j)),
        out_shape=jax.ShapeDtypeStruct((n_rows, n), F32),
        compiler_params=_cparams("parallel", "parallel"),
        name="out_proj",
    )(a, w, z, modtab)


def _expert_kernel(be_ref, nu_ref, x_ref, wg_ref, wu_ref, wd_ref, o_ref):
    i = pl.program_id(0)

    @pl.when(i < nu_ref[0])
    def _():
        x = x_ref[...]
        hid = _silu(_bdot(x, wg_ref[0])) * _bdot(x, wu_ref[0])
        o_ref[...] = _bdot(hid, wd_ref[0])

    @pl.when(i >= nu_ref[0])
    def _():
        o_ref[...] = jnp.zeros_like(o_ref)


def _expert_blocks(xs, blk_expert, n_used, w_gate, w_up, w_down):
    n_rows, d = xs.shape
    n_blocks = n_rows // MOE_BLOCK
    grid_spec = pltpu.PrefetchScalarGridSpec(
        num_scalar_prefetch=2,
        grid=(n_blocks,),
        in_specs=[pl.BlockSpec((MOE_BLOCK, d), lambda i, be, nu: (i, 0)),
                  pl.BlockSpec((1, d, D_EXPERT), lambda i, be, nu: (be[i], 0, 0)),
                  pl.BlockSpec((1, d, D_EXPERT), lambda i, be, nu: (be[i], 0, 0)),
                  pl.BlockSpec((1, D_EXPERT, d), lambda i, be, nu: (be[i], 0, 0))],
        out_specs=pl.BlockSpec((MOE_BLOCK, d), lambda i, be, nu: (i, 0)),
    )
    return pl.pallas_call(
        _expert_kernel,
        grid_spec=grid_spec,
        out_shape=jax.ShapeDtypeStruct((n_rows, d), F32),
        compiler_params=_cparams("arbitrary"),
        name="moe_experts",
    )(blk_expert, n_used, xs, w_gate, w_up, w_down)


def _hier_moe(rows, z, norm_g, modtab, w_grp, b_grp, w_exp, b_exp, w_gate, w_up, w_down, *, n_rows):
    d = z.shape[1]
    n_logit = N_GROUPS + N_EXPERTS
    w_route = jnp.concatenate([w_grp, w_exp, jnp.zeros((d, 128 - n_logit), F32)], axis=1)
    logits, h2 = _norm_matmul(rows, z, norm_g, modtab, w_route, shift_row=3, scale_row=4, col_lo=0, n_cols=128,
                              tn=128, tm_cap=512, n_rows=n_rows, emit_h=True, name="moe_route")
    grp_logits = logits[:, :N_GROUPS] + b_grp
    exp_logits = (logits[:, N_GROUPS:n_logit] + b_exp).reshape(n_rows, N_GROUPS, EXPERTS_PER_GROUP)
    g_idx = jnp.argmax(grp_logits, axis=-1)
    p_grp = jnp.max(jax.nn.softmax(grp_logits, axis=-1), axis=-1, keepdims=True)
    in_grp = jnp.take_along_axis(exp_logits, g_idx[:, None, None], axis=1)[:, 0]
    top_val, top_idx = lax.top_k(in_grp, TOP_K)
    wts = p_grp * jax.nn.softmax(top_val, axis=-1)
    eid = (g_idx[:, None] * EXPERTS_PER_GROUP + top_idx).reshape(-1).astype(jnp.int32)
    n_assign = n_rows * TOP_K
    order = jnp.argsort(eid)
    e_sorted = eid[order]
    counts = jnp.bincount(eid, length=N_EXPERTS)
    padded = (counts + MOE_BLOCK - 1) // MOE_BLOCK * MOE_BLOCK
    pad_end = jnp.cumsum(padded)
    pad_start = pad_end - padded
    start = jnp.cumsum(counts) - counts
    dest = (pad_start[e_sorted] + (jnp.arange(n_assign) - start[e_sorted])).astype(jnp.int32)
    n_blocks = (n_assign + N_EXPERTS * (MOE_BLOCK - 1) + MOE_BLOCK - 1) // MOE_BLOCK
    n_pad_rows = n_blocks * MOE_BLOCK
    tok_sorted = (order // TOP_K).astype(jnp.int32)
    row_tok = jnp.full((n_pad_rows,), n_rows, jnp.int32).at[dest].set(tok_sorted)
    blk_expert = jnp.minimum(jnp.searchsorted(pad_end, jnp.arange(n_blocks) * MOE_BLOCK, side='right'),
                             N_EXPERTS - 1).astype(jnp.int32)
    n_used = (pad_end[-1] // MOE_BLOCK).astype(jnp.int32).reshape(1)
    h2_pad = jnp.concatenate([h2, jnp.zeros((1, d), h2.dtype)], axis=0)
    xs = h2_pad[row_tok]
    ys = _expert_blocks(xs, blk_expert, n_used, w_gate, w_up, w_down)
    pos = jnp.zeros((n_assign,), jnp.int32).at[order].set(dest).reshape(n_rows, TOP_K)
    out = ys[pos[:, 0]] * wts[:, 0:1] + ys[pos[:, 1]] * wts[:, 1:2]
    return out


def kernel(x, c, ctx, c_ctx, mod_w, mod_b, norm1_g, norm2_g, w_in, dn_conv, dn_a_log, dn_dt_bias, dn_norm_g, dn_out, lru_conv, lru_conv_b, lru_wa, lru_ba, lru_wx, lru_bx, lru_lambda, lru_out, att_qn_g, att_kn_g, att_sink, att_out, w_o, moe_w_grp, moe_b_grp, moe_w_exp, moe_b_exp, moe_w_gate, moe_w_up, moe_w_down):
    batch, seq, d = x.shape
    n_ctx = ctx.shape[1]
    depth = mod_w.shape[0]
    rows = _Rows(batch, seq, n_ctx)
    assert d == D_MODEL and batch + 1 <= 8

    z = jnp.concatenate([x.reshape(batch * seq, d), ctx.reshape(batch * n_ctx, d)], axis=0)

    cc = jnp.concatenate([c_ctx[None], c, jnp.zeros((7 - batch, d), F32)], axis=0)
    mods = _mod_tables(cc, mod_w, mod_b).reshape(depth, 8, 6, d)
    mods = jnp.concatenate([mods, jnp.zeros((depth, 8, 2, d), F32)], axis=2)

    cos, sin = _rope_tables(seq, n_ctx)

    qkvz_w = 4 * MIX_W
    small_lo = qkvz_w
    rest_lo = small_lo + 4 * N_HEADS
    rest_w = 3 * MIX_W + 2 * KV_W + 3 * D_MODEL
    lg_col, aq_col, ak_col = MIX_W, 2 * MIX_W, 3 * MIX_W
    av_col, gate_col = ak_col + KV_W, ak_col + 2 * KV_W

    for l in range(depth):
        last = l == depth - 1
        modtab = mods[l]
        n_out = rows.n_lat_rows if last else rows.n_rows

        w_small = w_in[l][:, small_lo:rest_lo]
        w_small = jnp.pad(w_small.reshape(d, 4, 1, N_HEADS), ((0, 0), (0, 0), (0, 0), (0, 128 - N_HEADS))
                          ).reshape(d, 4 * 128)
        w_rest = w_in[l][:, rest_lo:]
        p_a = _norm_matmul(rows, z, norm1_g[l], modtab, w_in[l], shift_row=0, scale_row=1, col_lo=0,
                           n_cols=qkvz_w, tn=512, name="in_proj_a")
        p_s = _norm_matmul(rows, z, norm1_g[l], modtab, w_small, shift_row=0, scale_row=1, col_lo=0,
                           n_cols=4 * 128, tn=512, name="in_proj_s")
        p_b = _norm_matmul(rows, z, norm1_g[l], modtab, w_rest, shift_row=0, scale_row=1, col_lo=0,
                           n_cols=rest_w, tn=512, name="in_proj_b")

        qkv = _dn_prep(rows, p_a, dn_conv[l])
        par = jnp.zeros((2, 8, 128), F32)
        par = par.at[:, 0, :N_HEADS].set(dn_a_log[l]).at[:, 1, :N_HEADS].set(dn_dt_bias[l])
        o_f = _dn_scan(rows, qkv, p_s, par, reverse=False)
        o_b = _dn_scan(rows, qkv, p_s, par, reverse=True)

        lru_args = (lru_conv[l], lru_conv_b[l], lru_wa[l], lru_wx[l], lru_ba[l], lru_bx[l], lru_lambda[l])
        hs_f = _lru_scan(rows, p_b, *lru_args, reverse=False)
        hs_b = _lru_scan(rows, p_b, *lru_args, reverse=True)

        qn, kn = _qk_prep(rows, p_b, cos, sin, att_qn_g[l], att_kn_g[l], aq_col, ak_col)
        sink_b = jnp.broadcast_to(att_sink[l][:, None], (N_HEADS, 128))
        y_c = _attention(rows, qn, kn, p_b, av_col, sink_b, with_ctx_queries=not last)

        merged = _merge(rows, o_f, o_b, p_a, hs_f, hs_b, p_b, y_c, dn_out[l], lru_out[l], att_out[l],
                        dn_norm_g[l], z_col=3 * MIX_W, lg_col=lg_col, gate_col=gate_col, n_rows=n_out)
        z = _mm_residual(rows, merged, w_o[l], z, modtab, gate_row=2, n_rows=n_out)

        f = _hier_moe(rows, z, norm2_g[l], modtab, moe_w_grp[l], moe_b_grp[l], moe_w_exp[l], moe_b_exp[l],
                      moe_w_gate[l], moe_w_up[l], moe_w_down[l], n_rows=n_out)
        gate2 = jnp.concatenate([jnp.repeat(modtab[1:1 + batch, 5], seq, axis=0),
                                 jnp.broadcast_to(modtab[0, 5], (batch * n_ctx, d))], axis=0)[:n_out]
        z = z + gate2 * f

    return z[:rows.n_lat_rows].reshape(batch, seq, d)
```

```python
import functools

import jax
import jax.numpy as jnp
from jax import lax
from jax.experimental import pallas as pl
from jax.experimental.pallas import tpu as pltpu

F32 = jnp.float32
BF16 = jnp.bfloat16

EPS = 1e-6
D_MODEL = 2048
N_HEADS = 8
HEAD_DIM = 128
MIX_W = N_HEADS * HEAD_DIM
KV_HEADS = 2
KV_W = KV_HEADS * HEAD_DIM
DN_CHUNK = 64
DN_UNIT = 128
CONV_W = 4
LRU_C = 8.0
ATT_WINDOW = 128
GRID_W = 64
ROPE_THETA = 10000.0
N_GROUPS = 8
EXPERTS_PER_GROUP = 8
N_EXPERTS = N_GROUPS * EXPERTS_PER_GROUP
TOP_K = 2
D_EXPERT = 512
MOE_BLOCK = 128
ROW_BLK = 256
NEG_BIG = -1e30
VMEM_LIMIT = 56 * 1024 * 1024


def _cparams(*sem):
    return pltpu.CompilerParams(dimension_semantics=sem, vmem_limit_bytes=VMEM_LIMIT)


def _bdot(a, b):
    return jnp.dot(a.astype(BF16), b.astype(BF16), preferred_element_type=F32)


def _bdot_nt(a, b):
    return lax.dot_general(a.astype(BF16), b.astype(BF16), (((1,), (1,)), ((), ())),
                           preferred_element_type=F32)


def _dot01(m01, x):
    m = m01.astype(BF16)
    x0 = x.astype(BF16)
    r1 = x - x0.astype(F32)
    x1 = r1.astype(BF16)
    x2 = (r1 - x1.astype(F32)).astype(BF16)
    dot = lambda t: jnp.dot(m, t, preferred_element_type=F32)
    return dot(x0) + dot(x1) + dot(x2)


def _silu(x):
    return x * jax.nn.sigmoid(x)


def _softplus(x):
    return jnp.maximum(x, 0.0) + jnp.log1p(jnp.exp(-jnp.abs(x)))


class _Rows:
    def __init__(self, batch, seq, n_ctx):
        assert seq % ROW_BLK == 0 and n_ctx % ROW_BLK == 0
        self.batch, self.seq, self.n_ctx = batch, seq, n_ctx
        self.nlat = seq // ROW_BLK
        self.nctx = n_ctx // ROW_BLK
        self.n_lat_rows = batch * seq
        self.n_rows = batch * (seq + n_ctx)
        self.steps = self.nlat + self.nctx

    def seq_block(self, b, t, reverse):
        if reverse:
            jc, jl = self.nctx - 1 - t, self.nlat - 1 - (t - self.nctx)
        else:
            jc, jl = t, t - self.nctx
        return jnp.where(t < self.nctx, self.batch * self.nlat + b * self.nctx + jc, b * self.nlat + jl)

    def mod_index(self, tile, tm):
        r0 = tile * tm
        return jnp.where(r0 < self.n_lat_rows, 1 + r0 // self.seq, 0)

    def tile_rows(self, cap):
        tm = cap
        while self.seq % tm or (self.batch * self.n_ctx) % tm:
            tm //= 2
        return tm


def _modtab_kernel(c_ref, w_ref, b_ref, o_ref):
    o_ref[0] = _bdot(_silu(c_ref[...]), w_ref[0]) + b_ref[0]


def _mod_tables(cc, mod_w, mod_b):
    depth, d, n = mod_w.shape
    tn = 1024
    return pl.pallas_call(
        _modtab_kernel,
        grid=(depth, n // tn),
        in_specs=[pl.BlockSpec((8, d), lambda l, j: (0, 0)),
                  pl.BlockSpec((1, d, tn), lambda l, j: (l, 0, j)),
                  pl.BlockSpec((1, 1, tn), lambda l, j: (l, 0, j))],
        out_specs=pl.BlockSpec((1, 8, tn), lambda l, j: (l, 0, j)),
        out_shape=jax.ShapeDtypeStruct((depth, 8, n), F32),
        compiler_params=_cparams("parallel", "parallel"),
        name="mod_tables",
    )(cc, mod_w, mod_b.reshape(depth, 1, n))


class _ProjCols:
    tile = 512
    small = 4 * N_HEADS
    qkvz = 4 * MIX_W
    rest = 3 * MIX_W + 2 * KV_W + 3 * D_MODEL
    z, lx, lg, aq = 3 * MIX_W, 4 * MIX_W, 5 * MIX_W, 6 * MIX_W
    ak = aq + MIX_W
    av = ak + KV_W
    gates = av + KV_W
    scalars = qkvz + rest
    total = scalars + 4 * 128


def _relayout_kernel(a_ref, b_ref, s_ref, o_ref, *, n_aligned, n_tiles, shift):
    j = pl.program_id(1)

    @pl.when(j < n_aligned)
    def _():
        o_ref[0] = a_ref[0].astype(BF16)

    @pl.when((j >= n_aligned) & (j < n_tiles - 1))
    def _():
        o_ref[0] = jnp.concatenate([a_ref[0][:, shift:], b_ref[0][:, :shift]], axis=1).astype(BF16)

    @pl.when(j == n_tiles - 1)
    def _():
        o_ref[0] = s_ref[0].astype(BF16)


def _relayout_w_in(w_in):
    depth, d, n_in = w_in.shape
    pc = _ProjCols
    assert n_in == pc.qkvz + pc.small + pc.rest and pc.qkvz % pc.tile == 0 and pc.rest % pc.tile == 0
    n_aligned = pc.qkvz // pc.tile
    n_tiles = pc.total // pc.tile
    per = pc.tile // 128
    w_small = w_in[:, :, pc.qkvz:pc.qkvz + pc.small].reshape(depth, d, 4, N_HEADS)
    w_small = jnp.pad(w_small, ((0, 0), (0, 0), (0, 0), (0, 128 - N_HEADS))).reshape(depth, d, 4 * 128)
    return pl.pallas_call(
        functools.partial(_relayout_kernel, n_aligned=n_aligned, n_tiles=n_tiles, shift=pc.small),
        grid=(depth, n_tiles),
        in_specs=[pl.BlockSpec((1, d, pc.tile), lambda l, j: (l, 0, jnp.minimum(j, n_tiles - 2))),
                  pl.BlockSpec((1, d, 128), lambda l, j: (l, 0, jnp.minimum(j + 1, n_tiles - 1) * per)),
                  pl.BlockSpec((1, d, pc.tile), lambda l, j: (l, 0, 0))],
        out_specs=pl.BlockSpec((1, d, pc.tile), lambda l, j: (l, 0, j)),
        out_shape=jax.ShapeDtypeStruct((depth, d, pc.total), BF16),
        compiler_params=_cparams("parallel", "parallel"),
        name="relayout_w_in",
    )(w_in, w_in, w_small)


def _norm_mm_kernel(z_ref, g_ref, mod_ref, w_ref, o_ref, *rest, shift_row, scale_row, emit_h):
    h_scr = rest[-1]
    tm = z_ref.shape[0]
    sub = min(tm, 128)

    @pl.when(pl.program_id(1) == 0)
    def _():
        g = g_ref[...]
        m = mod_ref[0]
        scale1 = 1.0 + m[scale_row:scale_row + 1]
        shift = m[shift_row:shift_row + 1]

        def body(i, carry):
            r0 = pl.multiple_of(i * sub, sub)
            z = z_ref[pl.ds(r0, sub), :]
            y = z * lax.rsqrt(jnp.mean(z * z, axis=-1, keepdims=True) + EPS) * g
            h_scr[pl.ds(r0, sub), :] = (y * scale1 + shift).astype(BF16)
            return carry

        lax.fori_loop(0, tm // sub, body, 0)

    h = h_scr[...]
    o_ref[...] = jnp.dot(h, w_ref[...].astype(BF16), preferred_element_type=F32)
    if emit_h:
        @pl.when(pl.program_id(1) == 0)
        def _():
            rest[0][...] = h


def _norm_matmul(rows, z, g, modtab, w, *, shift_row, scale_row, tn, layer=None, tm_cap=1024,
                 n_rows=None, emit_h=False, name="norm_matmul"):
    n_rows = rows.n_rows if n_rows is None else n_rows
    d = z.shape[1]
    n_cols = w.shape[-1]
    tm = rows.tile_rows(tm_cap)
    assert n_rows % tm == 0 and n_cols % tn == 0
    if layer is None:
        w_spec = pl.BlockSpec((d, tn), lambda i, j: (0, j))
    else:
        w_spec = pl.BlockSpec((None, d, tn), lambda i, j: (layer, 0, j))
    out_shape = [jax.ShapeDtypeStruct((n_rows, n_cols), F32)]
    out_specs = [pl.BlockSpec((tm, tn), lambda i, j: (i, j))]
    if emit_h:
        out_shape.append(jax.ShapeDtypeStruct((n_rows, d), BF16))
        out_specs.append(pl.BlockSpec((tm, d), lambda i, j: (i, 0)))
    res = pl.pallas_call(
        functools.partial(_norm_mm_kernel, shift_row=shift_row, scale_row=scale_row, emit_h=emit_h),
        grid=(n_rows // tm, n_cols // tn),
        in_specs=[pl.BlockSpec((tm, d), lambda i, j: (i, 0)),
                  pl.BlockSpec((1, d), lambda i, j: (0, 0)),
                  pl.BlockSpec((1, 8, d), lambda i, j: (rows.mod_index(i, tm), 0, 0)),
                  w_spec],
        out_specs=out_specs,
        out_shape=out_shape,
        scratch_shapes=[pltpu.VMEM((tm, d), BF16)],
        compiler_params=_cparams("parallel", "arbitrary"),
        name=name,
    )(z, g.reshape(1, d), modtab, w)
    return res if emit_h else res[0]


def _halo_flags(rows, blk):
    is_lat = blk < rows.batch * rows.nlat
    j = jnp.where(is_lat, blk % rows.nlat, (blk - rows.batch * rows.nlat) % rows.nctx)
    nseg = jnp.where(is_lat, rows.nlat, rows.nctx)
    return (j != 0).astype(F32), (j != nseg - 1).astype(F32)


def _conv4(prev8, cur, next8, w, use_prev, use_next):
    n = cur.shape[0]
    x = jnp.concatenate([prev8 * use_prev, cur, next8 * use_next], axis=0)
    tot = n + 16
    y = w[1:2] * cur
    y = y + w[0:1] * pltpu.roll(x, 1, 0)[8:8 + n]
    y = y + w[2:3] * pltpu.roll(x, tot - 1, 0)[8:8 + n]
    y = y + w[3:4] * pltpu.roll(x, tot - 2, 0)[8:8 + n]
    return y


def _halo_specs(rows, width, col_blk, blk_of):
    per = ROW_BLK // 8
    last8 = rows.n_rows // 8 - 1
    return [
        pl.BlockSpec((8, width), lambda *g: (jnp.maximum(blk_of(*g) * per - 1, 0), col_blk(*g))),
        pl.BlockSpec((ROW_BLK, width), lambda *g: (blk_of(*g), col_blk(*g))),
        pl.BlockSpec((8, width), lambda *g: (jnp.minimum((blk_of(*g) + 1) * per, last8), col_blk(*g))),
    ]


def _dn_prep_kernel(prev_ref, cur_ref, next_ref, w_ref, o_ref, *, rows):
    blk = pl.program_id(0)
    kind = pl.program_id(1)
    use_prev, use_next = _halo_flags(rows, blk)
    y = _silu(_conv4(prev_ref[...], cur_ref[...], next_ref[...], w_ref[...], use_prev, use_next))
    q_scale = jnp.where(kind == 0, HEAD_DIM ** -0.5, 1.0)
    for h in range(N_HEADS):
        sl = slice(h * HEAD_DIM, (h + 1) * HEAD_DIM)
        yh = y[:, sl]
        inv = lax.rsqrt(jnp.sum(yh * yh, axis=-1, keepdims=True) + EPS)
        o_ref[:, sl] = yh * (jnp.where(kind == 2, 1.0, inv) * q_scale)


def _dn_prep(rows, p_a, conv_w):
    return pl.pallas_call(
        functools.partial(_dn_prep_kernel, rows=rows),
        grid=(rows.n_rows // ROW_BLK, 3),
        in_specs=_halo_specs(rows, MIX_W, lambda i, j: j, lambda i, j: i)
        + [pl.BlockSpec((CONV_W, MIX_W), lambda i, j: (0, j))],
        out_specs=pl.BlockSpec((ROW_BLK, MIX_W), lambda i, j: (i, j)),
        out_shape=jax.ShapeDtypeStruct((rows.n_rows, 3 * MIX_W), F32),
        compiler_params=_cparams("parallel", "parallel"),
        name="dn_prep",
    )(p_a, p_a, p_a, conv_w)


def _dot_b(a, b):
    return jnp.dot(a, b, preferred_element_type=F32).astype(BF16)


def _dn_scan_kernel(q_ref, k_ref, v_ref, beta_ref, alpha_ref, par_ref, o_ref,
                    s_scr, u_scr, wq_scr, kdt_scr, aqk_scr, *, reverse):
    n = ROW_BLK
    nchunk = n // DN_CHUNK
    per_unit = DN_UNIT // DN_CHUNK

    @pl.when(pl.program_id(1) == 0)
    def _():
        s_scr[...] = jnp.zeros_like(s_scr)

    def tri(m):
        row = lax.broadcasted_iota(jnp.int32, (m, m), 0)
        col = lax.broadcasted_iota(jnp.int32, (m, m), 1)
        same = lambda s: (row >> (s.bit_length() - 1)) == (col >> (s.bit_length() - 1))
        ahead = (col >= row) if reverse else (col <= row)
        return row, col, same, same(DN_CHUNK) & ahead, same(DN_CHUNK) & ahead & (row != col)

    _, _, same_n, incl_n, _ = tri(n)
    par = par_ref[0]
    beta_all = jax.nn.sigmoid(beta_ref[...])
    g_all = -jnp.exp(par[0:1]) * _softplus(alpha_ref[...] + par[1:2])
    gc_all = _dot01(incl_n.astype(F32), g_all)
    gt_all = _dot01(same_n(DN_CHUNK).astype(F32), g_all)
    gc_t = gc_all.T
    egc_all = jnp.exp(gc_all)
    ekd_all = jnp.exp(gt_all - gc_all)
    egt_all = jnp.exp(gt_all)

    row_u, col_u, same_u, incl_u, strict_u = tri(DN_UNIT)
    one_b = lambda m: jnp.where(m, 1.0, 0.0).astype(BF16)
    eye_b = one_b(row_u == col_u)
    diag8_b = one_b(same_u(8))
    off_b = {s: one_b(same_u(2 * s) & jnp.logical_not(same_u(s))) for s in (8, 16, 32)}

    units = [(h, j) for h in range(N_HEADS) for j in range(n // DN_UNIT)]
    a_b, rhs_b = [], []
    for h, j in units:
        rs = slice(j * DN_UNIT, (j + 1) * DN_UNIT)
        sl = slice(h * HEAD_DIM, (h + 1) * HEAD_DIM)
        q, k, v = q_ref[rs, sl], k_ref[rs, sl], v_ref[rs, sl]
        bcol = beta_all[rs, h:h + 1]
        egc = egc_all[rs, h:h + 1]
        dec = jnp.exp(jnp.where(incl_u, gc_all[rs, h:h + 1] - gc_t[h:h + 1, rs], NEG_BIG))
        kb = k * bcol
        k_b = k.astype(BF16)
        a_b.append(jnp.where(strict_u, _bdot_nt(kb, k_b) * dec, 0.0).astype(BF16))
        aqk_scr[h, rs, :] = (_bdot_nt(q, k_b) * dec).astype(BF16)
        rhs_b.append(jnp.concatenate([v * bcol, kb * egc], axis=1).astype(BF16))
        qd = (q * egc).astype(BF16)
        kd = k * ekd_all[rs, h:h + 1]
        for c in range(per_unit):
            cs = slice(c * DN_CHUNK, (c + 1) * DN_CHUNK)
            wq_scr[h, j * per_unit + c, DN_CHUNK:, :] = qd[cs]
            kdt_scr[h, j * per_unit + c] = kd[cs].T.astype(BF16)

    d1 = [a * diag8_b for a in a_b]
    d2 = [_dot_b(d, d) for d in d1]
    d4 = [_dot_b(d, d) for d in d2]
    x = [_dot_b(eye_b - d, eye_b + e) for d, e in zip(d1, d2)]
    x = [_dot_b(xx, eye_b + e) for xx, e in zip(x, d4)]
    for s in (8, 16, 32):
        t = [_dot_b(a * off_b[s], xx) for a, xx in zip(a_b, x)]
        x = [xx - _dot_b(xx, tt) for xx, tt in zip(x, t)]
    for (h, j), xx, rhs in zip(units, x, rhs_b):
        uw = jnp.dot(xx, rhs, preferred_element_type=F32)
        u_scr[h, j * DN_UNIT:(j + 1) * DN_UNIT, :] = uw[:, :HEAD_DIM]
        for c in range(per_unit):
            wq_scr[h, j * per_unit + c, :DN_CHUNK, :] = uw[c * DN_CHUNK:(c + 1) * DN_CHUNK, HEAD_DIM:].astype(BF16)

    zeros_b = jnp.zeros((DN_CHUNK, HEAD_DIM), BF16)
    for ci in range(nchunk):
        c = nchunk - 1 - ci if reverse else ci
        rs = slice(c * DN_CHUNK, (c + 1) * DN_CHUNK)
        for h in range(N_HEADS):
            s_h = s_scr[h]
            r = jnp.dot(wq_scr[h, c], s_h.astype(BF16), preferred_element_type=F32)
            v_new = (u_scr[h, rs, :] - r[:DN_CHUNK]).astype(BF16)
            v_unit = jnp.concatenate([v_new, zeros_b] if c % per_unit == 0 else [zeros_b, v_new], axis=0)
            o = r[DN_CHUNK:] + jnp.dot(aqk_scr[h, rs, :], v_unit, preferred_element_type=F32)
            o_ref[rs, h * HEAD_DIM:(h + 1) * HEAD_DIM] = o
            s_scr[h] = s_h * egt_all[c * DN_CHUNK:c * DN_CHUNK + 1, h:h + 1] + jnp.dot(
                kdt_scr[h, c], v_new, preferred_element_type=F32)


def _dn_scan(rows, qkv, p, par, *, small_col, reverse):
    d = 1 if reverse else 0
    blk = lambda b, t: rows.seq_block(b, t, reverse)
    nchunk = ROW_BLK // DN_CHUNK
    sc = small_col // 128
    return pl.pallas_call(
        functools.partial(_dn_scan_kernel, reverse=reverse),
        grid=(rows.batch, rows.steps),
        in_specs=[pl.BlockSpec((ROW_BLK, MIX_W), lambda b, t: (blk(b, t), 0)),
                  pl.BlockSpec((ROW_BLK, MIX_W), lambda b, t: (blk(b, t), 1)),
                  pl.BlockSpec((ROW_BLK, MIX_W), lambda b, t: (blk(b, t), 2)),
                  pl.BlockSpec((ROW_BLK, 128), lambda b, t: (blk(b, t), sc + d)),
                  pl.BlockSpec((ROW_BLK, 128), lambda b, t: (blk(b, t), sc + 2 + d)),
                  pl.BlockSpec((1, 8, 128), lambda b, t: (d, 0, 0))],
        out_specs=pl.BlockSpec((ROW_BLK, MIX_W), lambda b, t: (blk(b, t), 0)),
        out_shape=jax.ShapeDtypeStruct((rows.n_rows, MIX_W), F32),
        scratch_shapes=[pltpu.VMEM((N_HEADS, HEAD_DIM, HEAD_DIM), F32),
                        pltpu.VMEM((N_HEADS, ROW_BLK, HEAD_DIM), F32),
                        pltpu.VMEM((N_HEADS, nchunk, 2 * DN_CHUNK, HEAD_DIM), BF16),
                        pltpu.VMEM((N_HEADS, nchunk, HEAD_DIM, DN_CHUNK), BF16),
                        pltpu.VMEM((N_HEADS, ROW_BLK, DN_UNIT), BF16)],
        compiler_params=_cparams("parallel", "arbitrary"),
        name="dn_scan_bwd" if reverse else "dn_scan_fwd",
    )(qkv, qkv, qkv, p, p, par)


def _lru_kernel(prev_ref, cur_ref, next_ref, cw_ref, cb_ref, wa_ref, wx_ref, ba_ref, bx_ref, lam_ref,
                o_ref, h_scr, a_scr, u_scr, *, rows, reverse):
    b, t = pl.program_id(0), pl.program_id(1)

    @pl.when(t == 0)
    def _():
        h_scr[...] = jnp.zeros_like(h_scr)

    use_prev, use_next = _halo_flags(rows, rows.seq_block(b, t, reverse))
    xc = _conv4(prev_ref[...], cur_ref[...], next_ref[...], cw_ref[...], use_prev, use_next) + cb_ref[...]
    for kb in range(N_HEADS):
        sl = slice(kb * HEAD_DIM, (kb + 1) * HEAD_DIM)
        xb = xc[:, sl]
        r = jax.nn.sigmoid(_bdot(xb, wa_ref[0, kb]) + ba_ref[0, :, sl])
        i = jax.nn.sigmoid(_bdot(xb, wx_ref[0, kb]) + bx_ref[0, :, sl])
        log_a = -LRU_C * r * _softplus(-lam_ref[0, :, sl])
        a_scr[:, sl] = jnp.exp(log_a)
        u_scr[:, sl] = jnp.sqrt(1.0 - jnp.exp(2.0 * log_a)) * (i * xb)

    ngrp = ROW_BLK // 8
    sub = lax.broadcasted_iota(jnp.int32, (8, MIX_W), 0)

    def group_step(gi, h_prev):
        g = ngrp - 1 - gi if reverse else gi
        r0 = pl.multiple_of(g * 8, 8)
        a = a_scr[pl.ds(r0, 8), :]
        u = u_scr[pl.ds(r0, 8), :]
        for dist in (1, 2, 4):
            if reverse:
                a_sh, u_sh, m = pltpu.roll(a, 8 - dist, 0), pltpu.roll(u, 8 - dist, 0), sub < 8 - dist
            else:
                a_sh, u_sh, m = pltpu.roll(a, dist, 0), pltpu.roll(u, dist, 0), sub >= dist
            u = jnp.where(m, a * u_sh + u, u)
            a = jnp.where(m, a * a_sh, a)
        h = u + a * h_prev
        o_ref[pl.ds(r0, 8), :] = h
        return h[0:1] if reverse else h[7:8]

    h_scr[...] = lax.fori_loop(0, ngrp, group_step, h_scr[...])


def _lru_scan(rows, p_b, cw, cb, wa, wx, ba, bx, lam, *, x_col, reverse):
    d = 1 if reverse else 0
    blk = lambda b, t: rows.seq_block(b, t, reverse)
    vec = lambda: pl.BlockSpec((1, 1, MIX_W), lambda b, t: (d, 0, 0))
    mat = lambda: pl.BlockSpec((1, N_HEADS, HEAD_DIM, HEAD_DIM), lambda b, t: (d, 0, 0, 0))
    return pl.pallas_call(
        functools.partial(_lru_kernel, rows=rows, reverse=reverse),
        grid=(rows.batch, rows.steps),
        in_specs=_halo_specs(rows, MIX_W, lambda b, t: x_col // MIX_W, blk)
        + [pl.BlockSpec((CONV_W, MIX_W), lambda b, t: (0, 0)),
           pl.BlockSpec((1, MIX_W), lambda b, t: (0, 0)),
           mat(), mat(), vec(), vec(), vec()],
        out_specs=pl.BlockSpec((ROW_BLK, MIX_W), lambda b, t: (blk(b, t), 0)),
        out_shape=jax.ShapeDtypeStruct((rows.n_rows, MIX_W), F32),
        scratch_shapes=[pltpu.VMEM((1, MIX_W), F32),
                        pltpu.VMEM((ROW_BLK, MIX_W), F32),
                        pltpu.VMEM((ROW_BLK, MIX_W), F32)],
        compiler_params=_cparams("parallel", "arbitrary"),
        name="lru_bwd" if reverse else "lru_fwd",
    )(p_b, p_b, p_b, cw, cb.reshape(1, MIX_W), wa, wx,
      ba.reshape(2, 1, MIX_W), bx.reshape(2, 1, MIX_W), lam.reshape(2, 1, MIX_W))


def _rope_tables(seq, n_ctx):
    half = HEAD_DIM // 2
    pos = jnp.arange(seq)
    inv = ROPE_THETA ** (-jnp.arange(0, half, 2, dtype=F32) / half)
    ang_r = (pos // GRID_W).astype(F32)[:, None] * inv
    ang_c = (pos % GRID_W).astype(F32)[:, None] * inv
    cos = jnp.concatenate([jnp.cos(ang_r)] * 2 + [jnp.cos(ang_c)] * 2, axis=-1)
    sin = jnp.concatenate([-jnp.sin(ang_r), jnp.sin(ang_r), -jnp.sin(ang_c), jnp.sin(ang_c)], axis=-1)
    cos = jnp.concatenate([cos, jnp.ones((n_ctx, HEAD_DIM), F32)], axis=0)
    sin = jnp.concatenate([sin, jnp.zeros((n_ctx, HEAD_DIM), F32)], axis=0)
    return cos, sin


def _qk_prep_kernel(q_ref, k_ref, cos_ref, sin_ref, qg_ref, kg_ref, qo_ref, ko_ref):
    cos, sin = cos_ref[...], sin_ref[...]
    lane = lax.broadcasted_iota(jnp.int32, cos.shape, 1)
    first = (lane & (HEAD_DIM // 2 - 1)) < (HEAD_DIM // 4)

    def norm_rope(x, g):
        y = x * lax.rsqrt(jnp.mean(x * x, axis=-1, keepdims=True) + EPS) * g
        partner = jnp.where(first, pltpu.roll(y, HEAD_DIM - HEAD_DIM // 4, 1), pltpu.roll(y, HEAD_DIM // 4, 1))
        return y * cos + partner * sin

    for h in range(N_HEADS):
        sl = slice(h * HEAD_DIM, (h + 1) * HEAD_DIM)
        qo_ref[:, sl] = (norm_rope(q_ref[:, sl], qg_ref[...]) * (HEAD_DIM ** -0.5)).astype(BF16)
    for h in range(KV_HEADS):
        sl = slice(h * HEAD_DIM, (h + 1) * HEAD_DIM)
        ko_ref[:, sl] = norm_rope(k_ref[:, sl], kg_ref[...]).astype(BF16)


def _qk_prep(rows, p_b, cos, sin, qn_g, kn_g, q_col, k_col):
    def tab(i):
        return jnp.where(i < rows.batch * rows.nlat, i % rows.nlat,
                         rows.nlat + (i - rows.batch * rows.nlat) % rows.nctx)

    return pl.pallas_call(
        _qk_prep_kernel,
        grid=(rows.n_rows // ROW_BLK,),
        in_specs=[pl.BlockSpec((ROW_BLK, MIX_W), lambda i: (i, q_col // MIX_W)),
                  pl.BlockSpec((ROW_BLK, KV_W), lambda i: (i, k_col // KV_W)),
                  pl.BlockSpec((ROW_BLK, HEAD_DIM), lambda i: (tab(i), 0)),
                  pl.BlockSpec((ROW_BLK, HEAD_DIM), lambda i: (tab(i), 0)),
                  pl.BlockSpec((1, HEAD_DIM), lambda i: (0, 0)),
                  pl.BlockSpec((1, HEAD_DIM), lambda i: (0, 0))],
        out_specs=[pl.BlockSpec((ROW_BLK, MIX_W), lambda i: (i, 0)),
                   pl.BlockSpec((ROW_BLK, KV_W), lambda i: (i, 0))],
        out_shape=[jax.ShapeDtypeStruct((rows.n_rows, MIX_W), BF16),
                   jax.ShapeDtypeStruct((rows.n_rows, KV_W), BF16)],
        compiler_params=_cparams("parallel"),
        name="qk_prep",
    )(p_b, p_b, cos, sin, qn_g.reshape(1, HEAD_DIM), kn_g.reshape(1, HEAD_DIM))


def _attn_core(q, keys, vals, sink_ref, valid, o_ref):
    grp = N_HEADS // KV_HEADS
    nq = q.shape[0]
    for kvh in range(KV_HEADS):
        sl = slice(kvh * HEAD_DIM, (kvh + 1) * HEAD_DIM)
        kk = jnp.concatenate([t[:, sl] for t in keys], axis=0)
        vv = jnp.concatenate([t[:, sl].astype(BF16) for t in vals], axis=0)
        q4 = jnp.concatenate([q[:, (kvh * grp + g) * HEAD_DIM:(kvh * grp + g + 1) * HEAD_DIM]
                              for g in range(grp)], axis=0)
        s = _bdot_nt(q4, kk)
        if valid is not None:
            s = jnp.where(valid, s, NEG_BIG)
        sink = jnp.concatenate([jnp.broadcast_to(sink_ref[kvh * grp + g:kvh * grp + g + 1, 0:1], (nq, 1))
                                for g in range(grp)], axis=0)
        m = jnp.maximum(jnp.max(s, axis=-1, keepdims=True), sink)
        e = jnp.exp(s - m)
        p = e / (jnp.sum(e, axis=-1, keepdims=True) + jnp.exp(sink - m))
        o = _bdot(p, vv)
        for g in range(grp):
            hq = kvh * grp + g
            o_ref[:, hq * HEAD_DIM:(hq + 1) * HEAD_DIM] = o[g * nq:(g + 1) * nq].astype(BF16)


def _attn_latent_kernel(q_ref, k0_ref, k1_ref, k2_ref, kc_ref, v0_ref, v1_ref, v2_ref, vc_ref, sink_ref, o_ref,
                        *, seq):
    n = pl.program_id(1)
    blk = q_ref.shape[0]
    n_loc = 3 * blk
    n_ctx = kc_ref.shape[0]
    grp = N_HEADS // KV_HEADS
    qi = lax.broadcasted_iota(jnp.int32, (grp * blk, n_loc + n_ctx), 0) & (blk - 1)
    kj = lax.broadcasted_iota(jnp.int32, (grp * blk, n_loc + n_ctx), 1)
    rel = kj - blk
    kpos = n * blk + rel
    local_ok = (jnp.abs(qi - rel) <= ATT_WINDOW) & (kpos >= 0) & (kpos < seq)
    valid = local_ok | (kj >= n_loc)
    _attn_core(q_ref[...], [k0_ref[...], k1_ref[...], k2_ref[...], kc_ref[...]],
               [v0_ref[...], v1_ref[...], v2_ref[...], vc_ref[...]], sink_ref, valid, o_ref)


def _attn_ctx_kernel(q_ref, kc_ref, vc_ref, sink_ref, o_ref):
    _attn_core(q_ref[...], [kc_ref[...]], [vc_ref[...]], sink_ref, None, o_ref)


def _attention(rows, qn, kn, p_b, v_col, sink_b, *, with_ctx_queries):
    blk = 128
    nb = rows.seq // blk
    vcb = v_col // KV_W
    ctx_blk0 = rows.n_lat_rows // rows.n_ctx
    assert rows.n_lat_rows % rows.n_ctx == 0
    kspec = lambda off: pl.BlockSpec((blk, KV_W), lambda b, n: (b * nb + jnp.clip(n + off, 0, nb - 1), 0))
    vspec = lambda off: pl.BlockSpec((blk, KV_W), lambda b, n: (b * nb + jnp.clip(n + off, 0, nb - 1), vcb))
    n_out = rows.n_rows if with_ctx_queries else rows.n_lat_rows
    y_lat = pl.pallas_call(
        functools.partial(_attn_latent_kernel, seq=rows.seq),
        grid=(rows.batch, nb),
        in_specs=[pl.BlockSpec((blk, MIX_W), lambda b, n: (b * nb + n, 0)),
                  kspec(-1), kspec(0), kspec(1),
                  pl.BlockSpec((rows.n_ctx, KV_W), lambda b, n: (ctx_blk0 + b, 0)),
                  vspec(-1), vspec(0), vspec(1),
                  pl.BlockSpec((rows.n_ctx, KV_W), lambda b, n: (ctx_blk0 + b, vcb)),
                  pl.BlockSpec((8, 128), lambda b, n: (0, 0))],
        out_specs=pl.BlockSpec((blk, MIX_W), lambda b, n: (b * nb + n, 0)),
        out_shape=jax.ShapeDtypeStruct((n_out, MIX_W), BF16),
        compiler_params=_cparams("parallel", "parallel"),
        name="attn_latent",
    )(qn, kn, kn, kn, kn, p_b, p_b, p_b, p_b, sink_b)
    if not with_ctx_queries:
        return y_lat
    ncb = rows.n_ctx // blk
    q0 = rows.n_lat_rows // blk
    return pl.pallas_call(
        lambda q_ref, kc_ref, vc_ref, sink_ref, y_in_ref, o_ref: _attn_ctx_kernel(q_ref, kc_ref, vc_ref, sink_ref, o_ref),
        grid=(rows.batch, ncb),
        in_specs=[pl.BlockSpec((blk, MIX_W), lambda b, i: (q0 + b * ncb + i, 0)),
                  pl.BlockSpec((rows.n_ctx, KV_W), lambda b, i: (ctx_blk0 + b, 0)),
                  pl.BlockSpec((rows.n_ctx, KV_W), lambda b, i: (ctx_blk0 + b, vcb)),
                  pl.BlockSpec((8, 128), lambda b, i: (0, 0)),
                  pl.BlockSpec(memory_space=pl.ANY)],
        out_specs=pl.BlockSpec((blk, MIX_W), lambda b, i: (q0 + b * ncb + i, 0)),
        out_shape=jax.ShapeDtypeStruct((n_out, MIX_W), BF16),
        input_output_aliases={4: 0},
        compiler_params=_cparams("parallel", "parallel"),
        name="attn_ctx",
    )(qn, kn, p_b, sink_b, y_lat)


def _merge_kernel(of_ref, ob_ref, z_ref, hf_ref, hb_ref, lg_ref, yc_ref, ga_ref, gb_ref, gc_ref,
                  wa_ref, wb_ref, wc_ref, ng_ref, o_ref, ya_scr, yb_scr):
    tm = of_ref.shape[0]
    sub = min(tm, 128)

    @pl.when(pl.program_id(1) == 0)
    def _():
        ng = ng_ref[...]

        def body(i, carry):
            r0 = pl.multiple_of(i * sub, sub)
            rs = pl.ds(r0, sub)
            for h in range(N_HEADS):
                sl = slice(h * HEAD_DIM, (h + 1) * HEAD_DIM)
                o = of_ref[rs, sl] + ob_ref[rs, sl]
                y = o * lax.rsqrt(jnp.mean(o * o, axis=-1, keepdims=True) + EPS) * ng
                ya_scr[rs, sl] = (y * _silu(z_ref[rs, sl])).astype(BF16)
            yb_scr[rs, :] = (jax.nn.gelu(lg_ref[rs, :]) * (hf_ref[rs, :] + hb_ref[rs, :])).astype(BF16)
            return carry

        lax.fori_loop(0, tm // sub, body, 0)

    acc = jax.nn.sigmoid(ga_ref[...]) * _bdot(ya_scr[...], wa_ref[...])
    acc = acc + jax.nn.sigmoid(gb_ref[...]) * _bdot(yb_scr[...], wb_ref[...])
    acc = acc + jax.nn.sigmoid(gc_ref[...]) * _bdot(yc_ref[...], wc_ref[...])
    o_ref[...] = acc.astype(BF16)


def _merge(rows, o_f, o_b, p_a, hs_f, hs_b, p_b, y_c, dn_out, lru_out, att_out, dn_norm_g, *,
           z_col, lg_col, gate_col, n_rows):
    tm = rows.tile_rows(512)
    tn = 512
    assert n_rows % tm == 0 and gate_col % tn == 0
    row = lambda cb: pl.BlockSpec((tm, MIX_W), lambda i, j: (i, cb))
    gate = lambda br: pl.BlockSpec((tm, tn), lambda i, j: (i, (gate_col + br * D_MODEL) // tn + j))
    wgt = lambda: pl.BlockSpec((MIX_W, tn), lambda i, j: (0, j))
    return pl.pallas_call(
        _merge_kernel,
        grid=(n_rows // tm, D_MODEL // tn),
        in_specs=[row(0), row(0), row(z_col // MIX_W), row(0), row(0), row(lg_col // MIX_W), row(0),
                  gate(0), gate(1), gate(2), wgt(), wgt(), wgt(),
                  pl.BlockSpec((1, HEAD_DIM), lambda i, j: (0, 0))],
        out_specs=pl.BlockSpec((tm, tn), lambda i, j: (i, j)),
        out_shape=jax.ShapeDtypeStruct((n_rows, D_MODEL), BF16),
        scratch_shapes=[pltpu.VMEM((tm, MIX_W), BF16), pltpu.VMEM((tm, MIX_W), BF16)],
        compiler_params=_cparams("parallel", "arbitrary"),
        name="merge",
    )(o_f, o_b, p_a, hs_f, hs_b, p_b, y_c, p_b, p_b, p_b, dn_out, lru_out, att_out,
      dn_norm_g.reshape(1, HEAD_DIM))


def _mm_residual_kernel(a_ref, w_ref, z_ref, mod_ref, o_ref, *, gate_row):
    acc = _bdot(a_ref[...], w_ref[...])
    o_ref[...] = z_ref[...] + mod_ref[0][gate_row:gate_row + 1] * acc


def _mm_residual(rows, a, w, z, modtab, *, gate_row, n_rows):
    tm = rows.tile_rows(1024)
    tn = 512
    k, n = w.shape
    assert n_rows % tm == 0
    return pl.pallas_call(
        functools.partial(_mm_residual_kernel, gate_row=gate_row),
        grid=(n_rows // tm, n // tn),
        in_specs=[pl.BlockSpec((tm, k), lambda i, j: (i, 0)),
                  pl.BlockSpec((k, tn), lambda i, j: (0, j)),
                  pl.BlockSpec((tm, tn), lambda i, j: (i, j)),
                  pl.BlockSpec((1, 8, tn), lambda i, j: (rows.mod_index(i, tm), 0, j))],
        out_specs=pl.BlockSpec((tm, tn), lambda i, j: (i, j)),
        out_shape=jax.ShapeDtypeStruct((n_rows, n), F32),
        compiler_params=_cparams("parallel", "parallel"),
        name="out_proj",
    )(a, w, z, modtab)


def _expert_kernel(be_ref, nu_ref, x_ref, wg_ref, wu_ref, wd_ref, o_ref):
    i = pl.program_id(0)

    @pl.when(i < nu_ref[0])
    def _():
        x = x_ref[...]
        hid = _silu(_bdot(x, wg_ref[0])) * _bdot(x, wu_ref[0])
        o_ref[...] = _bdot(hid, wd_ref[0])

    @pl.when(i >= nu_ref[0])
    def _():
        o_ref[...] = jnp.zeros_like(o_ref)


def _expert_blocks(xs, blk_expert, n_used, w_gate, w_up, w_down):
    n_rows, d = xs.shape
    n_blocks = n_rows // MOE_BLOCK
    grid_spec = pltpu.PrefetchScalarGridSpec(
        num_scalar_prefetch=2,
        grid=(n_blocks,),
        in_specs=[pl.BlockSpec((MOE_BLOCK, d), lambda i, be, nu: (i, 0)),
                  pl.BlockSpec((1, d, D_EXPERT), lambda i, be, nu: (be[i], 0, 0)),
                  pl.BlockSpec((1, d, D_EXPERT), lambda i, be, nu: (be[i], 0, 0)),
                  pl.BlockSpec((1, D_EXPERT, d), lambda i, be, nu: (be[i], 0, 0))],
        out_specs=pl.BlockSpec((MOE_BLOCK, d), lambda i, be, nu: (i, 0)),
    )
    return pl.pallas_call(
        _expert_kernel,
        grid_spec=grid_spec,
        out_shape=jax.ShapeDtypeStruct((n_rows, d), F32),
        compiler_params=_cparams("arbitrary"),
        name="moe_experts",
    )(blk_expert, n_used, xs, w_gate, w_up, w_down)


def _hier_moe(rows, z, norm_g, modtab, w_grp, b_grp, w_exp, b_exp, w_gate, w_up, w_down, *, n_rows):
    d = z.shape[1]
    n_logit = N_GROUPS + N_EXPERTS
    w_route = jnp.concatenate([w_grp, w_exp, jnp.zeros((d, 128 - n_logit), F32)], axis=1)
    logits, h2 = _norm_matmul(rows, z, norm_g, modtab, w_route, shift_row=3, scale_row=4,
                              tn=128, tm_cap=512, n_rows=n_rows, emit_h=True, name="moe_route")
    grp_logits = logits[:, :N_GROUPS] + b_grp
    exp_logits = (logits[:, N_GROUPS:n_logit] + b_exp).reshape(n_rows, N_GROUPS, EXPERTS_PER_GROUP)
    g_idx = jnp.argmax(grp_logits, axis=-1)
    p_grp = jnp.max(jax.nn.softmax(grp_logits, axis=-1), axis=-1, keepdims=True)
    in_grp = jnp.take_along_axis(exp_logits, g_idx[:, None, None], axis=1)[:, 0]
    top_val, top_idx = lax.top_k(in_grp, TOP_K)
    wts = p_grp * jax.nn.softmax(top_val, axis=-1)
    eid = (g_idx[:, None] * EXPERTS_PER_GROUP + top_idx).reshape(-1).astype(jnp.int32)
    n_assign = n_rows * TOP_K
    order = jnp.argsort(eid)
    e_sorted = eid[order]
    counts = jnp.bincount(eid, length=N_EXPERTS)
    padded = (counts + MOE_BLOCK - 1) // MOE_BLOCK * MOE_BLOCK
    pad_end = jnp.cumsum(padded)
    pad_start = pad_end - padded
    start = jnp.cumsum(counts) - counts
    dest = (pad_start[e_sorted] + (jnp.arange(n_assign) - start[e_sorted])).astype(jnp.int32)
    n_blocks = (n_assign + N_EXPERTS * (MOE_BLOCK - 1) + MOE_BLOCK - 1) // MOE_BLOCK
    n_pad_rows = n_blocks * MOE_BLOCK
    tok_sorted = (order // TOP_K).astype(jnp.int32)
    row_tok = jnp.full((n_pad_rows,), n_rows, jnp.int32).at[dest].set(tok_sorted)
    blk_expert = jnp.minimum(jnp.searchsorted(pad_end, jnp.arange(n_blocks) * MOE_BLOCK, side='right'),
                             N_EXPERTS - 1).astype(jnp.int32)
    n_used = (pad_end[-1] // MOE_BLOCK).astype(jnp.int32).reshape(1)
    h2_pad = jnp.concatenate([h2, jnp.zeros((1, d), h2.dtype)], axis=0)
    xs = h2_pad[row_tok]
    ys = _expert_blocks(xs, blk_expert, n_used, w_gate, w_up, w_down)
    pos = jnp.zeros((n_assign,), jnp.int32).at[order].set(dest).reshape(n_rows, TOP_K)
    out = ys[pos[:, 0]] * wts[:, 0:1] + ys[pos[:, 1]] * wts[:, 1:2]
    return out


def kernel(x, c, ctx, c_ctx, mod_w, mod_b, norm1_g, norm2_g, w_in, dn_conv, dn_a_log, dn_dt_bias, dn_norm_g, dn_out, lru_conv, lru_conv_b, lru_wa, lru_ba, lru_wx, lru_bx, lru_lambda, lru_out, att_qn_g, att_kn_g, att_sink, att_out, w_o, moe_w_grp, moe_b_grp, moe_w_exp, moe_b_exp, moe_w_gate, moe_w_up, moe_w_down):
    batch, seq, d = x.shape
    n_ctx = ctx.shape[1]
    depth = mod_w.shape[0]
    rows = _Rows(batch, seq, n_ctx)
    assert d == D_MODEL and batch + 1 <= 8

    z = jnp.concatenate([x.reshape(batch * seq, d), ctx.reshape(batch * n_ctx, d)], axis=0)

    cc = jnp.concatenate([c_ctx[None], c, jnp.zeros((7 - batch, d), F32)], axis=0)
    mods = _mod_tables(cc, mod_w, mod_b).reshape(depth, 8, 6, d)
    mods = jnp.concatenate([mods, jnp.zeros((depth, 8, 2, d), F32)], axis=2)

    cos, sin = _rope_tables(seq, n_ctx)
    pc = _ProjCols
    w_proj = _relayout_w_in(w_in)

    for l in range(depth):
        last = l == depth - 1
        modtab = mods[l]
        n_out = rows.n_lat_rows if last else rows.n_rows

        p = _norm_matmul(rows, z, norm1_g[l], modtab, w_proj, layer=l, shift_row=0, scale_row=1, tn=pc.tile,
                         name="in_proj")

        qkv = _dn_prep(rows, p, dn_conv[l])
        par = jnp.zeros((2, 8, 128), F32)
        par = par.at[:, 0, :N_HEADS].set(dn_a_log[l]).at[:, 1, :N_HEADS].set(dn_dt_bias[l])
        o_f = _dn_scan(rows, qkv, p, par, small_col=pc.scalars, reverse=False)
        o_b = _dn_scan(rows, qkv, p, par, small_col=pc.scalars, reverse=True)

        lru_args = (lru_conv[l], lru_conv_b[l], lru_wa[l], lru_wx[l], lru_ba[l], lru_bx[l], lru_lambda[l])
        hs_f = _lru_scan(rows, p, *lru_args, x_col=pc.lx, reverse=False)
        hs_b = _lru_scan(rows, p, *lru_args, x_col=pc.lx, reverse=True)

        qn, kn = _qk_prep(rows, p, cos, sin, att_qn_g[l], att_kn_g[l], pc.aq, pc.ak)
        sink_b = jnp.broadcast_to(att_sink[l][:, None], (N_HEADS, 128))
        y_c = _attention(rows, qn, kn, p, pc.av, sink_b, with_ctx_queries=not last)

        merged = _merge(rows, o_f, o_b, p, hs_f, hs_b, p, y_c, dn_out[l], lru_out[l], att_out[l],
                        dn_norm_g[l], z_col=pc.z, lg_col=pc.lg, gate_col=pc.gates, n_rows=n_out)
        z = _mm_residual(rows, merged, w_o[l], z, modtab, gate_row=2, n_rows=n_out)

        f = _hier_moe(rows, z, norm2_g[l], modtab, moe_w_grp[l], moe_b_grp[l], moe_w_exp[l], moe_b_exp[l],
                      moe_w_gate[l], moe_w_up[l], moe_w_down[l], n_rows=n_out)
        gate2 = jnp.concatenate([jnp.repeat(modtab[1:1 + batch, 5], seq, axis=0),
                                 jnp.broadcast_to(modtab[0, 5], (batch * n_ctx, d))], axis=0)[:n_out]
        z = z + gate2 * f

    return z[:rows.n_lat_rows].reshape(batch, seq, d)
```

```python
import functools

import jax
import jax.numpy as jnp
from jax import lax
from jax.experimental import pallas as pl
from jax.experimental.pallas import tpu as pltpu

F32 = jnp.float32
BF16 = jnp.bfloat16

EPS = 1e-6
D_MODEL = 2048
N_HEADS = 8
HEAD_DIM = 128
MIX_W = N_HEADS * HEAD_DIM
KV_HEADS = 2
KV_W = KV_HEADS * HEAD_DIM
DN_CHUNK = 64
DN_UNIT = 128
CONV_W = 4
LRU_C = 8.0
ATT_WINDOW = 128
GRID_W = 64
ROPE_THETA = 10000.0
N_GROUPS = 8
EXPERTS_PER_GROUP = 8
N_EXPERTS = N_GROUPS * EXPERTS_PER_GROUP
TOP_K = 2
D_EXPERT = 512
MOE_BLOCK = 128
ROW_BLK = 256
NEG_BIG = -1e30
VMEM_LIMIT = 56 * 1024 * 1024


def _cparams(*sem):
    return pltpu.CompilerParams(dimension_semantics=sem, vmem_limit_bytes=VMEM_LIMIT)


def _bdot(a, b):
    return jnp.dot(a.astype(BF16), b.astype(BF16), preferred_element_type=F32)


def _bdot_nt(a, b):
    return lax.dot_general(a.astype(BF16), b.astype(BF16), (((1,), (1,)), ((), ())),
                           preferred_element_type=F32)


def _dot01(m01, x):
    m = m01.astype(BF16)
    x0 = x.astype(BF16)
    r1 = x - x0.astype(F32)
    x1 = r1.astype(BF16)
    x2 = (r1 - x1.astype(F32)).astype(BF16)
    dot = lambda t: jnp.dot(m, t, preferred_element_type=F32)
    return dot(x0) + dot(x1) + dot(x2)


def _silu(x):
    return x * jax.nn.sigmoid(x)


def _softplus(x):
    return jnp.maximum(x, 0.0) + jnp.log1p(jnp.exp(-jnp.abs(x)))


class _Rows:
    def __init__(self, batch, seq, n_ctx):
        assert seq % ROW_BLK == 0 and n_ctx % ROW_BLK == 0
        self.batch, self.seq, self.n_ctx = batch, seq, n_ctx
        self.nlat = seq // ROW_BLK
        self.nctx = n_ctx // ROW_BLK
        self.n_lat_rows = batch * seq
        self.n_rows = batch * (seq + n_ctx)
        self.steps = self.nlat + self.nctx

    def seq_block(self, b, t, reverse):
        if reverse:
            jc, jl = self.nctx - 1 - t, self.nlat - 1 - (t - self.nctx)
        else:
            jc, jl = t, t - self.nctx
        return jnp.where(t < self.nctx, self.batch * self.nlat + b * self.nctx + jc, b * self.nlat + jl)

    def mod_index(self, tile, tm):
        r0 = tile * tm
        return jnp.where(r0 < self.n_lat_rows, 1 + r0 // self.seq, 0)

    def tile_rows(self, cap):
        tm = cap
        while self.seq % tm or (self.batch * self.n_ctx) % tm:
            tm //= 2
        return tm


def _modtab_kernel(c_ref, w_ref, b_ref, o_ref):
    o_ref[0] = _bdot(_silu(c_ref[...]), w_ref[0]) + b_ref[0]


def _mod_tables(cc, mod_w, mod_b):
    depth, d, n = mod_w.shape
    tn = 1024
    return pl.pallas_call(
        _modtab_kernel,
        grid=(depth, n // tn),
        in_specs=[pl.BlockSpec((8, d), lambda l, j: (0, 0)),
                  pl.BlockSpec((1, d, tn), lambda l, j: (l, 0, j)),
                  pl.BlockSpec((1, 1, tn), lambda l, j: (l, 0, j))],
        out_specs=pl.BlockSpec((1, 8, tn), lambda l, j: (l, 0, j)),
        out_shape=jax.ShapeDtypeStruct((depth, 8, n), F32),
        compiler_params=_cparams("parallel", "parallel"),
        name="mod_tables",
    )(cc, mod_w, mod_b.reshape(depth, 1, n))


class _ProjCols:
    tile = 512
    small = 4 * N_HEADS
    qkvz = 4 * MIX_W
    rest = 3 * MIX_W + 2 * KV_W + 3 * D_MODEL
    z, lx, lg, aq = 3 * MIX_W, 4 * MIX_W, 5 * MIX_W, 6 * MIX_W
    ak = aq + MIX_W
    av = ak + KV_W
    gates = av + KV_W
    scalars = qkvz + rest
    total = scalars + 4 * 128


def _relayout_kernel(a_ref, b_ref, s_ref, o_ref, *, n_aligned, n_tiles, shift):
    j = pl.program_id(1)

    @pl.when(j < n_aligned)
    def _():
        o_ref[0] = a_ref[0].astype(BF16)

    @pl.when((j >= n_aligned) & (j < n_tiles - 1))
    def _():
        o_ref[0] = jnp.concatenate([a_ref[0][:, shift:], b_ref[0][:, :shift]], axis=1).astype(BF16)

    @pl.when(j == n_tiles - 1)
    def _():
        o_ref[0] = s_ref[0].astype(BF16)


def _relayout_w_in(w_in):
    depth, d, n_in = w_in.shape
    pc = _ProjCols
    assert n_in == pc.qkvz + pc.small + pc.rest and pc.qkvz % pc.tile == 0 and pc.rest % pc.tile == 0
    n_aligned = pc.qkvz // pc.tile
    n_tiles = pc.total // pc.tile
    per = pc.tile // 128
    w_small = w_in[:, :, pc.qkvz:pc.qkvz + pc.small].reshape(depth, d, 4, N_HEADS)
    w_small = jnp.pad(w_small, ((0, 0), (0, 0), (0, 0), (0, 128 - N_HEADS))).reshape(depth, d, 4 * 128)
    return pl.pallas_call(
        functools.partial(_relayout_kernel, n_aligned=n_aligned, n_tiles=n_tiles, shift=pc.small),
        grid=(depth, n_tiles),
        in_specs=[pl.BlockSpec((1, d, pc.tile), lambda l, j: (l, 0, jnp.minimum(j, n_tiles - 2))),
                  pl.BlockSpec((1, d, 128), lambda l, j: (l, 0, jnp.minimum(j + 1, n_tiles - 1) * per)),
                  pl.BlockSpec((1, d, pc.tile), lambda l, j: (l, 0, 0))],
        out_specs=pl.BlockSpec((1, d, pc.tile), lambda l, j: (l, 0, j)),
        out_shape=jax.ShapeDtypeStruct((depth, d, pc.total), BF16),
        compiler_params=_cparams("parallel", "parallel"),
        name="relayout_w_in",
    )(w_in, w_in, w_small)


def _norm_mm_kernel(z_ref, g_ref, mod_ref, w_ref, o_ref, *rest, shift_row, scale_row, emit_h):
    h_scr = rest[-1]
    tm = z_ref.shape[0]
    sub = min(tm, 128)

    @pl.when(pl.program_id(1) == 0)
    def _():
        g = g_ref[...]
        m = mod_ref[0]
        scale1 = 1.0 + m[scale_row:scale_row + 1]
        shift = m[shift_row:shift_row + 1]

        def body(i, carry):
            r0 = pl.multiple_of(i * sub, sub)
            z = z_ref[pl.ds(r0, sub), :]
            y = z * lax.rsqrt(jnp.mean(z * z, axis=-1, keepdims=True) + EPS) * g
            h_scr[pl.ds(r0, sub), :] = (y * scale1 + shift).astype(BF16)
            return carry

        lax.fori_loop(0, tm // sub, body, 0)

    h = h_scr[...]
    o_ref[...] = jnp.dot(h, w_ref[...].astype(BF16), preferred_element_type=F32)
    if emit_h:
        @pl.when(pl.program_id(1) == 0)
        def _():
            rest[0][...] = h


def _norm_matmul(rows, z, g, modtab, w, *, shift_row, scale_row, tn, layer=None, tm_cap=1024,
                 n_rows=None, emit_h=False, name="norm_matmul"):
    n_rows = rows.n_rows if n_rows is None else n_rows
    d = z.shape[1]
    n_cols = w.shape[-1]
    tm = rows.tile_rows(tm_cap)
    assert n_rows % tm == 0 and n_cols % tn == 0
    if layer is None:
        w_spec = pl.BlockSpec((d, tn), lambda i, j: (0, j))
    else:
        w_spec = pl.BlockSpec((None, d, tn), lambda i, j: (layer, 0, j))
    out_shape = [jax.ShapeDtypeStruct((n_rows, n_cols), F32)]
    out_specs = [pl.BlockSpec((tm, tn), lambda i, j: (i, j))]
    if emit_h:
        out_shape.append(jax.ShapeDtypeStruct((n_rows, d), BF16))
        out_specs.append(pl.BlockSpec((tm, d), lambda i, j: (i, 0)))
    res = pl.pallas_call(
        functools.partial(_norm_mm_kernel, shift_row=shift_row, scale_row=scale_row, emit_h=emit_h),
        grid=(n_rows // tm, n_cols // tn),
        in_specs=[pl.BlockSpec((tm, d), lambda i, j: (i, 0)),
                  pl.BlockSpec((1, d), lambda i, j: (0, 0)),
                  pl.BlockSpec((1, 8, d), lambda i, j: (rows.mod_index(i, tm), 0, 0)),
                  w_spec],
        out_specs=out_specs,
        out_shape=out_shape,
        scratch_shapes=[pltpu.VMEM((tm, d), BF16)],
        compiler_params=_cparams("parallel", "arbitrary"),
        name=name,
    )(z, g.reshape(1, d), modtab, w)
    return res if emit_h else res[0]


def _halo_flags(rows, blk):
    is_lat = blk < rows.batch * rows.nlat
    j = jnp.where(is_lat, blk % rows.nlat, (blk - rows.batch * rows.nlat) % rows.nctx)
    nseg = jnp.where(is_lat, rows.nlat, rows.nctx)
    return (j != 0).astype(F32), (j != nseg - 1).astype(F32)


def _conv4(prev8, cur, next8, w, use_prev, use_next):
    n = cur.shape[0]
    x = jnp.concatenate([prev8 * use_prev, cur, next8 * use_next], axis=0)
    tot = n + 16
    y = w[1:2] * cur
    y = y + w[0:1] * pltpu.roll(x, 1, 0)[8:8 + n]
    y = y + w[2:3] * pltpu.roll(x, tot - 1, 0)[8:8 + n]
    y = y + w[3:4] * pltpu.roll(x, tot - 2, 0)[8:8 + n]
    return y


def _halo_specs(rows, width, col_blk, blk_of):
    per = ROW_BLK // 8
    last8 = rows.n_rows // 8 - 1
    return [
        pl.BlockSpec((8, width), lambda *g: (jnp.maximum(blk_of(*g) * per - 1, 0), col_blk(*g))),
        pl.BlockSpec((ROW_BLK, width), lambda *g: (blk_of(*g), col_blk(*g))),
        pl.BlockSpec((8, width), lambda *g: (jnp.minimum((blk_of(*g) + 1) * per, last8), col_blk(*g))),
    ]


def _dn_prep_kernel(prev_ref, cur_ref, next_ref, w_ref, o_ref, *, rows):
    blk = pl.program_id(0)
    kind = pl.program_id(1)
    use_prev, use_next = _halo_flags(rows, blk)
    y = _silu(_conv4(prev_ref[...], cur_ref[...], next_ref[...], w_ref[...], use_prev, use_next))
    q_scale = jnp.where(kind == 0, HEAD_DIM ** -0.5, 1.0)
    for h in range(N_HEADS):
        sl = slice(h * HEAD_DIM, (h + 1) * HEAD_DIM)
        yh = y[:, sl]
        inv = lax.rsqrt(jnp.sum(yh * yh, axis=-1, keepdims=True) + EPS)
        o_ref[:, sl] = yh * (jnp.where(kind == 2, 1.0, inv) * q_scale)


def _dn_prep(rows, p_a, conv_w):
    return pl.pallas_call(
        functools.partial(_dn_prep_kernel, rows=rows),
        grid=(rows.n_rows // ROW_BLK, 3),
        in_specs=_halo_specs(rows, MIX_W, lambda i, j: j, lambda i, j: i)
        + [pl.BlockSpec((CONV_W, MIX_W), lambda i, j: (0, j))],
        out_specs=pl.BlockSpec((ROW_BLK, MIX_W), lambda i, j: (i, j)),
        out_shape=jax.ShapeDtypeStruct((rows.n_rows, 3 * MIX_W), F32),
        compiler_params=_cparams("parallel", "parallel"),
        name="dn_prep",
    )(p_a, p_a, p_a, conv_w)


def _dot_b(a, b):
    return jnp.dot(a, b, preferred_element_type=F32).astype(BF16)


def _dn_scan_kernel(q_ref, k_ref, v_ref, beta_ref, alpha_ref, par_ref, o_ref,
                    s_scr, u_scr, wq_scr, kdt_scr, aqk_scr, *, reverse):
    n = ROW_BLK
    nchunk = n // DN_CHUNK
    per_unit = DN_UNIT // DN_CHUNK

    @pl.when(pl.program_id(1) == 0)
    def _():
        s_scr[...] = jnp.zeros_like(s_scr)

    def tri(m):
        row = lax.broadcasted_iota(jnp.int32, (m, m), 0)
        col = lax.broadcasted_iota(jnp.int32, (m, m), 1)
        same = lambda s: (row >> (s.bit_length() - 1)) == (col >> (s.bit_length() - 1))
        ahead = (col >= row) if reverse else (col <= row)
        return row, col, same, same(DN_CHUNK) & ahead, same(DN_CHUNK) & ahead & (row != col)

    _, _, same_n, incl_n, _ = tri(n)
    par = par_ref[0]
    beta_all = jax.nn.sigmoid(beta_ref[...])
    g_all = -jnp.exp(par[0:1]) * _softplus(alpha_ref[...] + par[1:2])
    gc_all = _dot01(incl_n.astype(F32), g_all)
    gt_all = _dot01(same_n(DN_CHUNK).astype(F32), g_all)
    gc_t = gc_all.T
    egc_all = jnp.exp(gc_all)
    ekd_all = jnp.exp(gt_all - gc_all)
    egt_all = jnp.exp(gt_all)

    row_u, col_u, same_u, incl_u, strict_u = tri(DN_UNIT)
    one_b = lambda m: jnp.where(m, 1.0, 0.0).astype(BF16)
    eye_b = one_b(row_u == col_u)
    diag8_b = one_b(same_u(8))
    off_b = {s: one_b(same_u(2 * s) & jnp.logical_not(same_u(s))) for s in (8, 16, 32)}

    units = [(h, j) for h in range(N_HEADS) for j in range(n // DN_UNIT)]
    a_b, rhs_b = [], []
    for h, j in units:
        rs = slice(j * DN_UNIT, (j + 1) * DN_UNIT)
        sl = slice(h * HEAD_DIM, (h + 1) * HEAD_DIM)
        q, k, v = q_ref[rs, sl], k_ref[rs, sl], v_ref[rs, sl]
        bcol = beta_all[rs, h:h + 1]
        egc = egc_all[rs, h:h + 1]
        dec = jnp.exp(jnp.where(incl_u, gc_all[rs, h:h + 1] - gc_t[h:h + 1, rs], NEG_BIG))
        kb = k * bcol
        k_b = k.astype(BF16)
        a_b.append(jnp.where(strict_u, _bdot_nt(kb, k_b) * dec, 0.0).astype(BF16))
        aqk_scr[h, rs, :] = (_bdot_nt(q, k_b) * dec).astype(BF16)
        rhs_b.append(jnp.concatenate([v * bcol, kb * egc], axis=1).astype(BF16))
        qd = (q * egc).astype(BF16)
        kd = k * ekd_all[rs, h:h + 1]
        for c in range(per_unit):
            cs = slice(c * DN_CHUNK, (c + 1) * DN_CHUNK)
            wq_scr[h, j * per_unit + c, DN_CHUNK:, :] = qd[cs]
            kdt_scr[h, j * per_unit + c] = kd[cs].T.astype(BF16)

    d1 = [a * diag8_b for a in a_b]
    d2 = [_dot_b(d, d) for d in d1]
    d4 = [_dot_b(d, d) for d in d2]
    x = [_dot_b(eye_b - d, eye_b + e) for d, e in zip(d1, d2)]
    x = [_dot_b(xx, eye_b + e) for xx, e in zip(x, d4)]
    for s in (8, 16, 32):
        t = [_dot_b(a * off_b[s], xx) for a, xx in zip(a_b, x)]
        x = [xx - _dot_b(xx, tt) for xx, tt in zip(x, t)]
    for (h, j), xx, rhs in zip(units, x, rhs_b):
        uw = jnp.dot(xx, rhs, preferred_element_type=F32)
        u_scr[h, j * DN_UNIT:(j + 1) * DN_UNIT, :] = uw[:, :HEAD_DIM]
        for c in range(per_unit):
            wq_scr[h, j * per_unit + c, :DN_CHUNK, :] = uw[c * DN_CHUNK:(c + 1) * DN_CHUNK, HEAD_DIM:].astype(BF16)

    zeros_b = jnp.zeros((DN_CHUNK, HEAD_DIM), BF16)
    for ci in range(nchunk):
        c = nchunk - 1 - ci if reverse else ci
        rs = slice(c * DN_CHUNK, (c + 1) * DN_CHUNK)
        for h in range(N_HEADS):
            s_h = s_scr[h]
            r = jnp.dot(wq_scr[h, c], s_h.astype(BF16), preferred_element_type=F32)
            v_new = (u_scr[h, rs, :] - r[:DN_CHUNK]).astype(BF16)
            v_unit = jnp.concatenate([v_new, zeros_b] if c % per_unit == 0 else [zeros_b, v_new], axis=0)
            o = r[DN_CHUNK:] + jnp.dot(aqk_scr[h, rs, :], v_unit, preferred_element_type=F32)
            o_ref[rs, h * HEAD_DIM:(h + 1) * HEAD_DIM] = o
            s_scr[h] = s_h * egt_all[c * DN_CHUNK:c * DN_CHUNK + 1, h:h + 1] + jnp.dot(
                kdt_scr[h, c], v_new, preferred_element_type=F32)


def _dn_scan(rows, qkv, p, par, *, small_col, reverse):
    d = 1 if reverse else 0
    blk = lambda b, t: rows.seq_block(b, t, reverse)
    nchunk = ROW_BLK // DN_CHUNK
    sc = small_col // 128
    return pl.pallas_call(
        functools.partial(_dn_scan_kernel, reverse=reverse),
        grid=(rows.batch, rows.steps),
        in_specs=[pl.BlockSpec((ROW_BLK, MIX_W), lambda b, t: (blk(b, t), 0)),
                  pl.BlockSpec((ROW_BLK, MIX_W), lambda b, t: (blk(b, t), 1)),
                  pl.BlockSpec((ROW_BLK, MIX_W), lambda b, t: (blk(b, t), 2)),
                  pl.BlockSpec((ROW_BLK, 128), lambda b, t: (blk(b, t), sc + d)),
                  pl.BlockSpec((ROW_BLK, 128), lambda b, t: (blk(b, t), sc + 2 + d)),
                  pl.BlockSpec((1, 8, 128), lambda b, t: (d, 0, 0))],
        out_specs=pl.BlockSpec((ROW_BLK, MIX_W), lambda b, t: (blk(b, t), 0)),
        out_shape=jax.ShapeDtypeStruct((rows.n_rows, MIX_W), F32),
        scratch_shapes=[pltpu.VMEM((N_HEADS, HEAD_DIM, HEAD_DIM), F32),
                        pltpu.VMEM((N_HEADS, ROW_BLK, HEAD_DIM), F32),
                        pltpu.VMEM((N_HEADS, nchunk, 2 * DN_CHUNK, HEAD_DIM), BF16),
                        pltpu.VMEM((N_HEADS, nchunk, HEAD_DIM, DN_CHUNK), BF16),
                        pltpu.VMEM((N_HEADS, ROW_BLK, DN_UNIT), BF16)],
        compiler_params=_cparams("parallel", "arbitrary"),
        name="dn_scan_bwd" if reverse else "dn_scan_fwd",
    )(qkv, qkv, qkv, p, p, par)


def _lru_kernel(prev_ref, cur_ref, next_ref, cw_ref, cb_ref, wa_ref, wx_ref, ba_ref, bx_ref, lam_ref,
                o_ref, h_scr, a_scr, u_scr, *, rows, reverse):
    b, t = pl.program_id(0), pl.program_id(1)

    @pl.when(t == 0)
    def _():
        h_scr[...] = jnp.zeros_like(h_scr)

    use_prev, use_next = _halo_flags(rows, rows.seq_block(b, t, reverse))
    xc = _conv4(prev_ref[...], cur_ref[...], next_ref[...], cw_ref[...], use_prev, use_next) + cb_ref[...]
    for kb in range(N_HEADS):
        sl = slice(kb * HEAD_DIM, (kb + 1) * HEAD_DIM)
        xb = xc[:, sl]
        r = jax.nn.sigmoid(_bdot(xb, wa_ref[0, kb]) + ba_ref[0, :, sl])
        i = jax.nn.sigmoid(_bdot(xb, wx_ref[0, kb]) + bx_ref[0, :, sl])
        log_a = -LRU_C * r * _softplus(-lam_ref[0, :, sl])
        a_scr[:, sl] = jnp.exp(log_a)
        u_scr[:, sl] = jnp.sqrt(1.0 - jnp.exp(2.0 * log_a)) * (i * xb)

    ngrp = ROW_BLK // 8
    sub = lax.broadcasted_iota(jnp.int32, (8, MIX_W), 0)

    def group_step(gi, h_prev):
        g = ngrp - 1 - gi if reverse else gi
        r0 = pl.multiple_of(g * 8, 8)
        a = a_scr[pl.ds(r0, 8), :]
        u = u_scr[pl.ds(r0, 8), :]
        for dist in (1, 2, 4):
            if reverse:
                a_sh, u_sh, m = pltpu.roll(a, 8 - dist, 0), pltpu.roll(u, 8 - dist, 0), sub < 8 - dist
            else:
                a_sh, u_sh, m = pltpu.roll(a, dist, 0), pltpu.roll(u, dist, 0), sub >= dist
            u = jnp.where(m, a * u_sh + u, u)
            a = jnp.where(m, a * a_sh, a)
        h = u + a * h_prev
        o_ref[pl.ds(r0, 8), :] = h
        return h[0:1] if reverse else h[7:8]

    h_scr[...] = lax.fori_loop(0, ngrp, group_step, h_scr[...])


def _lru_scan(rows, p_b, cw, cb, wa, wx, ba, bx, lam, *, x_col, reverse):
    d = 1 if reverse else 0
    blk = lambda b, t: rows.seq_block(b, t, reverse)
    vec = lambda: pl.BlockSpec((1, 1, MIX_W), lambda b, t: (d, 0, 0))
    mat = lambda: pl.BlockSpec((1, N_HEADS, HEAD_DIM, HEAD_DIM), lambda b, t: (d, 0, 0, 0))
    return pl.pallas_call(
        functools.partial(_lru_kernel, rows=rows, reverse=reverse),
        grid=(rows.batch, rows.steps),
        in_specs=_halo_specs(rows, MIX_W, lambda b, t: x_col // MIX_W, blk)
        + [pl.BlockSpec((CONV_W, MIX_W), lambda b, t: (0, 0)),
           pl.BlockSpec((1, MIX_W), lambda b, t: (0, 0)),
           mat(), mat(), vec(), vec(), vec()],
        out_specs=pl.BlockSpec((ROW_BLK, MIX_W), lambda b, t: (blk(b, t), 0)),
        out_shape=jax.ShapeDtypeStruct((rows.n_rows, MIX_W), F32),
        scratch_shapes=[pltpu.VMEM((1, MIX_W), F32),
                        pltpu.VMEM((ROW_BLK, MIX_W), F32),
                        pltpu.VMEM((ROW_BLK, MIX_W), F32)],
        compiler_params=_cparams("parallel", "arbitrary"),
        name="lru_bwd" if reverse else "lru_fwd",
    )(p_b, p_b, p_b, cw, cb.reshape(1, MIX_W), wa, wx,
      ba.reshape(2, 1, MIX_W), bx.reshape(2, 1, MIX_W), lam.reshape(2, 1, MIX_W))


def _rope_tables(seq, n_ctx):
    half = HEAD_DIM // 2
    pos = jnp.arange(seq)
    inv = ROPE_THETA ** (-jnp.arange(0, half, 2, dtype=F32) / half)
    ang_r = (pos // GRID_W).astype(F32)[:, None] * inv
    ang_c = (pos % GRID_W).astype(F32)[:, None] * inv
    cos = jnp.concatenate([jnp.cos(ang_r)] * 2 + [jnp.cos(ang_c)] * 2, axis=-1)
    sin = jnp.concatenate([-jnp.sin(ang_r), jnp.sin(ang_r), -jnp.sin(ang_c), jnp.sin(ang_c)], axis=-1)
    cos = jnp.concatenate([cos, jnp.ones((n_ctx, HEAD_DIM), F32)], axis=0)
    sin = jnp.concatenate([sin, jnp.zeros((n_ctx, HEAD_DIM), F32)], axis=0)
    return cos, sin


def _qk_prep_kernel(q_ref, k_ref, cos_ref, sin_ref, qg_ref, kg_ref, qo_ref, ko_ref):
    cos, sin = cos_ref[...], sin_ref[...]
    lane = lax.broadcasted_iota(jnp.int32, cos.shape, 1)
    first = (lane & (HEAD_DIM // 2 - 1)) < (HEAD_DIM // 4)

    def norm_rope(x, g):
        y = x * lax.rsqrt(jnp.mean(x * x, axis=-1, keepdims=True) + EPS) * g
        partner = jnp.where(first, pltpu.roll(y, HEAD_DIM - HEAD_DIM // 4, 1), pltpu.roll(y, HEAD_DIM // 4, 1))
        return y * cos + partner * sin

    for h in range(N_HEADS):
        sl = slice(h * HEAD_DIM, (h + 1) * HEAD_DIM)
        qo_ref[:, sl] = (norm_rope(q_ref[:, sl], qg_ref[...]) * (HEAD_DIM ** -0.5)).astype(BF16)
    for h in range(KV_HEADS):
        sl = slice(h * HEAD_DIM, (h + 1) * HEAD_DIM)
        ko_ref[:, sl] = norm_rope(k_ref[:, sl], kg_ref[...]).astype(BF16)


def _qk_prep(rows, p_b, cos, sin, qn_g, kn_g, q_col, k_col):
    def tab(i):
        return jnp.where(i < rows.batch * rows.nlat, i % rows.nlat,
                         rows.nlat + (i - rows.batch * rows.nlat) % rows.nctx)

    return pl.pallas_call(
        _qk_prep_kernel,
        grid=(rows.n_rows // ROW_BLK,),
        in_specs=[pl.BlockSpec((ROW_BLK, MIX_W), lambda i: (i, q_col // MIX_W)),
                  pl.BlockSpec((ROW_BLK, KV_W), lambda i: (i, k_col // KV_W)),
                  pl.BlockSpec((ROW_BLK, HEAD_DIM), lambda i: (tab(i), 0)),
                  pl.BlockSpec((ROW_BLK, HEAD_DIM), lambda i: (tab(i), 0)),
                  pl.BlockSpec((1, HEAD_DIM), lambda i: (0, 0)),
                  pl.BlockSpec((1, HEAD_DIM), lambda i: (0, 0))],
        out_specs=[pl.BlockSpec((ROW_BLK, MIX_W), lambda i: (i, 0)),
                   pl.BlockSpec((ROW_BLK, KV_W), lambda i: (i, 0))],
        out_shape=[jax.ShapeDtypeStruct((rows.n_rows, MIX_W), BF16),
                   jax.ShapeDtypeStruct((rows.n_rows, KV_W), BF16)],
        compiler_params=_cparams("parallel"),
        name="qk_prep",
    )(p_b, p_b, cos, sin, qn_g.reshape(1, HEAD_DIM), kn_g.reshape(1, HEAD_DIM))


def _attn_core(q, keys, vals, sink_ref, valid, o_ref):
    grp = N_HEADS // KV_HEADS
    nq = q.shape[0]
    for kvh in range(KV_HEADS):
        sl = slice(kvh * HEAD_DIM, (kvh + 1) * HEAD_DIM)
        kk = jnp.concatenate([t[:, sl] for t in keys], axis=0)
        vv = jnp.concatenate([t[:, sl].astype(BF16) for t in vals], axis=0)
        q4 = jnp.concatenate([q[:, (kvh * grp + g) * HEAD_DIM:(kvh * grp + g + 1) * HEAD_DIM]
                              for g in range(grp)], axis=0)
        s = _bdot_nt(q4, kk)
        if valid is not None:
            s = jnp.where(valid, s, NEG_BIG)
        sink = jnp.concatenate([jnp.broadcast_to(sink_ref[kvh * grp + g:kvh * grp + g + 1, 0:1], (nq, 1))
                                for g in range(grp)], axis=0)
        m = jnp.maximum(jnp.max(s, axis=-1, keepdims=True), sink)
        e = jnp.exp(s - m)
        p = e / (jnp.sum(e, axis=-1, keepdims=True) + jnp.exp(sink - m))
        o = _bdot(p, vv)
        for g in range(grp):
            hq = kvh * grp + g
            o_ref[:, hq * HEAD_DIM:(hq + 1) * HEAD_DIM] = o[g * nq:(g + 1) * nq].astype(BF16)


def _attn_latent_kernel(q_ref, k0_ref, k1_ref, k2_ref, kc_ref, v0_ref, v1_ref, v2_ref, vc_ref, sink_ref, o_ref,
                        *, seq):
    n = pl.program_id(1)
    blk = q_ref.shape[0]
    n_loc = 3 * blk
    n_ctx = kc_ref.shape[0]
    grp = N_HEADS // KV_HEADS
    qi = lax.broadcasted_iota(jnp.int32, (grp * blk, n_loc + n_ctx), 0) & (blk - 1)
    kj = lax.broadcasted_iota(jnp.int32, (grp * blk, n_loc + n_ctx), 1)
    rel = kj - blk
    kpos = n * blk + rel
    local_ok = (jnp.abs(qi - rel) <= ATT_WINDOW) & (kpos >= 0) & (kpos < seq)
    valid = local_ok | (kj >= n_loc)
    _attn_core(q_ref[...], [k0_ref[...], k1_ref[...], k2_ref[...], kc_ref[...]],
               [v0_ref[...], v1_ref[...], v2_ref[...], vc_ref[...]], sink_ref, valid, o_ref)


def _attn_ctx_kernel(q_ref, kc_ref, vc_ref, sink_ref, o_ref):
    _attn_core(q_ref[...], [kc_ref[...]], [vc_ref[...]], sink_ref, None, o_ref)


def _attention(rows, qn, kn, p_b, v_col, sink_b, *, with_ctx_queries):
    blk = 128
    nb = rows.seq // blk
    vcb = v_col // KV_W
    ctx_blk0 = rows.n_lat_rows // rows.n_ctx
    assert rows.n_lat_rows % rows.n_ctx == 0
    kspec = lambda off: pl.BlockSpec((blk, KV_W), lambda b, n: (b * nb + jnp.clip(n + off, 0, nb - 1), 0))
    vspec = lambda off: pl.BlockSpec((blk, KV_W), lambda b, n: (b * nb + jnp.clip(n + off, 0, nb - 1), vcb))
    n_out = rows.n_rows if with_ctx_queries else rows.n_lat_rows
    y_lat = pl.pallas_call(
        functools.partial(_attn_latent_kernel, seq=rows.seq),
        grid=(rows.batch, nb),
        in_specs=[pl.BlockSpec((blk, MIX_W), lambda b, n: (b * nb + n, 0)),
                  kspec(-1), kspec(0), kspec(1),
                  pl.BlockSpec((rows.n_ctx, KV_W), lambda b, n: (ctx_blk0 + b, 0)),
                  vspec(-1), vspec(0), vspec(1),
                  pl.BlockSpec((rows.n_ctx, KV_W), lambda b, n: (ctx_blk0 + b, vcb)),
                  pl.BlockSpec((8, 128), lambda b, n: (0, 0))],
        out_specs=pl.BlockSpec((blk, MIX_W), lambda b, n: (b * nb + n, 0)),
        out_shape=jax.ShapeDtypeStruct((n_out, MIX_W), BF16),
        compiler_params=_cparams("parallel", "parallel"),
        name="attn_latent",
    )(qn, kn, kn, kn, kn, p_b, p_b, p_b, p_b, sink_b)
    if not with_ctx_queries:
        return y_lat
    ncb = rows.n_ctx // blk
    q0 = rows.n_lat_rows // blk
    return pl.pallas_call(
        lambda q_ref, kc_ref, vc_ref, sink_ref, y_in_ref, o_ref: _attn_ctx_kernel(q_ref, kc_ref, vc_ref, sink_ref, o_ref),
        grid=(rows.batch, ncb),
        in_specs=[pl.BlockSpec((blk, MIX_W), lambda b, i: (q0 + b * ncb + i, 0)),
                  pl.BlockSpec((rows.n_ctx, KV_W), lambda b, i: (ctx_blk0 + b, 0)),
                  pl.BlockSpec((rows.n_ctx, KV_W), lambda b, i: (ctx_blk0 + b, vcb)),
                  pl.BlockSpec((8, 128), lambda b, i: (0, 0)),
                  pl.BlockSpec(memory_space=pl.ANY)],
        out_specs=pl.BlockSpec((blk, MIX_W), lambda b, i: (q0 + b * ncb + i, 0)),
        out_shape=jax.ShapeDtypeStruct((n_out, MIX_W), BF16),
        input_output_aliases={4: 0},
        compiler_params=_cparams("parallel", "parallel"),
        name="attn_ctx",
    )(qn, kn, p_b, sink_b, y_lat)


def _merge_kernel(of_ref, ob_ref, z_ref, hf_ref, hb_ref, lg_ref, yc_ref, ga_ref, gb_ref, gc_ref,
                  wa_ref, wb_ref, wc_ref, ng_ref, o_ref, ya_scr, yb_scr):
    tm = of_ref.shape[0]
    sub = min(tm, 128)

    @pl.when(pl.program_id(1) == 0)
    def _():
        ng = ng_ref[...]

        def body(i, carry):
            r0 = pl.multiple_of(i * sub, sub)
            rs = pl.ds(r0, sub)
            for h in range(N_HEADS):
                sl = slice(h * HEAD_DIM, (h + 1) * HEAD_DIM)
                o = of_ref[rs, sl] + ob_ref[rs, sl]
                y = o * lax.rsqrt(jnp.mean(o * o, axis=-1, keepdims=True) + EPS) * ng
                ya_scr[rs, sl] = (y * _silu(z_ref[rs, sl])).astype(BF16)
            yb_scr[rs, :] = (jax.nn.gelu(lg_ref[rs, :]) * (hf_ref[rs, :] + hb_ref[rs, :])).astype(BF16)
            return carry

        lax.fori_loop(0, tm // sub, body, 0)

    acc = jax.nn.sigmoid(ga_ref[...]) * _bdot(ya_scr[...], wa_ref[...])
    acc = acc + jax.nn.sigmoid(gb_ref[...]) * _bdot(yb_scr[...], wb_ref[...])
    acc = acc + jax.nn.sigmoid(gc_ref[...]) * _bdot(yc_ref[...], wc_ref[...])
    o_ref[...] = acc.astype(BF16)


def _merge(rows, o_f, o_b, p_a, hs_f, hs_b, p_b, y_c, dn_out, lru_out, att_out, dn_norm_g, *,
           z_col, lg_col, gate_col, n_rows):
    tm = rows.tile_rows(512)
    tn = 512
    assert n_rows % tm == 0 and gate_col % tn == 0
    row = lambda cb: pl.BlockSpec((tm, MIX_W), lambda i, j: (i, cb))
    gate = lambda br: pl.BlockSpec((tm, tn), lambda i, j: (i, (gate_col + br * D_MODEL) // tn + j))
    wgt = lambda: pl.BlockSpec((MIX_W, tn), lambda i, j: (0, j))
    return pl.pallas_call(
        _merge_kernel,
        grid=(n_rows // tm, D_MODEL // tn),
        in_specs=[row(0), row(0), row(z_col // MIX_W), row(0), row(0), row(lg_col // MIX_W), row(0),
                  gate(0), gate(1), gate(2), wgt(), wgt(), wgt(),
                  pl.BlockSpec((1, HEAD_DIM), lambda i, j: (0, 0))],
        out_specs=pl.BlockSpec((tm, tn), lambda i, j: (i, j)),
        out_shape=jax.ShapeDtypeStruct((n_rows, D_MODEL), BF16),
        scratch_shapes=[pltpu.VMEM((tm, MIX_W), BF16), pltpu.VMEM((tm, MIX_W), BF16)],
        compiler_params=_cparams("parallel", "arbitrary"),
        name="merge",
    )(o_f, o_b, p_a, hs_f, hs_b, p_b, y_c, p_b, p_b, p_b, dn_out, lru_out, att_out,
      dn_norm_g.reshape(1, HEAD_DIM))


def _mm_residual_kernel(a_ref, w_ref, z_ref, mod_ref, o_ref, *, gate_row):
    acc = _bdot(a_ref[...], w_ref[...])
    o_ref[...] = z_ref[...] + mod_ref[0][gate_row:gate_row + 1] * acc


def _mm_residual(rows, a, w, z, modtab, *, gate_row, n_rows):
    tm = rows.tile_rows(1024)
    tn = 512
    k, n = w.shape
    assert n_rows % tm == 0
    return pl.pallas_call(
        functools.partial(_mm_residual_kernel, gate_row=gate_row),
        grid=(n_rows // tm, n // tn),
        in_specs=[pl.BlockSpec((tm, k), lambda i, j: (i, 0)),
                  pl.BlockSpec((k, tn), lambda i, j: (0, j)),
                  pl.BlockSpec((tm, tn), lambda i, j: (i, j)),
                  pl.BlockSpec((1, 8, tn), lambda i, j: (rows.mod_index(i, tm), 0, j))],
        out_specs=pl.BlockSpec((tm, tn), lambda i, j: (i, j)),
        out_shape=jax.ShapeDtypeStruct((n_rows, n), F32),
        compiler_params=_cparams("parallel", "parallel"),
        name="out_proj",
    )(a, w, z, modtab)


def _route_kernel(z_ref, g_ref, mod_ref, w_ref, b_ref, h_ref, r_ref):
    m = mod_ref[0]
    z = z_ref[...]
    y = z * lax.rsqrt(jnp.mean(z * z, axis=-1, keepdims=True) + EPS) * g_ref[...]
    h = (y * (1.0 + m[4:5]) + m[3:4]).astype(BF16)
    h_ref[...] = h
    logits = jnp.dot(h, w_ref[...].astype(BF16), preferred_element_type=F32) + b_ref[...]
    lane_i = lax.broadcasted_iota(jnp.int32, logits.shape, 1)
    lane = lane_i.astype(F32)
    far = float(2 * 128)

    def top(vals):
        best = jnp.max(vals, axis=-1, keepdims=True)
        return best, jnp.min(jnp.where(vals == best, lane, far), axis=-1, keepdims=True)

    is_grp = lane_i < N_GROUPS
    g_max, g_idx = top(jnp.where(is_grp, logits, NEG_BIG))
    p_grp = 1.0 / jnp.sum(jnp.where(is_grp, jnp.exp(logits - g_max), 0.0), axis=-1, keepdims=True)
    e_lane = lane_i - N_GROUPS
    in_grp = (e_lane >= 0) & ((e_lane >> 3).astype(F32) == g_idx)
    cand = jnp.where(in_grp, logits, NEG_BIG)
    t1, i1 = top(cand)
    t2, i2 = top(jnp.where(lane == i1, NEG_BIG, cand))
    e2 = jnp.exp(t2 - t1)
    w1 = p_grp / (1.0 + e2)
    w2 = w1 * e2
    r = jnp.where(lane_i == 0, i1 - N_GROUPS, jnp.where(lane_i == 1, i2 - N_GROUPS,
                  jnp.where(lane_i == 2, w1, jnp.where(lane_i == 3, w2, 0.0))))
    r_ref[...] = r


def _route(rows, z, norm_g, modtab, w_route, b_route, *, n_rows):
    d = z.shape[1]
    tm = rows.tile_rows(256)
    return pl.pallas_call(
        _route_kernel,
        grid=(n_rows // tm,),
        in_specs=[pl.BlockSpec((tm, d), lambda i: (i, 0)),
                  pl.BlockSpec((1, d), lambda i: (0, 0)),
                  pl.BlockSpec((1, 8, d), lambda i: (rows.mod_index(i, tm), 0, 0)),
                  pl.BlockSpec((d, 128), lambda i: (0, 0)),
                  pl.BlockSpec((1, 128), lambda i: (0, 0))],
        out_specs=[pl.BlockSpec((tm, d), lambda i: (i, 0)),
                   pl.BlockSpec((tm, 128), lambda i: (i, 0))],
        out_shape=[jax.ShapeDtypeStruct((n_rows, d), BF16),
                   jax.ShapeDtypeStruct((n_rows, 128), F32)],
        compiler_params=_cparams("parallel"),
        name="moe_route",
    )(z, norm_g.reshape(1, d), modtab, w_route, b_route)


def _rank_kernel(r_ref, rank_ref, cnt_ref, carry_scr):
    @pl.when(pl.program_id(0) == 0)
    def _():
        carry_scr[...] = jnp.zeros_like(carry_scr)

    r = r_ref[...]
    tm = r.shape[0]
    lane_i = lax.broadcasted_iota(jnp.int32, r.shape, 1)
    lane = lane_i.astype(F32)
    hot1 = lane == r[:, 0:1]
    hot2 = lane == r[:, 1:2]
    cnt = jnp.where(hot1 | hot2, 1.0, 0.0)
    row = lax.broadcasted_iota(jnp.int32, (tm, tm), 0)
    col = lax.broadcasted_iota(jnp.int32, (tm, tm), 1)
    before = jnp.where(col < row, 1.0, 0.0).astype(BF16)
    carry = carry_scr[0:1, :]
    prior = jnp.dot(before, cnt.astype(BF16), preferred_element_type=F32) + carry
    rank1 = jnp.sum(jnp.where(hot1, prior, 0.0), axis=-1, keepdims=True)
    rank2 = jnp.sum(jnp.where(hot2, prior, 0.0), axis=-1, keepdims=True)
    rank_ref[...] = jnp.where(lane_i == 0, rank1, jnp.where(lane_i == 1, rank2, 0.0))
    total = carry + jnp.sum(cnt, axis=0, keepdims=True)
    carry_scr[0:1, :] = total
    cnt_ref[...] = jnp.broadcast_to(total, cnt_ref.shape)


def _rank(r, *, tm):
    n = r.shape[0]
    return pl.pallas_call(
        _rank_kernel,
        grid=(n // tm,),
        in_specs=[pl.BlockSpec((tm, 128), lambda i: (i, 0))],
        out_specs=[pl.BlockSpec((tm, 128), lambda i: (i, 0)),
                   pl.BlockSpec((8, 128), lambda i: (0, 0))],
        out_shape=[jax.ShapeDtypeStruct((n, 128), F32), jax.ShapeDtypeStruct((8, 128), F32)],
        scratch_shapes=[pltpu.VMEM((8, 128), F32)],
        compiler_params=_cparams("arbitrary"),
        name="moe_rank",
    )(r)


def _expert_kernel(be_ref, fi_ref, nu_ref, x_ref, wg_ref, wu_ref, wd_ref, o_ref, wg_b, wu_b, wd_b):
    i = pl.program_id(0)

    @pl.when((i < nu_ref[0]) & (fi_ref[i] == 1))
    def _():
        wg_b[...] = wg_ref[...].astype(BF16)
        wu_b[...] = wu_ref[...].astype(BF16)
        wd_b[...] = wd_ref[...].astype(BF16)

    @pl.when(i < nu_ref[0])
    def _():
        x = x_ref[...]
        dot = lambda a, b: jnp.dot(a, b, preferred_element_type=F32)
        hid = _silu(dot(x, wg_b[...])) * dot(x, wu_b[...])
        o_ref[...] = dot(hid.astype(BF16), wd_b[...])

    @pl.when(i >= nu_ref[0])
    def _():
        o_ref[...] = jnp.zeros_like(o_ref)


def _expert_blocks(xs, blk_expert, first, n_used, w_gate, w_up, w_down, layer):
    n_rows, d = xs.shape
    n_blocks = n_rows // MOE_BLOCK
    wspec = lambda a, b: pl.BlockSpec((None, None, a, b), lambda i, be, fi, nu: (layer, be[i], 0, 0))
    grid_spec = pltpu.PrefetchScalarGridSpec(
        num_scalar_prefetch=3,
        grid=(n_blocks,),
        in_specs=[pl.BlockSpec((MOE_BLOCK, d), lambda i, be, fi, nu: (i, 0)),
                  wspec(d, D_EXPERT), wspec(d, D_EXPERT), wspec(D_EXPERT, d)],
        out_specs=pl.BlockSpec((MOE_BLOCK, d), lambda i, be, fi, nu: (i, 0)),
        scratch_shapes=[pltpu.VMEM((d, D_EXPERT), BF16), pltpu.VMEM((d, D_EXPERT), BF16),
                        pltpu.VMEM((D_EXPERT, d), BF16)],
    )
    return pl.pallas_call(
        _expert_kernel,
        grid_spec=grid_spec,
        out_shape=jax.ShapeDtypeStruct((n_rows, d), F32),
        compiler_params=_cparams("arbitrary"),
        name="moe_experts",
    )(blk_expert, first, n_used, xs, w_gate, w_up, w_down)


def _hier_moe(rows, z, norm_g, modtab, w_grp, b_grp, w_exp, b_exp, w_gate, w_up, w_down, layer, *, n_rows):
    d = z.shape[1]
    n_logit = N_GROUPS + N_EXPERTS
    w_route = jnp.concatenate([w_grp, w_exp, jnp.zeros((d, 128 - n_logit), F32)], axis=1)
    b_route = jnp.concatenate([b_grp, b_exp, jnp.zeros((128 - n_logit,), F32)]).reshape(1, 128)
    h2, route = _route(rows, z, norm_g, modtab, w_route, b_route, n_rows=n_rows)
    rank, totals = _rank(route, tm=rows.tile_rows(512))
    eid = route[:, :TOP_K].astype(jnp.int32)
    wts = route[:, TOP_K:2 * TOP_K]
    counts = totals[0, :N_EXPERTS].astype(jnp.int32)
    padded = (counts + MOE_BLOCK - 1) // MOE_BLOCK * MOE_BLOCK
    pad_end = jnp.cumsum(padded)
    pad_start = pad_end - padded
    dest = pad_start[eid] + rank[:, :TOP_K].astype(jnp.int32)
    n_assign = n_rows * TOP_K
    n_blocks = (n_assign + N_EXPERTS * (MOE_BLOCK - 1) + MOE_BLOCK - 1) // MOE_BLOCK
    n_pad_rows = n_blocks * MOE_BLOCK
    tok = jnp.broadcast_to(jnp.arange(n_rows, dtype=jnp.int32)[:, None], (n_rows, TOP_K))
    row_tok = jnp.full((n_pad_rows,), n_rows, jnp.int32).at[dest.reshape(-1)].set(tok.reshape(-1))
    blk_expert = jnp.minimum(jnp.searchsorted(pad_end, jnp.arange(n_blocks) * MOE_BLOCK, side='right'),
                             N_EXPERTS - 1).astype(jnp.int32)
    first = jnp.concatenate([jnp.ones((1,), jnp.int32), (blk_expert[1:] != blk_expert[:-1]).astype(jnp.int32)])
    n_used = (pad_end[-1] // MOE_BLOCK).astype(jnp.int32).reshape(1)
    h2_pad = jnp.concatenate([h2, jnp.zeros((1, d), h2.dtype)], axis=0)
    xs = h2_pad[row_tok]
    ys = _expert_blocks(xs, blk_expert, first, n_used, w_gate, w_up, w_down, layer)
    return ys[dest[:, 0]] * wts[:, 0:1] + ys[dest[:, 1]] * wts[:, 1:2]


def kernel(x, c, ctx, c_ctx, mod_w, mod_b, norm1_g, norm2_g, w_in, dn_conv, dn_a_log, dn_dt_bias, dn_norm_g, dn_out, lru_conv, lru_conv_b, lru_wa, lru_ba, lru_wx, lru_bx, lru_lambda, lru_out, att_qn_g, att_kn_g, att_sink, att_out, w_o, moe_w_grp, moe_b_grp, moe_w_exp, moe_b_exp, moe_w_gate, moe_w_up, moe_w_down):
    batch, seq, d = x.shape
    n_ctx = ctx.shape[1]
    depth = mod_w.shape[0]
    rows = _Rows(batch, seq, n_ctx)
    assert d == D_MODEL and batch + 1 <= 8

    z = jnp.concatenate([x.reshape(batch * seq, d), ctx.reshape(batch * n_ctx, d)], axis=0)

    cc = jnp.concatenate([c_ctx[None], c, jnp.zeros((7 - batch, d), F32)], axis=0)
    mods = _mod_tables(cc, mod_w, mod_b).reshape(depth, 8, 6, d)
    mods = jnp.concatenate([mods, jnp.zeros((depth, 8, 2, d), F32)], axis=2)

    cos, sin = _rope_tables(seq, n_ctx)
    pc = _ProjCols
    w_proj = _relayout_w_in(w_in)

    for l in range(depth):
        last = l == depth - 1
        modtab = mods[l]
        n_out = rows.n_lat_rows if last else rows.n_rows

        p = _norm_matmul(rows, z, norm1_g[l], modtab, w_proj, layer=l, shift_row=0, scale_row=1, tn=pc.tile,
                         name="in_proj")

        qkv = _dn_prep(rows, p, dn_conv[l])
        par = jnp.zeros((2, 8, 128), F32)
        par = par.at[:, 0, :N_HEADS].set(dn_a_log[l]).at[:, 1, :N_HEADS].set(dn_dt_bias[l])
        o_f = _dn_scan(rows, qkv, p, par, small_col=pc.scalars, reverse=False)
        o_b = _dn_scan(rows, qkv, p, par, small_col=pc.scalars, reverse=True)

        lru_args = (lru_conv[l], lru_conv_b[l], lru_wa[l], lru_wx[l], lru_ba[l], lru_bx[l], lru_lambda[l])
        hs_f = _lru_scan(rows, p, *lru_args, x_col=pc.lx, reverse=False)
        hs_b = _lru_scan(rows, p, *lru_args, x_col=pc.lx, reverse=True)

        qn, kn = _qk_prep(rows, p, cos, sin, att_qn_g[l], att_kn_g[l], pc.aq, pc.ak)
        sink_b = jnp.broadcast_to(att_sink[l][:, None], (N_HEADS, 128))
        y_c = _attention(rows, qn, kn, p, pc.av, sink_b, with_ctx_queries=not last)

        merged = _merge(rows, o_f, o_b, p, hs_f, hs_b, p, y_c, dn_out[l].astype(BF16), lru_out[l].astype(BF16),
                        att_out[l].astype(BF16), dn_norm_g[l], z_col=pc.z, lg_col=pc.lg, gate_col=pc.gates,
                        n_rows=n_out)
        z = _mm_residual(rows, merged, w_o[l].astype(BF16), z, modtab, gate_row=2, n_rows=n_out)

        f = _hier_moe(rows, z, norm2_g[l], modtab, moe_w_grp[l], moe_b_grp[l], moe_w_exp[l], moe_b_exp[l],
                      moe_w_gate, moe_w_up, moe_w_down, l, n_rows=n_out)
        gate2 = jnp.concatenate([jnp.repeat(modtab[1:1 + batch, 5], seq, axis=0),
                                 jnp.broadcast_to(modtab[0, 5], (batch * n_ctx, d))], axis=0)[:n_out]
        z = z + gate2 * f

    return z[:rows.n_lat_rows].reshape(batch, seq, d)
```

```python
import functools

import jax
import jax.numpy as jnp
from jax import lax
from jax.experimental import pallas as pl
from jax.experimental.pallas import tpu as pltpu

F32 = jnp.float32
BF16 = jnp.bfloat16

EPS = 1e-6
D_MODEL = 2048
N_HEADS = 8
HEAD_DIM = 128
MIX_W = N_HEADS * HEAD_DIM
KV_HEADS = 2
KV_W = KV_HEADS * HEAD_DIM
DN_CHUNK = 64
DN_UNIT = 128
CONV_W = 4
LRU_C = 8.0
ATT_WINDOW = 128
GRID_W = 64
ROPE_THETA = 10000.0
N_GROUPS = 8
EXPERTS_PER_GROUP = 8
N_EXPERTS = N_GROUPS * EXPERTS_PER_GROUP
TOP_K = 2
D_EXPERT = 512
MOE_BLOCK = 128
ROW_BLK = 256
NEG_BIG = -1e30
VMEM_LIMIT = 56 * 1024 * 1024


def _cparams(*sem):
    return pltpu.CompilerParams(dimension_semantics=sem, vmem_limit_bytes=VMEM_LIMIT)


def _bdot(a, b):
    return jnp.dot(a.astype(BF16), b.astype(BF16), preferred_element_type=F32)


def _bdot_nt(a, b):
    return lax.dot_general(a.astype(BF16), b.astype(BF16), (((1,), (1,)), ((), ())),
                           preferred_element_type=F32)


def _dot01(m01, x):
    m = m01.astype(BF16)
    x0 = x.astype(BF16)
    r1 = x - x0.astype(F32)
    x1 = r1.astype(BF16)
    x2 = (r1 - x1.astype(F32)).astype(BF16)
    dot = lambda t: jnp.dot(m, t, preferred_element_type=F32)
    return dot(x0) + dot(x1) + dot(x2)


def _silu(x):
    return x * jax.nn.sigmoid(x)


def _softplus(x):
    return jnp.maximum(x, 0.0) + jnp.log1p(jnp.exp(-jnp.abs(x)))


class _Rows:
    def __init__(self, batch, seq, n_ctx):
        assert seq % ROW_BLK == 0 and n_ctx % ROW_BLK == 0
        self.batch, self.seq, self.n_ctx = batch, seq, n_ctx
        self.nlat = seq // ROW_BLK
        self.nctx = n_ctx // ROW_BLK
        self.n_lat_rows = batch * seq
        self.n_rows = batch * (seq + n_ctx)
        self.steps = self.nlat + self.nctx

    def seq_block(self, b, t, reverse):
        if reverse:
            jc, jl = self.nctx - 1 - t, self.nlat - 1 - (t - self.nctx)
        else:
            jc, jl = t, t - self.nctx
        return jnp.where(t < self.nctx, self.batch * self.nlat + b * self.nctx + jc, b * self.nlat + jl)

    def mod_index(self, tile, tm):
        r0 = tile * tm
        return jnp.where(r0 < self.n_lat_rows, 1 + r0 // self.seq, 0)

    def tile_rows(self, cap):
        tm = cap
        while self.seq % tm or (self.batch * self.n_ctx) % tm:
            tm //= 2
        return tm


def _modtab_kernel(c_ref, w_ref, b_ref, o_ref):
    o_ref[0] = _bdot(_silu(c_ref[...]), w_ref[0]) + b_ref[0]


def _mod_tables(cc, mod_w, mod_b):
    depth, d, n = mod_w.shape
    tn = 1024
    return pl.pallas_call(
        _modtab_kernel,
        grid=(depth, n // tn),
        in_specs=[pl.BlockSpec((8, d), lambda l, j: (0, 0)),
                  pl.BlockSpec((1, d, tn), lambda l, j: (l, 0, j)),
                  pl.BlockSpec((1, 1, tn), lambda l, j: (l, 0, j))],
        out_specs=pl.BlockSpec((1, 8, tn), lambda l, j: (l, 0, j)),
        out_shape=jax.ShapeDtypeStruct((depth, 8, n), F32),
        compiler_params=_cparams("parallel", "parallel"),
        name="mod_tables",
    )(cc, mod_w, mod_b.reshape(depth, 1, n))


class _ProjCols:
    tile = 512
    small = 4 * N_HEADS
    qkvz = 4 * MIX_W
    rest = 3 * MIX_W + 2 * KV_W + 3 * D_MODEL
    z, lx, lg, aq = 3 * MIX_W, 4 * MIX_W, 5 * MIX_W, 6 * MIX_W
    ak = aq + MIX_W
    av = ak + KV_W
    gates = av + KV_W
    scalars = qkvz + rest
    total = scalars + 4 * 128


def _relayout_kernel(a_ref, b_ref, s_ref, o_ref, *, n_aligned, n_tiles, shift):
    j = pl.program_id(1)

    @pl.when(j < n_aligned)
    def _():
        o_ref[0] = a_ref[0].astype(BF16)

    @pl.when((j >= n_aligned) & (j < n_tiles - 1))
    def _():
        o_ref[0] = jnp.concatenate([a_ref[0][:, shift:], b_ref[0][:, :shift]], axis=1).astype(BF16)

    @pl.when(j == n_tiles - 1)
    def _():
        o_ref[0] = s_ref[0].astype(BF16)


def _relayout_w_in(w_in):
    depth, d, n_in = w_in.shape
    pc = _ProjCols
    assert n_in == pc.qkvz + pc.small + pc.rest and pc.qkvz % pc.tile == 0 and pc.rest % pc.tile == 0
    n_aligned = pc.qkvz // pc.tile
    n_tiles = pc.total // pc.tile
    per = pc.tile // 128
    w_small = w_in[:, :, pc.qkvz:pc.qkvz + pc.small].reshape(depth, d, 4, N_HEADS)
    w_small = jnp.pad(w_small, ((0, 0), (0, 0), (0, 0), (0, 128 - N_HEADS))).reshape(depth, d, 4 * 128)
    return pl.pallas_call(
        functools.partial(_relayout_kernel, n_aligned=n_aligned, n_tiles=n_tiles, shift=pc.small),
        grid=(depth, n_tiles),
        in_specs=[pl.BlockSpec((1, d, pc.tile), lambda l, j: (l, 0, jnp.minimum(j, n_tiles - 2))),
                  pl.BlockSpec((1, d, 128), lambda l, j: (l, 0, jnp.minimum(j + 1, n_tiles - 1) * per)),
                  pl.BlockSpec((1, d, pc.tile), lambda l, j: (l, 0, 0))],
        out_specs=pl.BlockSpec((1, d, pc.tile), lambda l, j: (l, 0, j)),
        out_shape=jax.ShapeDtypeStruct((depth, d, pc.total), BF16),
        compiler_params=_cparams("parallel", "parallel"),
        name="relayout_w_in",
    )(w_in, w_in, w_small)


def _norm_mm_kernel(z_ref, g_ref, mod_ref, w_ref, o_ref, *rest, shift_row, scale_row, emit_h):
    h_scr = rest[-1]
    tm = z_ref.shape[0]
    sub = min(tm, 128)

    @pl.when(pl.program_id(1) == 0)
    def _():
        g = g_ref[...]
        m = mod_ref[0]
        scale1 = 1.0 + m[scale_row:scale_row + 1]
        shift = m[shift_row:shift_row + 1]

        def body(i, carry):
            r0 = pl.multiple_of(i * sub, sub)
            z = z_ref[pl.ds(r0, sub), :]
            y = z * lax.rsqrt(jnp.mean(z * z, axis=-1, keepdims=True) + EPS) * g
            h_scr[pl.ds(r0, sub), :] = (y * scale1 + shift).astype(BF16)
            return carry

        lax.fori_loop(0, tm // sub, body, 0)

    h = h_scr[...]
    o_ref[...] = jnp.dot(h, w_ref[...].astype(BF16), preferred_element_type=F32)
    if emit_h:
        @pl.when(pl.program_id(1) == 0)
        def _():
            rest[0][...] = h


def _norm_matmul(rows, z, g, modtab, w, *, shift_row, scale_row, tn, layer=None, tm_cap=1024,
                 n_rows=None, emit_h=False, name="norm_matmul"):
    n_rows = rows.n_rows if n_rows is None else n_rows
    d = z.shape[1]
    n_cols = w.shape[-1]
    tm = rows.tile_rows(tm_cap)
    assert n_rows % tm == 0 and n_cols % tn == 0
    if layer is None:
        w_spec = pl.BlockSpec((d, tn), lambda i, j: (0, j))
    else:
        w_spec = pl.BlockSpec((None, d, tn), lambda i, j: (layer, 0, j))
    out_shape = [jax.ShapeDtypeStruct((n_rows, n_cols), F32)]
    out_specs = [pl.BlockSpec((tm, tn), lambda i, j: (i, j))]
    if emit_h:
        out_shape.append(jax.ShapeDtypeStruct((n_rows, d), BF16))
        out_specs.append(pl.BlockSpec((tm, d), lambda i, j: (i, 0)))
    res = pl.pallas_call(
        functools.partial(_norm_mm_kernel, shift_row=shift_row, scale_row=scale_row, emit_h=emit_h),
        grid=(n_rows // tm, n_cols // tn),
        in_specs=[pl.BlockSpec((tm, d), lambda i, j: (i, 0)),
                  pl.BlockSpec((1, d), lambda i, j: (0, 0)),
                  pl.BlockSpec((1, 8, d), lambda i, j: (rows.mod_index(i, tm), 0, 0)),
                  w_spec],
        out_specs=out_specs,
        out_shape=out_shape,
        scratch_shapes=[pltpu.VMEM((tm, d), BF16)],
        compiler_params=_cparams("parallel", "arbitrary"),
        name=name,
    )(z, g.reshape(1, d), modtab, w)
    return res if emit_h else res[0]


def _halo_flags(rows, blk):
    is_lat = blk < rows.batch * rows.nlat
    j = jnp.where(is_lat, blk % rows.nlat, (blk - rows.batch * rows.nlat) % rows.nctx)
    nseg = jnp.where(is_lat, rows.nlat, rows.nctx)
    return (j != 0).astype(F32), (j != nseg - 1).astype(F32)


def _conv4(prev8, cur, next8, w, use_prev, use_next):
    n = cur.shape[0]
    x = jnp.concatenate([prev8 * use_prev, cur, next8 * use_next], axis=0)
    tot = n + 16
    y = w[1:2] * cur
    y = y + w[0:1] * pltpu.roll(x, 1, 0)[8:8 + n]
    y = y + w[2:3] * pltpu.roll(x, tot - 1, 0)[8:8 + n]
    y = y + w[3:4] * pltpu.roll(x, tot - 2, 0)[8:8 + n]
    return y


def _halo_specs(rows, width, col_blk, blk_of):
    per = ROW_BLK // 8
    last8 = rows.n_rows // 8 - 1
    return [
        pl.BlockSpec((8, width), lambda *g: (jnp.maximum(blk_of(*g) * per - 1, 0), col_blk(*g))),
        pl.BlockSpec((ROW_BLK, width), lambda *g: (blk_of(*g), col_blk(*g))),
        pl.BlockSpec((8, width), lambda *g: (jnp.minimum((blk_of(*g) + 1) * per, last8), col_blk(*g))),
    ]


def _dn_prep_kernel(prev_ref, cur_ref, next_ref, w_ref, o_ref, *, rows):
    blk = pl.program_id(0)
    kind = pl.program_id(1)
    use_prev, use_next = _halo_flags(rows, blk)
    y = _silu(_conv4(prev_ref[...], cur_ref[...], next_ref[...], w_ref[...], use_prev, use_next))
    q_scale = jnp.where(kind == 0, HEAD_DIM ** -0.5, 1.0)
    for h in range(N_HEADS):
        sl = slice(h * HEAD_DIM, (h + 1) * HEAD_DIM)
        yh = y[:, sl]
        inv = lax.rsqrt(jnp.sum(yh * yh, axis=-1, keepdims=True) + EPS)
        o_ref[:, sl] = yh * (jnp.where(kind == 2, 1.0, inv) * q_scale)


def _dn_prep(rows, p_a, conv_w):
    return pl.pallas_call(
        functools.partial(_dn_prep_kernel, rows=rows),
        grid=(rows.n_rows // ROW_BLK, 3),
        in_specs=_halo_specs(rows, MIX_W, lambda i, j: j, lambda i, j: i)
        + [pl.BlockSpec((CONV_W, MIX_W), lambda i, j: (0, j))],
        out_specs=pl.BlockSpec((ROW_BLK, MIX_W), lambda i, j: (i, j)),
        out_shape=jax.ShapeDtypeStruct((rows.n_rows, 3 * MIX_W), F32),
        compiler_params=_cparams("parallel", "parallel"),
        name="dn_prep",
    )(p_a, p_a, p_a, conv_w)


def _dot_b(a, b):
    return jnp.dot(a, b, preferred_element_type=F32).astype(BF16)


def _dn_scan_kernel(q_ref, k_ref, v_ref, beta_ref, alpha_ref, par_ref, o_ref,
                    s_scr, u_scr, wq_scr, kdt_scr, aqk_scr, *, reverse):
    n = ROW_BLK
    nchunk = n // DN_CHUNK
    per_unit = DN_UNIT // DN_CHUNK

    @pl.when(pl.program_id(1) == 0)
    def _():
        s_scr[...] = jnp.zeros_like(s_scr)

    def tri(m):
        row = lax.broadcasted_iota(jnp.int32, (m, m), 0)
        col = lax.broadcasted_iota(jnp.int32, (m, m), 1)
        same = lambda s: (row >> (s.bit_length() - 1)) == (col >> (s.bit_length() - 1))
        ahead = (col >= row) if reverse else (col <= row)
        return row, col, same, same(DN_CHUNK) & ahead, same(DN_CHUNK) & ahead & (row != col)

    _, _, same_n, incl_n, _ = tri(n)
    par = par_ref[0]
    beta_all = jax.nn.sigmoid(beta_ref[...])
    g_all = -jnp.exp(par[0:1]) * _softplus(alpha_ref[...] + par[1:2])
    gc_all = _dot01(incl_n.astype(F32), g_all)
    gt_all = _dot01(same_n(DN_CHUNK).astype(F32), g_all)
    gc_t = gc_all.T
    egc_all = jnp.exp(gc_all)
    ekd_all = jnp.exp(gt_all - gc_all)
    egt_all = jnp.exp(gt_all)

    row_u, col_u, same_u, incl_u, strict_u = tri(DN_UNIT)
    one_b = lambda m: jnp.where(m, 1.0, 0.0).astype(BF16)
    eye_b = one_b(row_u == col_u)
    diag8_b = one_b(same_u(8))
    off_b = {s: one_b(same_u(2 * s) & jnp.logical_not(same_u(s))) for s in (8, 16, 32)}

    units = [(h, j) for h in range(N_HEADS) for j in range(n // DN_UNIT)]
    a_b, rhs_b = [], []
    for h, j in units:
        rs = slice(j * DN_UNIT, (j + 1) * DN_UNIT)
        sl = slice(h * HEAD_DIM, (h + 1) * HEAD_DIM)
        q, k, v = q_ref[rs, sl], k_ref[rs, sl], v_ref[rs, sl]
        bcol = beta_all[rs, h:h + 1]
        egc = egc_all[rs, h:h + 1]
        dec = jnp.exp(jnp.where(incl_u, gc_all[rs, h:h + 1] - gc_t[h:h + 1, rs], NEG_BIG))
        kb = k * bcol
        k_b = k.astype(BF16)
        a_b.append(jnp.where(strict_u, _bdot_nt(kb, k_b) * dec, 0.0).astype(BF16))
        aqk_scr[h, rs, :] = (_bdot_nt(q, k_b) * dec).astype(BF16)
        rhs_b.append(jnp.concatenate([v * bcol, kb * egc], axis=1).astype(BF16))
        qd = (q * egc).astype(BF16)
        kd = k * ekd_all[rs, h:h + 1]
        for c in range(per_unit):
            cs = slice(c * DN_CHUNK, (c + 1) * DN_CHUNK)
            wq_scr[h, j * per_unit + c, DN_CHUNK:, :] = qd[cs]
            kdt_scr[h, j * per_unit + c] = kd[cs].T.astype(BF16)

    d1 = [a * diag8_b for a in a_b]
    d2 = [_dot_b(d, d) for d in d1]
    d4 = [_dot_b(d, d) for d in d2]
    x = [_dot_b(eye_b - d, eye_b + e) for d, e in zip(d1, d2)]
    x = [_dot_b(xx, eye_b + e) for xx, e in zip(x, d4)]
    for s in (8, 16, 32):
        t = [_dot_b(a * off_b[s], xx) for a, xx in zip(a_b, x)]
        x = [xx - _dot_b(xx, tt) for xx, tt in zip(x, t)]
    for (h, j), xx, rhs in zip(units, x, rhs_b):
        uw = jnp.dot(xx, rhs, preferred_element_type=F32)
        u_scr[h, j * DN_UNIT:(j + 1) * DN_UNIT, :] = uw[:, :HEAD_DIM]
        for c in range(per_unit):
            wq_scr[h, j * per_unit + c, :DN_CHUNK, :] = uw[c * DN_CHUNK:(c + 1) * DN_CHUNK, HEAD_DIM:].astype(BF16)

    zeros_b = jnp.zeros((DN_CHUNK, HEAD_DIM), BF16)
    for ci in range(nchunk):
        c = nchunk - 1 - ci if reverse else ci
        rs = slice(c * DN_CHUNK, (c + 1) * DN_CHUNK)
        for h in range(N_HEADS):
            s_h = s_scr[h]
            r = jnp.dot(wq_scr[h, c], s_h.astype(BF16), preferred_element_type=F32)
            v_new = (u_scr[h, rs, :] - r[:DN_CHUNK]).astype(BF16)
            v_unit = jnp.concatenate([v_new, zeros_b] if c % per_unit == 0 else [zeros_b, v_new], axis=0)
            o = r[DN_CHUNK:] + jnp.dot(aqk_scr[h, rs, :], v_unit, preferred_element_type=F32)
            o_ref[rs, h * HEAD_DIM:(h + 1) * HEAD_DIM] = o
            s_scr[h] = s_h * egt_all[c * DN_CHUNK:c * DN_CHUNK + 1, h:h + 1] + jnp.dot(
                kdt_scr[h, c], v_new, preferred_element_type=F32)


def _dn_scan(rows, qkv, p, par, *, small_col, reverse):
    d = 1 if reverse else 0
    blk = lambda b, t: rows.seq_block(b, t, reverse)
    nchunk = ROW_BLK // DN_CHUNK
    sc = small_col // 128
    return pl.pallas_call(
        functools.partial(_dn_scan_kernel, reverse=reverse),
        grid=(rows.batch, rows.steps),
        in_specs=[pl.BlockSpec((ROW_BLK, MIX_W), lambda b, t: (blk(b, t), 0)),
                  pl.BlockSpec((ROW_BLK, MIX_W), lambda b, t: (blk(b, t), 1)),
                  pl.BlockSpec((ROW_BLK, MIX_W), lambda b, t: (blk(b, t), 2)),
                  pl.BlockSpec((ROW_BLK, 128), lambda b, t: (blk(b, t), sc + d)),
                  pl.BlockSpec((ROW_BLK, 128), lambda b, t: (blk(b, t), sc + 2 + d)),
                  pl.BlockSpec((1, 8, 128), lambda b, t: (d, 0, 0))],
        out_specs=pl.BlockSpec((ROW_BLK, MIX_W), lambda b, t: (blk(b, t), 0)),
        out_shape=jax.ShapeDtypeStruct((rows.n_rows, MIX_W), F32),
        scratch_shapes=[pltpu.VMEM((N_HEADS, HEAD_DIM, HEAD_DIM), F32),
                        pltpu.VMEM((N_HEADS, ROW_BLK, HEAD_DIM), F32),
                        pltpu.VMEM((N_HEADS, nchunk, 2 * DN_CHUNK, HEAD_DIM), BF16),
                        pltpu.VMEM((N_HEADS, nchunk, HEAD_DIM, DN_CHUNK), BF16),
                        pltpu.VMEM((N_HEADS, ROW_BLK, DN_UNIT), BF16)],
        compiler_params=_cparams("parallel", "arbitrary"),
        name="dn_scan_bwd" if reverse else "dn_scan_fwd",
    )(qkv, qkv, qkv, p, p, par)


def _lru_kernel(prev_ref, cur_ref, next_ref, cw_ref, cb_ref, wa_ref, wx_ref, ba_ref, bx_ref, lam_ref,
                o_ref, h_scr, a_scr, u_scr, *, rows, reverse):
    b, t = pl.program_id(0), pl.program_id(1)

    @pl.when(t == 0)
    def _():
        h_scr[...] = jnp.zeros_like(h_scr)

    use_prev, use_next = _halo_flags(rows, rows.seq_block(b, t, reverse))
    xc = _conv4(prev_ref[...], cur_ref[...], next_ref[...], cw_ref[...], use_prev, use_next) + cb_ref[...]
    for kb in range(N_HEADS):
        sl = slice(kb * HEAD_DIM, (kb + 1) * HEAD_DIM)
        xb = xc[:, sl]
        r = jax.nn.sigmoid(_bdot(xb, wa_ref[0, kb]) + ba_ref[0, :, sl])
        i = jax.nn.sigmoid(_bdot(xb, wx_ref[0, kb]) + bx_ref[0, :, sl])
        log_a = -LRU_C * r * _softplus(-lam_ref[0, :, sl])
        a_scr[:, sl] = jnp.exp(log_a)
        u_scr[:, sl] = jnp.sqrt(1.0 - jnp.exp(2.0 * log_a)) * (i * xb)

    ngrp = ROW_BLK // 8
    sub = lax.broadcasted_iota(jnp.int32, (8, MIX_W), 0)

    def group_step(gi, h_prev):
        g = ngrp - 1 - gi if reverse else gi
        r0 = pl.multiple_of(g * 8, 8)
        a = a_scr[pl.ds(r0, 8), :]
        u = u_scr[pl.ds(r0, 8), :]
        for dist in (1, 2, 4):
            if reverse:
                a_sh, u_sh, m = pltpu.roll(a, 8 - dist, 0), pltpu.roll(u, 8 - dist, 0), sub < 8 - dist
            else:
                a_sh, u_sh, m = pltpu.roll(a, dist, 0), pltpu.roll(u, dist, 0), sub >= dist
            u = jnp.where(m, a * u_sh + u, u)
            a = jnp.where(m, a * a_sh, a)
        h = u + a * h_prev
        o_ref[pl.ds(r0, 8), :] = h
        return h[0:1] if reverse else h[7:8]

    h_scr[...] = lax.fori_loop(0, ngrp, group_step, h_scr[...])


def _lru_scan(rows, p_b, cw, cb, wa, wx, ba, bx, lam, *, x_col, reverse):
    d = 1 if reverse else 0
    blk = lambda b, t: rows.seq_block(b, t, reverse)
    vec = lambda: pl.BlockSpec((1, 1, MIX_W), lambda b, t: (d, 0, 0))
    mat = lambda: pl.BlockSpec((1, N_HEADS, HEAD_DIM, HEAD_DIM), lambda b, t: (d, 0, 0, 0))
    return pl.pallas_call(
        functools.partial(_lru_kernel, rows=rows, reverse=reverse),
        grid=(rows.batch, rows.steps),
        in_specs=_halo_specs(rows, MIX_W, lambda b, t: x_col // MIX_W, blk)
        + [pl.BlockSpec((CONV_W, MIX_W), lambda b, t: (0, 0)),
           pl.BlockSpec((1, MIX_W), lambda b, t: (0, 0)),
           mat(), mat(), vec(), vec(), vec()],
        out_specs=pl.BlockSpec((ROW_BLK, MIX_W), lambda b, t: (blk(b, t), 0)),
        out_shape=jax.ShapeDtypeStruct((rows.n_rows, MIX_W), F32),
        scratch_shapes=[pltpu.VMEM((1, MIX_W), F32),
                        pltpu.VMEM((ROW_BLK, MIX_W), F32),
                        pltpu.VMEM((ROW_BLK, MIX_W), F32)],
        compiler_params=_cparams("parallel", "arbitrary"),
        name="lru_bwd" if reverse else "lru_fwd",
    )(p_b, p_b, p_b, cw, cb.reshape(1, MIX_W), wa, wx,
      ba.reshape(2, 1, MIX_W), bx.reshape(2, 1, MIX_W), lam.reshape(2, 1, MIX_W))


def _rope_tables(seq, n_ctx):
    half = HEAD_DIM // 2
    pos = jnp.arange(seq)
    inv = ROPE_THETA ** (-jnp.arange(0, half, 2, dtype=F32) / half)
    ang_r = (pos // GRID_W).astype(F32)[:, None] * inv
    ang_c = (pos % GRID_W).astype(F32)[:, None] * inv
    cos = jnp.concatenate([jnp.cos(ang_r)] * 2 + [jnp.cos(ang_c)] * 2, axis=-1)
    sin = jnp.concatenate([-jnp.sin(ang_r), jnp.sin(ang_r), -jnp.sin(ang_c), jnp.sin(ang_c)], axis=-1)
    cos = jnp.concatenate([cos, jnp.ones((n_ctx, HEAD_DIM), F32)], axis=0)
    sin = jnp.concatenate([sin, jnp.zeros((n_ctx, HEAD_DIM), F32)], axis=0)
    return cos, sin


def _qk_prep_kernel(q_ref, k_ref, cos_ref, sin_ref, qg_ref, kg_ref, qo_ref, ko_ref):
    cos, sin = cos_ref[...], sin_ref[...]
    lane = lax.broadcasted_iota(jnp.int32, cos.shape, 1)
    first = (lane & (HEAD_DIM // 2 - 1)) < (HEAD_DIM // 4)

    def norm_rope(x, g):
        y = x * lax.rsqrt(jnp.mean(x * x, axis=-1, keepdims=True) + EPS) * g
        partner = jnp.where(first, pltpu.roll(y, HEAD_DIM - HEAD_DIM // 4, 1), pltpu.roll(y, HEAD_DIM // 4, 1))
        return y * cos + partner * sin

    for h in range(N_HEADS):
        sl = slice(h * HEAD_DIM, (h + 1) * HEAD_DIM)
        qo_ref[:, sl] = (norm_rope(q_ref[:, sl], qg_ref[...]) * (HEAD_DIM ** -0.5)).astype(BF16)
    for h in range(KV_HEADS):
        sl = slice(h * HEAD_DIM, (h + 1) * HEAD_DIM)
        ko_ref[:, sl] = norm_rope(k_ref[:, sl], kg_ref[...]).astype(BF16)


def _qk_prep(rows, p_b, cos, sin, qn_g, kn_g, q_col, k_col):
    def tab(i):
        return jnp.where(i < rows.batch * rows.nlat, i % rows.nlat,
                         rows.nlat + (i - rows.batch * rows.nlat) % rows.nctx)

    return pl.pallas_call(
        _qk_prep_kernel,
        grid=(rows.n_rows // ROW_BLK,),
        in_specs=[pl.BlockSpec((ROW_BLK, MIX_W), lambda i: (i, q_col // MIX_W)),
                  pl.BlockSpec((ROW_BLK, KV_W), lambda i: (i, k_col // KV_W)),
                  pl.BlockSpec((ROW_BLK, HEAD_DIM), lambda i: (tab(i), 0)),
                  pl.BlockSpec((ROW_BLK, HEAD_DIM), lambda i: (tab(i), 0)),
                  pl.BlockSpec((1, HEAD_DIM), lambda i: (0, 0)),
                  pl.BlockSpec((1, HEAD_DIM), lambda i: (0, 0))],
        out_specs=[pl.BlockSpec((ROW_BLK, MIX_W), lambda i: (i, 0)),
                   pl.BlockSpec((ROW_BLK, KV_W), lambda i: (i, 0))],
        out_shape=[jax.ShapeDtypeStruct((rows.n_rows, MIX_W), BF16),
                   jax.ShapeDtypeStruct((rows.n_rows, KV_W), BF16)],
        compiler_params=_cparams("parallel"),
        name="qk_prep",
    )(p_b, p_b, cos, sin, qn_g.reshape(1, HEAD_DIM), kn_g.reshape(1, HEAD_DIM))


def _attn_core(q, keys, vals, sink_ref, valid, o_ref):
    grp = N_HEADS // KV_HEADS
    nq = q.shape[0]
    for kvh in range(KV_HEADS):
        sl = slice(kvh * HEAD_DIM, (kvh + 1) * HEAD_DIM)
        kk = jnp.concatenate([t[:, sl] for t in keys], axis=0)
        vv = jnp.concatenate([t[:, sl].astype(BF16) for t in vals], axis=0)
        q4 = jnp.concatenate([q[:, (kvh * grp + g) * HEAD_DIM:(kvh * grp + g + 1) * HEAD_DIM]
                              for g in range(grp)], axis=0)
        s = _bdot_nt(q4, kk)
        if valid is not None:
            s = jnp.where(valid, s, NEG_BIG)
        sink = jnp.concatenate([jnp.broadcast_to(sink_ref[kvh * grp + g:kvh * grp + g + 1, 0:1], (nq, 1))
                                for g in range(grp)], axis=0)
        m = jnp.maximum(jnp.max(s, axis=-1, keepdims=True), sink)
        e = jnp.exp(s - m)
        p = e / (jnp.sum(e, axis=-1, keepdims=True) + jnp.exp(sink - m))
        o = _bdot(p, vv)
        for g in range(grp):
            hq = kvh * grp + g
            o_ref[:, hq * HEAD_DIM:(hq + 1) * HEAD_DIM] = o[g * nq:(g + 1) * nq].astype(BF16)


def _attn_latent_kernel(q_ref, k0_ref, k1_ref, k2_ref, kc_ref, v0_ref, v1_ref, v2_ref, vc_ref, sink_ref, o_ref,
                        *, seq):
    n = pl.program_id(1)
    blk = q_ref.shape[0]
    n_loc = 3 * blk
    n_ctx = kc_ref.shape[0]
    grp = N_HEADS // KV_HEADS
    qi = lax.broadcasted_iota(jnp.int32, (grp * blk, n_loc + n_ctx), 0) & (blk - 1)
    kj = lax.broadcasted_iota(jnp.int32, (grp * blk, n_loc + n_ctx), 1)
    rel = kj - blk
    kpos = n * blk + rel
    local_ok = (jnp.abs(qi - rel) <= ATT_WINDOW) & (kpos >= 0) & (kpos < seq)
    valid = local_ok | (kj >= n_loc)
    _attn_core(q_ref[...], [k0_ref[...], k1_ref[...], k2_ref[...], kc_ref[...]],
               [v0_ref[...], v1_ref[...], v2_ref[...], vc_ref[...]], sink_ref, valid, o_ref)


def _attn_ctx_kernel(q_ref, kc_ref, vc_ref, sink_ref, o_ref):
    _attn_core(q_ref[...], [kc_ref[...]], [vc_ref[...]], sink_ref, None, o_ref)


def _attention(rows, qn, kn, p_b, v_col, sink_b, *, with_ctx_queries):
    blk = 128
    nb = rows.seq // blk
    vcb = v_col // KV_W
    ctx_blk0 = rows.n_lat_rows // rows.n_ctx
    assert rows.n_lat_rows % rows.n_ctx == 0
    kspec = lambda off: pl.BlockSpec((blk, KV_W), lambda b, n: (b * nb + jnp.clip(n + off, 0, nb - 1), 0))
    vspec = lambda off: pl.BlockSpec((blk, KV_W), lambda b, n: (b * nb + jnp.clip(n + off, 0, nb - 1), vcb))
    n_out = rows.n_rows if with_ctx_queries else rows.n_lat_rows
    y_lat = pl.pallas_call(
        functools.partial(_attn_latent_kernel, seq=rows.seq),
        grid=(rows.batch, nb),
        in_specs=[pl.BlockSpec((blk, MIX_W), lambda b, n: (b * nb + n, 0)),
                  kspec(-1), kspec(0), kspec(1),
                  pl.BlockSpec((rows.n_ctx, KV_W), lambda b, n: (ctx_blk0 + b, 0)),
                  vspec(-1), vspec(0), vspec(1),
                  pl.BlockSpec((rows.n_ctx, KV_W), lambda b, n: (ctx_blk0 + b, vcb)),
                  pl.BlockSpec((8, 128), lambda b, n: (0, 0))],
        out_specs=pl.BlockSpec((blk, MIX_W), lambda b, n: (b * nb + n, 0)),
        out_shape=jax.ShapeDtypeStruct((n_out, MIX_W), BF16),
        compiler_params=_cparams("parallel", "parallel"),
        name="attn_latent",
    )(qn, kn, kn, kn, kn, p_b, p_b, p_b, p_b, sink_b)
    if not with_ctx_queries:
        return y_lat
    ncb = rows.n_ctx // blk
    q0 = rows.n_lat_rows // blk
    return pl.pallas_call(
        lambda q_ref, kc_ref, vc_ref, sink_ref, y_in_ref, o_ref: _attn_ctx_kernel(q_ref, kc_ref, vc_ref, sink_ref, o_ref),
        grid=(rows.batch, ncb),
        in_specs=[pl.BlockSpec((blk, MIX_W), lambda b, i: (q0 + b * ncb + i, 0)),
                  pl.BlockSpec((rows.n_ctx, KV_W), lambda b, i: (ctx_blk0 + b, 0)),
                  pl.BlockSpec((rows.n_ctx, KV_W), lambda b, i: (ctx_blk0 + b, vcb)),
                  pl.BlockSpec((8, 128), lambda b, i: (0, 0)),
                  pl.BlockSpec(memory_space=pl.ANY)],
        out_specs=pl.BlockSpec((blk, MIX_W), lambda b, i: (q0 + b * ncb + i, 0)),
        out_shape=jax.ShapeDtypeStruct((n_out, MIX_W), BF16),
        input_output_aliases={4: 0},
        compiler_params=_cparams("parallel", "parallel"),
        name="attn_ctx",
    )(qn, kn, p_b, sink_b, y_lat)


def _merge_kernel(of_ref, ob_ref, z_ref, hf_ref, hb_ref, lg_ref, yc_ref, ga_ref, gb_ref, gc_ref,
                  wa_ref, wb_ref, wc_ref, ng_ref, o_ref, ya_scr, yb_scr):
    tm = of_ref.shape[0]
    sub = min(tm, 128)

    @pl.when(pl.program_id(1) == 0)
    def _():
        ng = ng_ref[...]

        def body(i, carry):
            r0 = pl.multiple_of(i * sub, sub)
            rs = pl.ds(r0, sub)
            for h in range(N_HEADS):
                sl = slice(h * HEAD_DIM, (h + 1) * HEAD_DIM)
                o = of_ref[rs, sl] + ob_ref[rs, sl]
                y = o * lax.rsqrt(jnp.mean(o * o, axis=-1, keepdims=True) + EPS) * ng
                ya_scr[rs, sl] = (y * _silu(z_ref[rs, sl])).astype(BF16)
            yb_scr[rs, :] = (jax.nn.gelu(lg_ref[rs, :]) * (hf_ref[rs, :] + hb_ref[rs, :])).astype(BF16)
            return carry

        lax.fori_loop(0, tm // sub, body, 0)

    acc = jax.nn.sigmoid(ga_ref[...]) * _bdot(ya_scr[...], wa_ref[...])
    acc = acc + jax.nn.sigmoid(gb_ref[...]) * _bdot(yb_scr[...], wb_ref[...])
    acc = acc + jax.nn.sigmoid(gc_ref[...]) * _bdot(yc_ref[...], wc_ref[...])
    o_ref[...] = acc.astype(BF16)


def _merge(rows, o_f, o_b, p_a, hs_f, hs_b, p_b, y_c, dn_out, lru_out, att_out, dn_norm_g, *,
           z_col, lg_col, gate_col, n_rows):
    tm = rows.tile_rows(512)
    tn = 512
    assert n_rows % tm == 0 and gate_col % tn == 0
    row = lambda cb: pl.BlockSpec((tm, MIX_W), lambda i, j: (i, cb))
    gate = lambda br: pl.BlockSpec((tm, tn), lambda i, j: (i, (gate_col + br * D_MODEL) // tn + j))
    wgt = lambda: pl.BlockSpec((MIX_W, tn), lambda i, j: (0, j))
    return pl.pallas_call(
        _merge_kernel,
        grid=(n_rows // tm, D_MODEL // tn),
        in_specs=[row(0), row(0), row(z_col // MIX_W), row(0), row(0), row(lg_col // MIX_W), row(0),
                  gate(0), gate(1), gate(2), wgt(), wgt(), wgt(),
                  pl.BlockSpec((1, HEAD_DIM), lambda i, j: (0, 0))],
        out_specs=pl.BlockSpec((tm, tn), lambda i, j: (i, j)),
        out_shape=jax.ShapeDtypeStruct((n_rows, D_MODEL), BF16),
        scratch_shapes=[pltpu.VMEM((tm, MIX_W), BF16), pltpu.VMEM((tm, MIX_W), BF16)],
        compiler_params=_cparams("parallel", "arbitrary"),
        name="merge",
    )(o_f, o_b, p_a, hs_f, hs_b, p_b, y_c, p_b, p_b, p_b, dn_out, lru_out, att_out,
      dn_norm_g.reshape(1, HEAD_DIM))


def _mm_residual_kernel(a_ref, w_ref, z_ref, mod_ref, o_ref, *, gate_row):
    acc = _bdot(a_ref[...], w_ref[...])
    o_ref[...] = z_ref[...] + mod_ref[0][gate_row:gate_row + 1] * acc


def _mm_residual(rows, a, w, z, modtab, *, gate_row, n_rows):
    tm = rows.tile_rows(1024)
    tn = 512
    k, n = w.shape
    assert n_rows % tm == 0
    return pl.pallas_call(
        functools.partial(_mm_residual_kernel, gate_row=gate_row),
        grid=(n_rows // tm, n // tn),
        in_specs=[pl.BlockSpec((tm, k), lambda i, j: (i, 0)),
                  pl.BlockSpec((k, tn), lambda i, j: (0, j)),
                  pl.BlockSpec((tm, tn), lambda i, j: (i, j)),
                  pl.BlockSpec((1, 8, tn), lambda i, j: (rows.mod_index(i, tm), 0, j))],
        out_specs=pl.BlockSpec((tm, tn), lambda i, j: (i, j)),
        out_shape=jax.ShapeDtypeStruct((n_rows, n), F32),
        compiler_params=_cparams("parallel", "parallel"),
        name="out_proj",
    )(a, w, z, modtab)


def _route_kernel(z_ref, g_ref, mod_ref, w_ref, b_ref, h_ref, r_ref):
    m = mod_ref[0]
    z = z_ref[...]
    y = z * lax.rsqrt(jnp.mean(z * z, axis=-1, keepdims=True) + EPS) * g_ref[...]
    h = (y * (1.0 + m[4:5]) + m[3:4]).astype(BF16)
    h_ref[...] = h
    logits = jnp.dot(h, w_ref[...].astype(BF16), preferred_element_type=F32) + b_ref[...]
    lane_i = lax.broadcasted_iota(jnp.int32, logits.shape, 1)
    lane = lane_i.astype(F32)
    far = float(2 * 128)

    def top(vals):
        best = jnp.max(vals, axis=-1, keepdims=True)
        return best, jnp.min(jnp.where(vals == best, lane, far), axis=-1, keepdims=True)

    is_grp = lane_i < N_GROUPS
    g_max, g_idx = top(jnp.where(is_grp, logits, NEG_BIG))
    p_grp = 1.0 / jnp.sum(jnp.where(is_grp, jnp.exp(logits - g_max), 0.0), axis=-1, keepdims=True)
    e_lane = lane_i - N_GROUPS
    in_grp = (e_lane >= 0) & ((e_lane >> 3).astype(F32) == g_idx)
    cand = jnp.where(in_grp, logits, NEG_BIG)
    t1, i1 = top(cand)
    t2, i2 = top(jnp.where(lane == i1, NEG_BIG, cand))
    e2 = jnp.exp(t2 - t1)
    w1 = p_grp / (1.0 + e2)
    w2 = w1 * e2
    r = jnp.where(lane_i == 0, i1 - N_GROUPS, jnp.where(lane_i == 1, i2 - N_GROUPS,
                  jnp.where(lane_i == 2, w1, jnp.where(lane_i == 3, w2, 0.0))))
    r_ref[...] = r


def _route(rows, z, norm_g, modtab, w_route, b_route, *, n_rows):
    d = z.shape[1]
    tm = rows.tile_rows(256)
    return pl.pallas_call(
        _route_kernel,
        grid=(n_rows // tm,),
        in_specs=[pl.BlockSpec((tm, d), lambda i: (i, 0)),
                  pl.BlockSpec((1, d), lambda i: (0, 0)),
                  pl.BlockSpec((1, 8, d), lambda i: (rows.mod_index(i, tm), 0, 0)),
                  pl.BlockSpec((d, 128), lambda i: (0, 0)),
                  pl.BlockSpec((1, 128), lambda i: (0, 0))],
        out_specs=[pl.BlockSpec((tm, d), lambda i: (i, 0)),
                   pl.BlockSpec((tm, 128), lambda i: (i, 0))],
        out_shape=[jax.ShapeDtypeStruct((n_rows, d), BF16),
                   jax.ShapeDtypeStruct((n_rows, 128), F32)],
        compiler_params=_cparams("parallel"),
        name="moe_route",
    )(z, norm_g.reshape(1, d), modtab, w_route, b_route)


def _rank_kernel(r_ref, rank_ref, cnt_ref, carry_scr):
    @pl.when(pl.program_id(0) == 0)
    def _():
        carry_scr[...] = jnp.zeros_like(carry_scr)

    r = r_ref[...]
    tm = r.shape[0]
    lane_i = lax.broadcasted_iota(jnp.int32, r.shape, 1)
    lane = lane_i.astype(F32)
    hot1 = lane == r[:, 0:1]
    hot2 = lane == r[:, 1:2]
    cnt = jnp.where(hot1 | hot2, 1.0, 0.0)
    row = lax.broadcasted_iota(jnp.int32, (tm, tm), 0)
    col = lax.broadcasted_iota(jnp.int32, (tm, tm), 1)
    before = jnp.where(col < row, 1.0, 0.0).astype(BF16)
    carry = carry_scr[0:1, :]
    prior = jnp.dot(before, cnt.astype(BF16), preferred_element_type=F32) + carry
    rank1 = jnp.sum(jnp.where(hot1, prior, 0.0), axis=-1, keepdims=True)
    rank2 = jnp.sum(jnp.where(hot2, prior, 0.0), axis=-1, keepdims=True)
    rank_ref[...] = jnp.where(lane_i == 0, rank1, jnp.where(lane_i == 1, rank2, 0.0))
    total = carry + jnp.sum(cnt, axis=0, keepdims=True)
    carry_scr[0:1, :] = total
    cnt_ref[...] = jnp.broadcast_to(total, cnt_ref.shape)


def _rank(r, *, tm):
    n = r.shape[0]
    return pl.pallas_call(
        _rank_kernel,
        grid=(n // tm,),
        in_specs=[pl.BlockSpec((tm, 128), lambda i: (i, 0))],
        out_specs=[pl.BlockSpec((tm, 128), lambda i: (i, 0)),
                   pl.BlockSpec((8, 128), lambda i: (0, 0))],
        out_shape=[jax.ShapeDtypeStruct((n, 128), F32), jax.ShapeDtypeStruct((8, 128), F32)],
        scratch_shapes=[pltpu.VMEM((8, 128), F32)],
        compiler_params=_cparams("arbitrary"),
        name="moe_rank",
    )(r)


def _expert_kernel(be_ref, fi_ref, nu_ref, x_ref, wg_ref, wu_ref, wd_ref, *rest, blk0):
    o_ref, wg_b, wu_b, wd_b = rest[-4:]
    j = pl.program_id(0)
    i = blk0 + j

    @pl.when((i < nu_ref[0]) & ((fi_ref[i] == 1) | (j == 0)))
    def _():
        wg_b[...] = wg_ref[...].astype(BF16)
        wu_b[...] = wu_ref[...].astype(BF16)
        wd_b[...] = wd_ref[...].astype(BF16)

    @pl.when(i < nu_ref[0])
    def _():
        x = x_ref[...]
        dot = lambda a, b: jnp.dot(a, b, preferred_element_type=F32)
        hid = _silu(dot(x, wg_b[...])) * dot(x, wu_b[...])
        o_ref[...] = dot(hid.astype(BF16), wd_b[...])

    @pl.when(i >= nu_ref[0])
    def _():
        o_ref[...] = jnp.zeros_like(o_ref)


def _expert_blocks(xs, ys_prev, blk_expert, first, n_used, w_gate, w_up, w_down, layer, *, blk0, n_blocks_total):
    n_rows, d = xs.shape
    n_blocks = n_rows // MOE_BLOCK
    wspec = lambda a, b: pl.BlockSpec((None, None, a, b), lambda j, be, fi, nu: (layer, be[blk0 + j], 0, 0))
    in_specs = [pl.BlockSpec((MOE_BLOCK, d), lambda j, be, fi, nu: (j, 0)),
                wspec(d, D_EXPERT), wspec(d, D_EXPERT), wspec(D_EXPERT, d)]
    args = [blk_expert, first, n_used, xs, w_gate, w_up, w_down]
    aliases = {}
    if ys_prev is not None:
        in_specs.append(pl.BlockSpec(memory_space=pl.ANY))
        args.append(ys_prev)
        aliases = {len(args) - 1: 0}
    grid_spec = pltpu.PrefetchScalarGridSpec(
        num_scalar_prefetch=3,
        grid=(n_blocks,),
        in_specs=in_specs,
        out_specs=pl.BlockSpec((MOE_BLOCK, d), lambda j, be, fi, nu: (blk0 + j, 0)),
        scratch_shapes=[pltpu.VMEM((d, D_EXPERT), BF16), pltpu.VMEM((d, D_EXPERT), BF16),
                        pltpu.VMEM((D_EXPERT, d), BF16)],
    )
    return pl.pallas_call(
        functools.partial(_expert_kernel, blk0=blk0),
        grid_spec=grid_spec,
        out_shape=jax.ShapeDtypeStruct((n_blocks_total * MOE_BLOCK, d), F32),
        input_output_aliases=aliases,
        compiler_params=_cparams("arbitrary"),
        name="moe_experts",
    )(*args)


def _hier_moe(rows, z, norm_g, modtab, w_grp, b_grp, w_exp, b_exp, w_gate, w_up, w_down, layer, *, n_rows):
    d = z.shape[1]
    n_logit = N_GROUPS + N_EXPERTS
    w_route = jnp.concatenate([w_grp, w_exp, jnp.zeros((d, 128 - n_logit), F32)], axis=1)
    b_route = jnp.concatenate([b_grp, b_exp, jnp.zeros((128 - n_logit,), F32)]).reshape(1, 128)
    h2, route = _route(rows, z, norm_g, modtab, w_route, b_route, n_rows=n_rows)
    rank, totals = _rank(route, tm=rows.tile_rows(512))
    eid = route[:, :TOP_K].astype(jnp.int32)
    wts = route[:, TOP_K:2 * TOP_K]
    counts = totals[0, :N_EXPERTS].astype(jnp.int32)
    padded = (counts + MOE_BLOCK - 1) // MOE_BLOCK * MOE_BLOCK
    pad_end = jnp.cumsum(padded)
    pad_start = pad_end - padded
    dest = pad_start[eid] + rank[:, :TOP_K].astype(jnp.int32)
    n_assign = n_rows * TOP_K
    n_blocks = (n_assign + N_EXPERTS * (MOE_BLOCK - 1) + MOE_BLOCK - 1) // MOE_BLOCK
    n_pad_rows = n_blocks * MOE_BLOCK
    tok = jnp.broadcast_to(jnp.arange(n_rows, dtype=jnp.int32)[:, None], (n_rows, TOP_K))
    row_tok = jnp.full((n_pad_rows,), n_rows, jnp.int32).at[dest.reshape(-1)].set(tok.reshape(-1))
    blk_start = jnp.arange(n_blocks, dtype=jnp.int32) * MOE_BLOCK
    blk_expert = jnp.minimum(jnp.sum((pad_end[None, :] <= blk_start[:, None]).astype(jnp.int32), axis=1),
                             N_EXPERTS - 1)
    first = jnp.concatenate([jnp.ones((1,), jnp.int32), (blk_expert[1:] != blk_expert[:-1]).astype(jnp.int32)])
    n_used = (pad_end[-1] // MOE_BLOCK).astype(jnp.int32).reshape(1)
    h2_pad = jnp.concatenate([h2, jnp.zeros((1, d), h2.dtype)], axis=0)
    n_ranges = max(g for g in (4, 2, 1) if n_blocks % g == 0)
    per = n_blocks // n_ranges
    ys = None
    for g in range(n_ranges):
        xs = h2_pad[row_tok[g * per * MOE_BLOCK:(g + 1) * per * MOE_BLOCK]]
        ys = _expert_blocks(xs, ys, blk_expert, first, n_used, w_gate, w_up, w_down, layer,
                            blk0=g * per, n_blocks_total=n_blocks)
    return ys[dest[:, 0]] * wts[:, 0:1] + ys[dest[:, 1]] * wts[:, 1:2]


def kernel(x, c, ctx, c_ctx, mod_w, mod_b, norm1_g, norm2_g, w_in, dn_conv, dn_a_log, dn_dt_bias, dn_norm_g, dn_out, lru_conv, lru_conv_b, lru_wa, lru_ba, lru_wx, lru_bx, lru_lambda, lru_out, att_qn_g, att_kn_g, att_sink, att_out, w_o, moe_w_grp, moe_b_grp, moe_w_exp, moe_b_exp, moe_w_gate, moe_w_up, moe_w_down):
    batch, seq, d = x.shape
    n_ctx = ctx.shape[1]
    depth = mod_w.shape[0]
    rows = _Rows(batch, seq, n_ctx)
    assert d == D_MODEL and batch + 1 <= 8

    z = jnp.concatenate([x.reshape(batch * seq, d), ctx.reshape(batch * n_ctx, d)], axis=0)

    cc = jnp.concatenate([c_ctx[None], c, jnp.zeros((7 - batch, d), F32)], axis=0)
    mods = _mod_tables(cc, mod_w, mod_b).reshape(depth, 8, 6, d)
    mods = jnp.concatenate([mods, jnp.zeros((depth, 8, 2, d), F32)], axis=2)

    cos, sin = _rope_tables(seq, n_ctx)
    pc = _ProjCols
    w_proj = _relayout_w_in(w_in)

    for l in range(depth):
        last = l == depth - 1
        modtab = mods[l]
        n_out = rows.n_lat_rows if last else rows.n_rows

        p = _norm_matmul(rows, z, norm1_g[l], modtab, w_proj, layer=l, shift_row=0, scale_row=1, tn=pc.tile,
                         name="in_proj")

        qkv = _dn_prep(rows, p, dn_conv[l])
        par = jnp.zeros((2, 8, 128), F32)
        par = par.at[:, 0, :N_HEADS].set(dn_a_log[l]).at[:, 1, :N_HEADS].set(dn_dt_bias[l])
        o_f = _dn_scan(rows, qkv, p, par, small_col=pc.scalars, reverse=False)
        o_b = _dn_scan(rows, qkv, p, par, small_col=pc.scalars, reverse=True)

        lru_args = (lru_conv[l], lru_conv_b[l], lru_wa[l], lru_wx[l], lru_ba[l], lru_bx[l], lru_lambda[l])
        hs_f = _lru_scan(rows, p, *lru_args, x_col=pc.lx, reverse=False)
        hs_b = _lru_scan(rows, p, *lru_args, x_col=pc.lx, reverse=True)

        qn, kn = _qk_prep(rows, p, cos, sin, att_qn_g[l], att_kn_g[l], pc.aq, pc.ak)
        sink_b = jnp.broadcast_to(att_sink[l][:, None], (N_HEADS, 128))
        y_c = _attention(rows, qn, kn, p, pc.av, sink_b, with_ctx_queries=not last)

        merged = _merge(rows, o_f, o_b, p, hs_f, hs_b, p, y_c, dn_out[l].astype(BF16), lru_out[l].astype(BF16),
                        att_out[l].astype(BF16), dn_norm_g[l], z_col=pc.z, lg_col=pc.lg, gate_col=pc.gates,
                        n_rows=n_out)
        z = _mm_residual(rows, merged, w_o[l].astype(BF16), z, modtab, gate_row=2, n_rows=n_out)

        f = _hier_moe(rows, z, norm2_g[l], modtab, moe_w_grp[l], moe_b_grp[l], moe_w_exp[l], moe_b_exp[l],
                      moe_w_gate, moe_w_up, moe_w_down, l, n_rows=n_out)
        gate2 = jnp.concatenate([jnp.repeat(modtab[1:1 + batch, 5], seq, axis=0),
                                 jnp.broadcast_to(modtab[0, 5], (batch * n_ctx, d))], axis=0)[:n_out]
        z = z + gate2 * f

    return z[:rows.n_lat_rows].reshape(batch, seq, d)
```

```python
import functools

import jax
import jax.numpy as jnp
from jax import lax
from jax.experimental import pallas as pl
from jax.experimental.pallas import tpu as pltpu
from jax.experimental.pallas import tpu_sc as plsc

F32 = jnp.float32
BF16 = jnp.bfloat16

EPS = 1e-6
D_MODEL = 2048
N_HEADS = 8
HEAD_DIM = 128
MIX_W = N_HEADS * HEAD_DIM
KV_HEADS = 2
KV_W = KV_HEADS * HEAD_DIM
DN_CHUNK = 64
DN_UNIT = 128
CONV_W = 4
LRU_C = 8.0
ATT_WINDOW = 128
GRID_W = 64
ROPE_THETA = 10000.0
N_GROUPS = 8
EXPERTS_PER_GROUP = 8
N_EXPERTS = N_GROUPS * EXPERTS_PER_GROUP
TOP_K = 2
D_EXPERT = 512
MOE_BLOCK = 128
ROW_BLK = 256
NEG_BIG = -1e30
VMEM_LIMIT = 56 * 1024 * 1024


def _cparams(*sem):
    return pltpu.CompilerParams(dimension_semantics=sem, vmem_limit_bytes=VMEM_LIMIT)


def _bdot(a, b):
    return jnp.dot(a.astype(BF16), b.astype(BF16), preferred_element_type=F32)


def _bdot_nt(a, b):
    return lax.dot_general(a.astype(BF16), b.astype(BF16), (((1,), (1,)), ((), ())),
                           preferred_element_type=F32)


def _dot01(m01, x):
    m = m01.astype(BF16)
    x0 = x.astype(BF16)
    r1 = x - x0.astype(F32)
    x1 = r1.astype(BF16)
    x2 = (r1 - x1.astype(F32)).astype(BF16)
    dot = lambda t: jnp.dot(m, t, preferred_element_type=F32)
    return dot(x0) + dot(x1) + dot(x2)


def _silu(x):
    return x * jax.nn.sigmoid(x)


def _softplus(x):
    return jnp.maximum(x, 0.0) + jnp.log1p(jnp.exp(-jnp.abs(x)))


class _Rows:
    def __init__(self, batch, seq, n_ctx):
        assert seq % ROW_BLK == 0 and n_ctx % ROW_BLK == 0
        self.batch, self.seq, self.n_ctx = batch, seq, n_ctx
        self.nlat = seq // ROW_BLK
        self.nctx = n_ctx // ROW_BLK
        self.n_lat_rows = batch * seq
        self.n_rows = batch * (seq + n_ctx)
        self.steps = self.nlat + self.nctx

    def seq_block(self, b, t, reverse):
        if reverse:
            jc, jl = self.nctx - 1 - t, self.nlat - 1 - (t - self.nctx)
        else:
            jc, jl = t, t - self.nctx
        return jnp.where(t < self.nctx, self.batch * self.nlat + b * self.nctx + jc, b * self.nlat + jl)

    def mod_index(self, tile, tm):
        r0 = tile * tm
        return jnp.where(r0 < self.n_lat_rows, 1 + r0 // self.seq, 0)

    def tile_rows(self, cap):
        tm = cap
        while self.seq % tm or (self.batch * self.n_ctx) % tm:
            tm //= 2
        return tm


def _modtab_kernel(c_ref, w_ref, b_ref, o_ref):
    o_ref[0] = _bdot(_silu(c_ref[...]), w_ref[0]) + b_ref[0]


def _mod_tables(cc, mod_w, mod_b):
    depth, d, n = mod_w.shape
    tn = 1024
    return pl.pallas_call(
        _modtab_kernel,
        grid=(depth, n // tn),
        in_specs=[pl.BlockSpec((8, d), lambda l, j: (0, 0)),
                  pl.BlockSpec((1, d, tn), lambda l, j: (l, 0, j)),
                  pl.BlockSpec((1, 1, tn), lambda l, j: (l, 0, j))],
        out_specs=pl.BlockSpec((1, 8, tn), lambda l, j: (l, 0, j)),
        out_shape=jax.ShapeDtypeStruct((depth, 8, n), F32),
        compiler_params=_cparams("parallel", "parallel"),
        name="mod_tables",
    )(cc, mod_w, mod_b.reshape(depth, 1, n))


class _ProjCols:
    tile = 512
    small = 4 * N_HEADS
    qkvz = 4 * MIX_W
    rest = 3 * MIX_W + 2 * KV_W + 3 * D_MODEL
    z, lx, lg, aq = 3 * MIX_W, 4 * MIX_W, 5 * MIX_W, 6 * MIX_W
    ak = aq + MIX_W
    av = ak + KV_W
    gates = av + KV_W
    scalars = qkvz + rest
    total = scalars + 4 * 128


def _relayout_kernel(a_ref, b_ref, s_ref, o_ref, *, n_aligned, n_tiles, shift):
    j = pl.program_id(1)

    @pl.when(j < n_aligned)
    def _():
        o_ref[0] = a_ref[0].astype(BF16)

    @pl.when((j >= n_aligned) & (j < n_tiles - 1))
    def _():
        o_ref[0] = jnp.concatenate([a_ref[0][:, shift:], b_ref[0][:, :shift]], axis=1).astype(BF16)

    @pl.when(j == n_tiles - 1)
    def _():
        o_ref[0] = s_ref[0].astype(BF16)


def _relayout_w_in(w_in):
    depth, d, n_in = w_in.shape
    pc = _ProjCols
    assert n_in == pc.qkvz + pc.small + pc.rest and pc.qkvz % pc.tile == 0 and pc.rest % pc.tile == 0
    n_aligned = pc.qkvz // pc.tile
    n_tiles = pc.total // pc.tile
    per = pc.tile // 128
    w_small = w_in[:, :, pc.qkvz:pc.qkvz + pc.small].reshape(depth, d, 4, N_HEADS)
    w_small = jnp.pad(w_small, ((0, 0), (0, 0), (0, 0), (0, 128 - N_HEADS))).reshape(depth, d, 4 * 128)
    return pl.pallas_call(
        functools.partial(_relayout_kernel, n_aligned=n_aligned, n_tiles=n_tiles, shift=pc.small),
        grid=(depth, n_tiles),
        in_specs=[pl.BlockSpec((1, d, pc.tile), lambda l, j: (l, 0, jnp.minimum(j, n_tiles - 2))),
                  pl.BlockSpec((1, d, 128), lambda l, j: (l, 0, jnp.minimum(j + 1, n_tiles - 1) * per)),
                  pl.BlockSpec((1, d, pc.tile), lambda l, j: (l, 0, 0))],
        out_specs=pl.BlockSpec((1, d, pc.tile), lambda l, j: (l, 0, j)),
        out_shape=jax.ShapeDtypeStruct((depth, d, pc.total), BF16),
        compiler_params=_cparams("parallel", "parallel"),
        name="relayout_w_in",
    )(w_in, w_in, w_small)


def _norm_mm_kernel(z_ref, g_ref, mod_ref, w_ref, o_ref, *rest, shift_row, scale_row, emit_h):
    h_scr = rest[-1]
    tm = z_ref.shape[0]
    sub = min(tm, 128)

    @pl.when(pl.program_id(1) == 0)
    def _():
        g = g_ref[...]
        m = mod_ref[0]
        scale1 = 1.0 + m[scale_row:scale_row + 1]
        shift = m[shift_row:shift_row + 1]

        def body(i, carry):
            r0 = pl.multiple_of(i * sub, sub)
            z = z_ref[pl.ds(r0, sub), :]
            y = z * lax.rsqrt(jnp.mean(z * z, axis=-1, keepdims=True) + EPS) * g
            h_scr[pl.ds(r0, sub), :] = (y * scale1 + shift).astype(BF16)
            return carry

        lax.fori_loop(0, tm // sub, body, 0)

    h = h_scr[...]
    o_ref[...] = jnp.dot(h, w_ref[...].astype(BF16), preferred_element_type=F32)
    if emit_h:
        @pl.when(pl.program_id(1) == 0)
        def _():
            rest[0][...] = h


def _norm_matmul(rows, z, g, modtab, w, *, shift_row, scale_row, tn, layer=None, tm_cap=1024,
                 n_rows=None, emit_h=False, name="norm_matmul"):
    n_rows = rows.n_rows if n_rows is None else n_rows
    d = z.shape[1]
    n_cols = w.shape[-1]
    tm = rows.tile_rows(tm_cap)
    assert n_rows % tm == 0 and n_cols % tn == 0
    if layer is None:
        w_spec = pl.BlockSpec((d, tn), lambda i, j: (0, j))
    else:
        w_spec = pl.BlockSpec((None, d, tn), lambda i, j: (layer, 0, j))
    out_shape = [jax.ShapeDtypeStruct((n_rows, n_cols), F32)]
    out_specs = [pl.BlockSpec((tm, tn), lambda i, j: (i, j))]
    if emit_h:
        out_shape.append(jax.ShapeDtypeStruct((n_rows, d), BF16))
        out_specs.append(pl.BlockSpec((tm, d), lambda i, j: (i, 0)))
    res = pl.pallas_call(
        functools.partial(_norm_mm_kernel, shift_row=shift_row, scale_row=scale_row, emit_h=emit_h),
        grid=(n_rows // tm, n_cols // tn),
        in_specs=[pl.BlockSpec((tm, d), lambda i, j: (i, 0)),
                  pl.BlockSpec((1, d), lambda i, j: (0, 0)),
                  pl.BlockSpec((1, 8, d), lambda i, j: (rows.mod_index(i, tm), 0, 0)),
                  w_spec],
        out_specs=out_specs,
        out_shape=out_shape,
        scratch_shapes=[pltpu.VMEM((tm, d), BF16)],
        compiler_params=_cparams("parallel", "arbitrary"),
        name=name,
    )(z, g.reshape(1, d), modtab, w)
    return res if emit_h else res[0]


def _halo_flags(rows, blk):
    is_lat = blk < rows.batch * rows.nlat
    j = jnp.where(is_lat, blk % rows.nlat, (blk - rows.batch * rows.nlat) % rows.nctx)
    nseg = jnp.where(is_lat, rows.nlat, rows.nctx)
    return (j != 0).astype(F32), (j != nseg - 1).astype(F32)


def _conv4(prev8, cur, next8, w, use_prev, use_next):
    n = cur.shape[0]
    x = jnp.concatenate([prev8 * use_prev, cur, next8 * use_next], axis=0)
    tot = n + 16
    y = w[1:2] * cur
    y = y + w[0:1] * pltpu.roll(x, 1, 0)[8:8 + n]
    y = y + w[2:3] * pltpu.roll(x, tot - 1, 0)[8:8 + n]
    y = y + w[3:4] * pltpu.roll(x, tot - 2, 0)[8:8 + n]
    return y


def _halo_specs(rows, width, col_blk, blk_of):
    per = ROW_BLK // 8
    last8 = rows.n_rows // 8 - 1
    return [
        pl.BlockSpec((8, width), lambda *g: (jnp.maximum(blk_of(*g) * per - 1, 0), col_blk(*g))),
        pl.BlockSpec((ROW_BLK, width), lambda *g: (blk_of(*g), col_blk(*g))),
        pl.BlockSpec((8, width), lambda *g: (jnp.minimum((blk_of(*g) + 1) * per, last8), col_blk(*g))),
    ]


def _dn_prep_kernel(prev_ref, cur_ref, next_ref, w_ref, o_ref, *, rows):
    blk = pl.program_id(0)
    kind = pl.program_id(1)
    use_prev, use_next = _halo_flags(rows, blk)
    y = _silu(_conv4(prev_ref[...], cur_ref[...], next_ref[...], w_ref[...], use_prev, use_next))
    q_scale = jnp.where(kind == 0, HEAD_DIM ** -0.5, 1.0)
    for h in range(N_HEADS):
        sl = slice(h * HEAD_DIM, (h + 1) * HEAD_DIM)
        yh = y[:, sl]
        inv = lax.rsqrt(jnp.sum(yh * yh, axis=-1, keepdims=True) + EPS)
        o_ref[:, sl] = yh * (jnp.where(kind == 2, 1.0, inv) * q_scale)


def _dn_prep(rows, p_a, conv_w):
    return pl.pallas_call(
        functools.partial(_dn_prep_kernel, rows=rows),
        grid=(rows.n_rows // ROW_BLK, 3),
        in_specs=_halo_specs(rows, MIX_W, lambda i, j: j, lambda i, j: i)
        + [pl.BlockSpec((CONV_W, MIX_W), lambda i, j: (0, j))],
        out_specs=pl.BlockSpec((ROW_BLK, MIX_W), lambda i, j: (i, j)),
        out_shape=jax.ShapeDtypeStruct((rows.n_rows, 3 * MIX_W), F32),
        compiler_params=_cparams("parallel", "parallel"),
        name="dn_prep",
    )(p_a, p_a, p_a, conv_w)


def _dot_b(a, b):
    return jnp.dot(a, b, preferred_element_type=F32).astype(BF16)


def _dn_scan_kernel(q_ref, k_ref, v_ref, beta_ref, alpha_ref, par_ref, o_ref,
                    s_scr, u_scr, wq_scr, kdt_scr, aqk_scr, *, reverse):
    n = ROW_BLK
    nchunk = n // DN_CHUNK
    per_unit = DN_UNIT // DN_CHUNK

    @pl.when(pl.program_id(1) == 0)
    def _():
        s_scr[...] = jnp.zeros_like(s_scr)

    def tri(m):
        row = lax.broadcasted_iota(jnp.int32, (m, m), 0)
        col = lax.broadcasted_iota(jnp.int32, (m, m), 1)
        same = lambda s: (row >> (s.bit_length() - 1)) == (col >> (s.bit_length() - 1))
        ahead = (col >= row) if reverse else (col <= row)
        return row, col, same, same(DN_CHUNK) & ahead, same(DN_CHUNK) & ahead & (row != col)

    _, _, same_n, incl_n, _ = tri(n)
    par = par_ref[0]
    beta_all = jax.nn.sigmoid(beta_ref[...])
    g_all = -jnp.exp(par[0:1]) * _softplus(alpha_ref[...] + par[1:2])
    gc_all = _dot01(incl_n.astype(F32), g_all)
    gt_all = _dot01(same_n(DN_CHUNK).astype(F32), g_all)
    gc_t = gc_all.T
    egc_all = jnp.exp(gc_all)
    ekd_all = jnp.exp(gt_all - gc_all)
    egt_all = jnp.exp(gt_all)

    row_u, col_u, same_u, incl_u, strict_u = tri(DN_UNIT)
    one_b = lambda m: jnp.where(m, 1.0, 0.0).astype(BF16)
    eye_b = one_b(row_u == col_u)
    diag8_b = one_b(same_u(8))
    off_b = {s: one_b(same_u(2 * s) & jnp.logical_not(same_u(s))) for s in (8, 16, 32)}

    units = [(h, j) for h in range(N_HEADS) for j in range(n // DN_UNIT)]
    a_b, rhs_b = [], []
    for h, j in units:
        rs = slice(j * DN_UNIT, (j + 1) * DN_UNIT)
        sl = slice(h * HEAD_DIM, (h + 1) * HEAD_DIM)
        q, k, v = q_ref[rs, sl], k_ref[rs, sl], v_ref[rs, sl]
        bcol = beta_all[rs, h:h + 1]
        egc = egc_all[rs, h:h + 1]
        dec = jnp.exp(jnp.where(incl_u, gc_all[rs, h:h + 1] - gc_t[h:h + 1, rs], NEG_BIG))
        kb = k * bcol
        k_b = k.astype(BF16)
        a_b.append(jnp.where(strict_u, _bdot_nt(kb, k_b) * dec, 0.0).astype(BF16))
        aqk_scr[h, rs, :] = (_bdot_nt(q, k_b) * dec).astype(BF16)
        rhs_b.append(jnp.concatenate([v * bcol, kb * egc], axis=1).astype(BF16))
        qd = (q * egc).astype(BF16)
        kd = k * ekd_all[rs, h:h + 1]
        for c in range(per_unit):
            cs = slice(c * DN_CHUNK, (c + 1) * DN_CHUNK)
            wq_scr[h, j * per_unit + c, DN_CHUNK:, :] = qd[cs]
            kdt_scr[h, j * per_unit + c] = kd[cs].T.astype(BF16)

    d1 = [a * diag8_b for a in a_b]
    d2 = [_dot_b(d, d) for d in d1]
    d4 = [_dot_b(d, d) for d in d2]
    x = [_dot_b(eye_b - d, eye_b + e) for d, e in zip(d1, d2)]
    x = [_dot_b(xx, eye_b + e) for xx, e in zip(x, d4)]
    for s in (8, 16, 32):
        t = [_dot_b(a * off_b[s], xx) for a, xx in zip(a_b, x)]
        x = [xx - _dot_b(xx, tt) for xx, tt in zip(x, t)]
    for (h, j), xx, rhs in zip(units, x, rhs_b):
        uw = jnp.dot(xx, rhs, preferred_element_type=F32)
        u_scr[h, j * DN_UNIT:(j + 1) * DN_UNIT, :] = uw[:, :HEAD_DIM]
        for c in range(per_unit):
            wq_scr[h, j * per_unit + c, :DN_CHUNK, :] = uw[c * DN_CHUNK:(c + 1) * DN_CHUNK, HEAD_DIM:].astype(BF16)

    zeros_b = jnp.zeros((DN_CHUNK, HEAD_DIM), BF16)
    for ci in range(nchunk):
        c = nchunk - 1 - ci if reverse else ci
        rs = slice(c * DN_CHUNK, (c + 1) * DN_CHUNK)
        for h in range(N_HEADS):
            s_h = s_scr[h]
            r = jnp.dot(wq_scr[h, c], s_h.astype(BF16), preferred_element_type=F32)
            v_new = (u_scr[h, rs, :] - r[:DN_CHUNK]).astype(BF16)
            v_unit = jnp.concatenate([v_new, zeros_b] if c % per_unit == 0 else [zeros_b, v_new], axis=0)
            o = r[DN_CHUNK:] + jnp.dot(aqk_scr[h, rs, :], v_unit, preferred_element_type=F32)
            o_ref[rs, h * HEAD_DIM:(h + 1) * HEAD_DIM] = o
            s_scr[h] = s_h * egt_all[c * DN_CHUNK:c * DN_CHUNK + 1, h:h + 1] + jnp.dot(
                kdt_scr[h, c], v_new, preferred_element_type=F32)


def _dn_scan(rows, qkv, p, par, *, small_col, reverse):
    d = 1 if reverse else 0
    blk = lambda b, t: rows.seq_block(b, t, reverse)
    nchunk = ROW_BLK // DN_CHUNK
    sc = small_col // 128
    return pl.pallas_call(
        functools.partial(_dn_scan_kernel, reverse=reverse),
        grid=(rows.batch, rows.steps),
        in_specs=[pl.BlockSpec((ROW_BLK, MIX_W), lambda b, t: (blk(b, t), 0)),
                  pl.BlockSpec((ROW_BLK, MIX_W), lambda b, t: (blk(b, t), 1)),
                  pl.BlockSpec((ROW_BLK, MIX_W), lambda b, t: (blk(b, t), 2)),
                  pl.BlockSpec((ROW_BLK, 128), lambda b, t: (blk(b, t), sc + d)),
                  pl.BlockSpec((ROW_BLK, 128), lambda b, t: (blk(b, t), sc + 2 + d)),
                  pl.BlockSpec((1, 8, 128), lambda b, t: (d, 0, 0))],
        out_specs=pl.BlockSpec((ROW_BLK, MIX_W), lambda b, t: (blk(b, t), 0)),
        out_shape=jax.ShapeDtypeStruct((rows.n_rows, MIX_W), F32),
        scratch_shapes=[pltpu.VMEM((N_HEADS, HEAD_DIM, HEAD_DIM), F32),
                        pltpu.VMEM((N_HEADS, ROW_BLK, HEAD_DIM), F32),
                        pltpu.VMEM((N_HEADS, nchunk, 2 * DN_CHUNK, HEAD_DIM), BF16),
                        pltpu.VMEM((N_HEADS, nchunk, HEAD_DIM, DN_CHUNK), BF16),
                        pltpu.VMEM((N_HEADS, ROW_BLK, DN_UNIT), BF16)],
        compiler_params=_cparams("parallel", "arbitrary"),
        name="dn_scan_bwd" if reverse else "dn_scan_fwd",
    )(qkv, qkv, qkv, p, p, par)


def _lru_kernel(prev_ref, cur_ref, next_ref, cw_ref, cb_ref, wa_ref, wx_ref, ba_ref, bx_ref, lam_ref,
                o_ref, h_scr, a_scr, u_scr, *, rows, reverse):
    b, t = pl.program_id(0), pl.program_id(1)

    @pl.when(t == 0)
    def _():
        h_scr[...] = jnp.zeros_like(h_scr)

    use_prev, use_next = _halo_flags(rows, rows.seq_block(b, t, reverse))
    xc = _conv4(prev_ref[...], cur_ref[...], next_ref[...], cw_ref[...], use_prev, use_next) + cb_ref[...]
    for kb in range(N_HEADS):
        sl = slice(kb * HEAD_DIM, (kb + 1) * HEAD_DIM)
        xb = xc[:, sl]
        r = jax.nn.sigmoid(_bdot(xb, wa_ref[0, kb]) + ba_ref[0, :, sl])
        i = jax.nn.sigmoid(_bdot(xb, wx_ref[0, kb]) + bx_ref[0, :, sl])
        log_a = -LRU_C * r * _softplus(-lam_ref[0, :, sl])
        a_scr[:, sl] = jnp.exp(log_a)
        u_scr[:, sl] = jnp.sqrt(1.0 - jnp.exp(2.0 * log_a)) * (i * xb)

    ngrp = ROW_BLK // 8
    sub = lax.broadcasted_iota(jnp.int32, (8, MIX_W), 0)

    def group_step(gi, h_prev):
        g = ngrp - 1 - gi if reverse else gi
        r0 = pl.multiple_of(g * 8, 8)
        a = a_scr[pl.ds(r0, 8), :]
        u = u_scr[pl.ds(r0, 8), :]
        for dist in (1, 2, 4):
            if reverse:
                a_sh, u_sh, m = pltpu.roll(a, 8 - dist, 0), pltpu.roll(u, 8 - dist, 0), sub < 8 - dist
            else:
                a_sh, u_sh, m = pltpu.roll(a, dist, 0), pltpu.roll(u, dist, 0), sub >= dist
            u = jnp.where(m, a * u_sh + u, u)
            a = jnp.where(m, a * a_sh, a)
        h = u + a * h_prev
        o_ref[pl.ds(r0, 8), :] = h
        return h[0:1] if reverse else h[7:8]

    h_scr[...] = lax.fori_loop(0, ngrp, group_step, h_scr[...])


def _lru_scan(rows, p_b, cw, cb, wa, wx, ba, bx, lam, *, x_col, reverse):
    d = 1 if reverse else 0
    blk = lambda b, t: rows.seq_block(b, t, reverse)
    vec = lambda: pl.BlockSpec((1, 1, MIX_W), lambda b, t: (d, 0, 0))
    mat = lambda: pl.BlockSpec((1, N_HEADS, HEAD_DIM, HEAD_DIM), lambda b, t: (d, 0, 0, 0))
    return pl.pallas_call(
        functools.partial(_lru_kernel, rows=rows, reverse=reverse),
        grid=(rows.batch, rows.steps),
        in_specs=_halo_specs(rows, MIX_W, lambda b, t: x_col // MIX_W, blk)
        + [pl.BlockSpec((CONV_W, MIX_W), lambda b, t: (0, 0)),
           pl.BlockSpec((1, MIX_W), lambda b, t: (0, 0)),
           mat(), mat(), vec(), vec(), vec()],
        out_specs=pl.BlockSpec((ROW_BLK, MIX_W), lambda b, t: (blk(b, t), 0)),
        out_shape=jax.ShapeDtypeStruct((rows.n_rows, MIX_W), F32),
        scratch_shapes=[pltpu.VMEM((1, MIX_W), F32),
                        pltpu.VMEM((ROW_BLK, MIX_W), F32),
                        pltpu.VMEM((ROW_BLK, MIX_W), F32)],
        compiler_params=_cparams("parallel", "arbitrary"),
        name="lru_bwd" if reverse else "lru_fwd",
    )(p_b, p_b, p_b, cw, cb.reshape(1, MIX_W), wa, wx,
      ba.reshape(2, 1, MIX_W), bx.reshape(2, 1, MIX_W), lam.reshape(2, 1, MIX_W))


def _rope_tables(seq, n_ctx):
    half = HEAD_DIM // 2
    pos = jnp.arange(seq)
    inv = ROPE_THETA ** (-jnp.arange(0, half, 2, dtype=F32) / half)
    ang_r = (pos // GRID_W).astype(F32)[:, None] * inv
    ang_c = (pos % GRID_W).astype(F32)[:, None] * inv
    cos = jnp.concatenate([jnp.cos(ang_r)] * 2 + [jnp.cos(ang_c)] * 2, axis=-1)
    sin = jnp.concatenate([-jnp.sin(ang_r), jnp.sin(ang_r), -jnp.sin(ang_c), jnp.sin(ang_c)], axis=-1)
    cos = jnp.concatenate([cos, jnp.ones((n_ctx, HEAD_DIM), F32)], axis=0)
    sin = jnp.concatenate([sin, jnp.zeros((n_ctx, HEAD_DIM), F32)], axis=0)
    return cos, sin


def _qk_prep_kernel(q_ref, k_ref, cos_ref, sin_ref, qg_ref, kg_ref, qo_ref, ko_ref):
    cos, sin = cos_ref[...], sin_ref[...]
    lane = lax.broadcasted_iota(jnp.int32, cos.shape, 1)
    first = (lane & (HEAD_DIM // 2 - 1)) < (HEAD_DIM // 4)

    def norm_rope(x, g):
        y = x * lax.rsqrt(jnp.mean(x * x, axis=-1, keepdims=True) + EPS) * g
        partner = jnp.where(first, pltpu.roll(y, HEAD_DIM - HEAD_DIM // 4, 1), pltpu.roll(y, HEAD_DIM // 4, 1))
        return y * cos + partner * sin

    for h in range(N_HEADS):
        sl = slice(h * HEAD_DIM, (h + 1) * HEAD_DIM)
        qo_ref[:, sl] = (norm_rope(q_ref[:, sl], qg_ref[...]) * (HEAD_DIM ** -0.5)).astype(BF16)
    for h in range(KV_HEADS):
        sl = slice(h * HEAD_DIM, (h + 1) * HEAD_DIM)
        ko_ref[:, sl] = norm_rope(k_ref[:, sl], kg_ref[...]).astype(BF16)


def _qk_prep(rows, p_b, cos, sin, qn_g, kn_g, q_col, k_col):
    def tab(i):
        return jnp.where(i < rows.batch * rows.nlat, i % rows.nlat,
                         rows.nlat + (i - rows.batch * rows.nlat) % rows.nctx)

    return pl.pallas_call(
        _qk_prep_kernel,
        grid=(rows.n_rows // ROW_BLK,),
        in_specs=[pl.BlockSpec((ROW_BLK, MIX_W), lambda i: (i, q_col // MIX_W)),
                  pl.BlockSpec((ROW_BLK, KV_W), lambda i: (i, k_col // KV_W)),
                  pl.BlockSpec((ROW_BLK, HEAD_DIM), lambda i: (tab(i), 0)),
                  pl.BlockSpec((ROW_BLK, HEAD_DIM), lambda i: (tab(i), 0)),
                  pl.BlockSpec((1, HEAD_DIM), lambda i: (0, 0)),
                  pl.BlockSpec((1, HEAD_DIM), lambda i: (0, 0))],
        out_specs=[pl.BlockSpec((ROW_BLK, MIX_W), lambda i: (i, 0)),
                   pl.BlockSpec((ROW_BLK, KV_W), lambda i: (i, 0))],
        out_shape=[jax.ShapeDtypeStruct((rows.n_rows, MIX_W), BF16),
                   jax.ShapeDtypeStruct((rows.n_rows, KV_W), BF16)],
        compiler_params=_cparams("parallel"),
        name="qk_prep",
    )(p_b, p_b, cos, sin, qn_g.reshape(1, HEAD_DIM), kn_g.reshape(1, HEAD_DIM))


def _attn_core(q, keys, vals, sink_ref, valid, o_ref):
    grp = N_HEADS // KV_HEADS
    nq = q.shape[0]
    for kvh in range(KV_HEADS):
        sl = slice(kvh * HEAD_DIM, (kvh + 1) * HEAD_DIM)
        kk = jnp.concatenate([t[:, sl] for t in keys], axis=0)
        vv = jnp.concatenate([t[:, sl].astype(BF16) for t in vals], axis=0)
        q4 = jnp.concatenate([q[:, (kvh * grp + g) * HEAD_DIM:(kvh * grp + g + 1) * HEAD_DIM]
                              for g in range(grp)], axis=0)
        s = _bdot_nt(q4, kk)
        if valid is not None:
            s = jnp.where(valid, s, NEG_BIG)
        sink = jnp.concatenate([jnp.broadcast_to(sink_ref[kvh * grp + g:kvh * grp + g + 1, 0:1], (nq, 1))
                                for g in range(grp)], axis=0)
        m = jnp.maximum(jnp.max(s, axis=-1, keepdims=True), sink)
        e = jnp.exp(s - m)
        p = e / (jnp.sum(e, axis=-1, keepdims=True) + jnp.exp(sink - m))
        o = _bdot(p, vv)
        for g in range(grp):
            hq = kvh * grp + g
            o_ref[:, hq * HEAD_DIM:(hq + 1) * HEAD_DIM] = o[g * nq:(g + 1) * nq].astype(BF16)


def _attn_kernel(q_ref, k0_ref, k1_ref, k2_ref, kc_ref, v0_ref, v1_ref, v2_ref, vc_ref, sink_ref, o_ref,
                 *, seq, nb):
    n = pl.program_id(1)

    @pl.when(n < nb)
    def _():
        blk = q_ref.shape[0]
        n_loc = 3 * blk
        n_ctx = kc_ref.shape[0]
        grp = N_HEADS // KV_HEADS
        qi = lax.broadcasted_iota(jnp.int32, (grp * blk, n_loc + n_ctx), 0) & (blk - 1)
        kj = lax.broadcasted_iota(jnp.int32, (grp * blk, n_loc + n_ctx), 1)
        rel = kj - blk
        kpos = n * blk + rel
        local_ok = (jnp.abs(qi - rel) <= ATT_WINDOW) & (kpos >= 0) & (kpos < seq)
        valid = local_ok | (kj >= n_loc)
        _attn_core(q_ref[...], [k0_ref[...], k1_ref[...], k2_ref[...], kc_ref[...]],
                   [v0_ref[...], v1_ref[...], v2_ref[...], vc_ref[...]], sink_ref, valid, o_ref)

    @pl.when(n >= nb)
    def _():
        _attn_core(q_ref[...], [kc_ref[...]], [vc_ref[...]], sink_ref, None, o_ref)


def _attention(rows, qn, kn, p_b, v_col, sink_b, *, with_ctx_queries):
    blk = 128
    nb = rows.seq // blk
    vcb = v_col // KV_W
    ctx_blk0 = rows.n_lat_rows // rows.n_ctx
    assert rows.n_lat_rows % rows.n_ctx == 0
    kspec = lambda off: pl.BlockSpec((blk, KV_W), lambda b, n: (b * nb + jnp.clip(n + off, 0, nb - 1), 0))
    vspec = lambda off: pl.BlockSpec((blk, KV_W), lambda b, n: (b * nb + jnp.clip(n + off, 0, nb - 1), vcb))
    n_out = rows.n_rows if with_ctx_queries else rows.n_lat_rows
    ncb = rows.n_ctx // blk if with_ctx_queries else 0
    q0 = rows.n_lat_rows // blk
    qrow = lambda b, n: jnp.where(n < nb, b * nb + n, q0 + b * ncb + (n - nb))
    return pl.pallas_call(
        functools.partial(_attn_kernel, seq=rows.seq, nb=nb),
        grid=(rows.batch, nb + ncb),
        in_specs=[pl.BlockSpec((blk, MIX_W), lambda b, n: (qrow(b, n), 0)),
                  kspec(-1), kspec(0), kspec(1),
                  pl.BlockSpec((rows.n_ctx, KV_W), lambda b, n: (ctx_blk0 + b, 0)),
                  vspec(-1), vspec(0), vspec(1),
                  pl.BlockSpec((rows.n_ctx, KV_W), lambda b, n: (ctx_blk0 + b, vcb)),
                  pl.BlockSpec((8, 128), lambda b, n: (0, 0))],
        out_specs=pl.BlockSpec((blk, MIX_W), lambda b, n: (qrow(b, n), 0)),
        out_shape=jax.ShapeDtypeStruct((n_out, MIX_W), BF16),
        compiler_params=_cparams("parallel", "parallel"),
        name="attention",
    )(qn, kn, kn, kn, kn, p_b, p_b, p_b, p_b, sink_b)


def _merge_kernel(of_ref, ob_ref, z_ref, hf_ref, hb_ref, lg_ref, yc_ref, ga_ref, gb_ref, gc_ref,
                  wa_ref, wb_ref, wc_ref, ng_ref, o_ref, ya_scr, yb_scr):
    tm = of_ref.shape[0]
    sub = min(tm, 128)

    @pl.when(pl.program_id(1) == 0)
    def _():
        ng = ng_ref[...]

        def body(i, carry):
            r0 = pl.multiple_of(i * sub, sub)
            rs = pl.ds(r0, sub)
            for h in range(N_HEADS):
                sl = slice(h * HEAD_DIM, (h + 1) * HEAD_DIM)
                o = of_ref[rs, sl] + ob_ref[rs, sl]
                y = o * lax.rsqrt(jnp.mean(o * o, axis=-1, keepdims=True) + EPS) * ng
                ya_scr[rs, sl] = (y * _silu(z_ref[rs, sl])).astype(BF16)
            yb_scr[rs, :] = (jax.nn.gelu(lg_ref[rs, :]) * (hf_ref[rs, :] + hb_ref[rs, :])).astype(BF16)
            return carry

        lax.fori_loop(0, tm // sub, body, 0)

    acc = jax.nn.sigmoid(ga_ref[...]) * _bdot(ya_scr[...], wa_ref[...])
    acc = acc + jax.nn.sigmoid(gb_ref[...]) * _bdot(yb_scr[...], wb_ref[...])
    acc = acc + jax.nn.sigmoid(gc_ref[...]) * _bdot(yc_ref[...], wc_ref[...])
    o_ref[...] = acc.astype(BF16)


def _merge(rows, o_f, o_b, p_a, hs_f, hs_b, p_b, y_c, dn_out, lru_out, att_out, dn_norm_g, *,
           z_col, lg_col, gate_col, n_rows):
    tm = rows.tile_rows(512)
    tn = 512
    assert n_rows % tm == 0 and gate_col % tn == 0
    row = lambda cb: pl.BlockSpec((tm, MIX_W), lambda i, j: (i, cb))
    gate = lambda br: pl.BlockSpec((tm, tn), lambda i, j: (i, (gate_col + br * D_MODEL) // tn + j))
    wgt = lambda: pl.BlockSpec((MIX_W, tn), lambda i, j: (0, j))
    return pl.pallas_call(
        _merge_kernel,
        grid=(n_rows // tm, D_MODEL // tn),
        in_specs=[row(0), row(0), row(z_col // MIX_W), row(0), row(0), row(lg_col // MIX_W), row(0),
                  gate(0), gate(1), gate(2), wgt(), wgt(), wgt(),
                  pl.BlockSpec((1, HEAD_DIM), lambda i, j: (0, 0))],
        out_specs=pl.BlockSpec((tm, tn), lambda i, j: (i, j)),
        out_shape=jax.ShapeDtypeStruct((n_rows, D_MODEL), BF16),
        scratch_shapes=[pltpu.VMEM((tm, MIX_W), BF16), pltpu.VMEM((tm, MIX_W), BF16)],
        compiler_params=_cparams("parallel", "arbitrary"),
        name="merge",
    )(o_f, o_b, p_a, hs_f, hs_b, p_b, y_c, p_b, p_b, p_b, dn_out, lru_out, att_out,
      dn_norm_g.reshape(1, HEAD_DIM))


def _mm_residual_kernel(a_ref, w_ref, z_ref, mod_ref, o_ref, *, gate_row):
    acc = _bdot(a_ref[...], w_ref[...])
    o_ref[...] = z_ref[...] + mod_ref[0][gate_row:gate_row + 1] * acc


def _mm_residual(rows, a, w, z, modtab, *, gate_row, n_rows):
    tm = rows.tile_rows(1024)
    tn = 512
    k, n = w.shape
    assert n_rows % tm == 0
    return pl.pallas_call(
        functools.partial(_mm_residual_kernel, gate_row=gate_row),
        grid=(n_rows // tm, n // tn),
        in_specs=[pl.BlockSpec((tm, k), lambda i, j: (i, 0)),
                  pl.BlockSpec((k, tn), lambda i, j: (0, j)),
                  pl.BlockSpec((tm, tn), lambda i, j: (i, j)),
                  pl.BlockSpec((1, 8, tn), lambda i, j: (rows.mod_index(i, tm), 0, j))],
        out_specs=pl.BlockSpec((tm, tn), lambda i, j: (i, j)),
        out_shape=jax.ShapeDtypeStruct((n_rows, n), F32),
        compiler_params=_cparams("parallel", "parallel"),
        name="out_proj",
    )(a, w, z, modtab)


def _pack_halves(h):
    w = h.shape[1] // 2
    lo = lax.bitcast_convert_type(h[:, :w].astype(F32), jnp.uint32) >> 16
    hi = lax.bitcast_convert_type(h[:, w:].astype(F32), jnp.uint32) & jnp.uint32(0xFFFF0000)
    return lo | hi


def _unpack_halves(p):
    lo = lax.bitcast_convert_type(p << 16, F32).astype(BF16)
    hi = lax.bitcast_convert_type(p & jnp.uint32(0xFFFF0000), F32).astype(BF16)
    return jnp.concatenate([lo, hi], axis=1)


def _gather_rows(x, idx, *, window=32):
    m, w = idx.shape[0], x.shape[1]
    chunk = 128
    assert m % chunk == 0 and chunk % window == 0
    n_chunks = m // chunk
    mesh = plsc.VectorSubcoreMesh(core_axis_name="core", subcore_axis_name="subcore")
    n_workers = mesh.num_cores * mesh.num_subcores

    @pl.kernel(out_type=jax.ShapeDtypeStruct((m, w), x.dtype), mesh=mesh,
               scratch_types=[pltpu.VMEM((chunk,), jnp.int32), pltpu.VMEM((window, w), x.dtype)])
    def gather_kernel(x_hbm, i_hbm, o_hbm, idx_v, buf):
        wid = lax.axis_index("core") * mesh.num_subcores + lax.axis_index("subcore")

        def step(c, carry):
            r0 = (wid + c * n_workers) * chunk
            pltpu.sync_copy(i_hbm.at[pl.ds(r0, chunk)], idx_v)
            for k in range(chunk // window):
                pltpu.sync_copy(x_hbm.at[idx_v.at[pl.ds(k * window, window)]], buf)
                pltpu.sync_copy(buf, o_hbm.at[pl.ds(r0 + k * window, window)])
            return carry

        lax.fori_loop(0, (n_chunks - wid + n_workers - 1) // n_workers, step, 0)

    return gather_kernel(x, idx)


def _route_kernel(z_ref, g_ref, mod_ref, w_ref, b_ref, h_ref, r_ref):
    m = mod_ref[0]
    z = z_ref[...]
    y = z * lax.rsqrt(jnp.mean(z * z, axis=-1, keepdims=True) + EPS) * g_ref[...]
    h = (y * (1.0 + m[4:5]) + m[3:4]).astype(BF16)
    h_ref[...] = _pack_halves(h)
    logits = jnp.dot(h, w_ref[...].astype(BF16), preferred_element_type=F32) + b_ref[...]
    lane_i = lax.broadcasted_iota(jnp.int32, logits.shape, 1)
    lane = lane_i.astype(F32)
    far = float(2 * 128)

    def top(vals):
        best = jnp.max(vals, axis=-1, keepdims=True)
        return best, jnp.min(jnp.where(vals == best, lane, far), axis=-1, keepdims=True)

    is_grp = lane_i < N_GROUPS
    g_max, g_idx = top(jnp.where(is_grp, logits, NEG_BIG))
    p_grp = 1.0 / jnp.sum(jnp.where(is_grp, jnp.exp(logits - g_max), 0.0), axis=-1, keepdims=True)
    e_lane = lane_i - N_GROUPS
    in_grp = (e_lane >= 0) & ((e_lane >> 3).astype(F32) == g_idx)
    cand = jnp.where(in_grp, logits, NEG_BIG)
    t1, i1 = top(cand)
    t2, i2 = top(jnp.where(lane == i1, NEG_BIG, cand))
    e2 = jnp.exp(t2 - t1)
    w1 = p_grp / (1.0 + e2)
    w2 = w1 * e2
    r = jnp.where(lane_i == 0, i1 - N_GROUPS, jnp.where(lane_i == 1, i2 - N_GROUPS,
                  jnp.where(lane_i == 2, w1, jnp.where(lane_i == 3, w2, 0.0))))
    r_ref[...] = r


def _route(rows, z, norm_g, modtab, w_route, b_route, *, n_rows):
    d = z.shape[1]
    tm = rows.tile_rows(256)
    return pl.pallas_call(
        _route_kernel,
        grid=(n_rows // tm,),
        in_specs=[pl.BlockSpec((tm, d), lambda i: (i, 0)),
                  pl.BlockSpec((1, d), lambda i: (0, 0)),
                  pl.BlockSpec((1, 8, d), lambda i: (rows.mod_index(i, tm), 0, 0)),
                  pl.BlockSpec((d, 128), lambda i: (0, 0)),
                  pl.BlockSpec((1, 128), lambda i: (0, 0))],
        out_specs=[pl.BlockSpec((tm, d // 2), lambda i: (i, 0)),
                   pl.BlockSpec((tm, 128), lambda i: (i, 0))],
        out_shape=[jax.ShapeDtypeStruct((n_rows, d // 2), jnp.uint32),
                   jax.ShapeDtypeStruct((n_rows, 128), F32)],
        compiler_params=_cparams("parallel"),
        name="moe_route",
    )(z, norm_g.reshape(1, d), modtab, w_route, b_route)


def _rank_kernel(r_ref, rank_ref, cnt_ref, carry_scr):
    @pl.when(pl.program_id(0) == 0)
    def _():
        carry_scr[...] = jnp.zeros_like(carry_scr)

    r = r_ref[...]
    tm = r.shape[0]
    lane_i = lax.broadcasted_iota(jnp.int32, r.shape, 1)
    lane = lane_i.astype(F32)
    hot1 = lane == r[:, 0:1]
    hot2 = lane == r[:, 1:2]
    cnt = jnp.where(hot1 | hot2, 1.0, 0.0)
    row = lax.broadcasted_iota(jnp.int32, (tm, tm), 0)
    col = lax.broadcasted_iota(jnp.int32, (tm, tm), 1)
    before = jnp.where(col < row, 1.0, 0.0).astype(BF16)
    carry = carry_scr[0:1, :]
    prior = jnp.dot(before, cnt.astype(BF16), preferred_element_type=F32) + carry
    rank1 = jnp.sum(jnp.where(hot1, prior, 0.0), axis=-1, keepdims=True)
    rank2 = jnp.sum(jnp.where(hot2, prior, 0.0), axis=-1, keepdims=True)
    rank_ref[...] = jnp.where(lane_i == 0, rank1, jnp.where(lane_i == 1, rank2, 0.0))
    total = carry + jnp.sum(cnt, axis=0, keepdims=True)
    carry_scr[0:1, :] = total
    cnt_ref[...] = jnp.broadcast_to(total, cnt_ref.shape)


def _rank(r, *, tm):
    n = r.shape[0]
    return pl.pallas_call(
        _rank_kernel,
        grid=(n // tm,),
        in_specs=[pl.BlockSpec((tm, 128), lambda i: (i, 0))],
        out_specs=[pl.BlockSpec((tm, 128), lambda i: (i, 0)),
                   pl.BlockSpec((8, 128), lambda i: (0, 0))],
        out_shape=[jax.ShapeDtypeStruct((n, 128), F32), jax.ShapeDtypeStruct((8, 128), F32)],
        scratch_shapes=[pltpu.VMEM((8, 128), F32)],
        compiler_params=_cparams("arbitrary"),
        name="moe_rank",
    )(r)


def _expert_kernel(be_ref, fi_ref, nu_ref, x_ref, wg_ref, wu_ref, wd_ref, o_ref, wg_b, wu_b, wd_b):
    i = pl.program_id(0)

    @pl.when((i < nu_ref[0]) & (fi_ref[i] == 1))
    def _():
        wg_b[...] = wg_ref[...].astype(BF16)
        wu_b[...] = wu_ref[...].astype(BF16)
        wd_b[...] = wd_ref[...].astype(BF16)

    @pl.when(i < nu_ref[0])
    def _():
        x = _unpack_halves(x_ref[...])
        dot = lambda a, b: jnp.dot(a, b, preferred_element_type=F32)
        hid = _silu(dot(x, wg_b[...])) * dot(x, wu_b[...])
        o_ref[...] = dot(hid.astype(BF16), wd_b[...])

    @pl.when(i >= nu_ref[0])
    def _():
        o_ref[...] = jnp.zeros_like(o_ref)


def _expert_blocks(xs, blk_expert, first, n_used, w_gate, w_up, w_down, layer):
    n_rows, half = xs.shape
    d = 2 * half
    n_blocks = n_rows // MOE_BLOCK
    wspec = lambda a, b: pl.BlockSpec((None, None, a, b), lambda i, be, fi, nu: (layer, be[i], 0, 0))
    grid_spec = pltpu.PrefetchScalarGridSpec(
        num_scalar_prefetch=3,
        grid=(n_blocks,),
        in_specs=[pl.BlockSpec((MOE_BLOCK, half), lambda i, be, fi, nu: (i, 0)),
                  wspec(d, D_EXPERT), wspec(d, D_EXPERT), wspec(D_EXPERT, d)],
        out_specs=pl.BlockSpec((MOE_BLOCK, d), lambda i, be, fi, nu: (i, 0)),
        scratch_shapes=[pltpu.VMEM((d, D_EXPERT), BF16), pltpu.VMEM((d, D_EXPERT), BF16),
                        pltpu.VMEM((D_EXPERT, d), BF16)],
    )
    return pl.pallas_call(
        _expert_kernel,
        grid_spec=grid_spec,
        out_shape=jax.ShapeDtypeStruct((n_rows, d), F32),
        compiler_params=_cparams("arbitrary"),
        name="moe_experts",
    )(blk_expert, first, n_used, xs, w_gate, w_up, w_down)


def _hier_moe(rows, z, norm_g, modtab, w_grp, b_grp, w_exp, b_exp, w_gate, w_up, w_down, layer, *, n_rows):
    d = z.shape[1]
    n_logit = N_GROUPS + N_EXPERTS
    w_route = jnp.concatenate([w_grp, w_exp, jnp.zeros((d, 128 - n_logit), F32)], axis=1)
    b_route = jnp.concatenate([b_grp, b_exp, jnp.zeros((128 - n_logit,), F32)]).reshape(1, 128)
    h2, route = _route(rows, z, norm_g, modtab, w_route, b_route, n_rows=n_rows)
    rank, totals = _rank(route, tm=rows.tile_rows(512))
    eid = route[:, :TOP_K].astype(jnp.int32)
    wts = route[:, TOP_K:2 * TOP_K]
    counts = totals[0, :N_EXPERTS].astype(jnp.int32)
    padded = (counts + MOE_BLOCK - 1) // MOE_BLOCK * MOE_BLOCK
    pad_end = jnp.cumsum(padded)
    pad_start = pad_end - padded
    dest = pad_start[eid] + rank[:, :TOP_K].astype(jnp.int32)
    n_assign = n_rows * TOP_K
    n_blocks = (n_assign + N_EXPERTS * (MOE_BLOCK - 1) + MOE_BLOCK - 1) // MOE_BLOCK
    n_pad_rows = n_blocks * MOE_BLOCK
    tok = jnp.broadcast_to(jnp.arange(n_rows, dtype=jnp.int32)[:, None], (n_rows, TOP_K))
    row_tok = jnp.full((n_pad_rows,), n_rows, jnp.int32).at[dest.reshape(-1)].set(tok.reshape(-1))
    blk_start = jnp.arange(n_blocks, dtype=jnp.int32) * MOE_BLOCK
    blk_expert = jnp.minimum(jnp.sum((pad_end[None, :] <= blk_start[:, None]).astype(jnp.int32), axis=1),
                             N_EXPERTS - 1)
    first = jnp.concatenate([jnp.ones((1,), jnp.int32), (blk_expert[1:] != blk_expert[:-1]).astype(jnp.int32)])
    n_used = (pad_end[-1] // MOE_BLOCK).astype(jnp.int32).reshape(1)
    h2_pad = jnp.concatenate([h2, jnp.zeros((1, d // 2), h2.dtype)], axis=0)
    xs = _gather_rows(h2_pad, row_tok)
    ys = _expert_blocks(xs, blk_expert, first, n_used, w_gate, w_up, w_down, layer)
    picked = _gather_rows(ys, dest.T.reshape(-1), window=16)
    return picked[:n_rows] * wts[:, 0:1] + picked[n_rows:] * wts[:, 1:2]


def kernel(x, c, ctx, c_ctx, mod_w, mod_b, norm1_g, norm2_g, w_in, dn_conv, dn_a_log, dn_dt_bias, dn_norm_g, dn_out, lru_conv, lru_conv_b, lru_wa, lru_ba, lru_wx, lru_bx, lru_lambda, lru_out, att_qn_g, att_kn_g, att_sink, att_out, w_o, moe_w_grp, moe_b_grp, moe_w_exp, moe_b_exp, moe_w_gate, moe_w_up, moe_w_down):
    batch, seq, d = x.shape
    n_ctx = ctx.shape[1]
    depth = mod_w.shape[0]
    rows = _Rows(batch, seq, n_ctx)
    assert d == D_MODEL and batch + 1 <= 8

    z = jnp.concatenate([x.reshape(batch * seq, d), ctx.reshape(batch * n_ctx, d)], axis=0)

    cc = jnp.concatenate([c_ctx[None], c, jnp.zeros((7 - batch, d), F32)], axis=0)
    mods = _mod_tables(cc, mod_w, mod_b).reshape(depth, 8, 6, d)
    mods = jnp.concatenate([mods, jnp.zeros((depth, 8, 2, d), F32)], axis=2)

    cos, sin = _rope_tables(seq, n_ctx)
    pc = _ProjCols
    w_proj = _relayout_w_in(w_in)

    for l in range(depth):
        last = l == depth - 1
        modtab = mods[l]
        n_out = rows.n_lat_rows if last else rows.n_rows

        p = _norm_matmul(rows, z, norm1_g[l], modtab, w_proj, layer=l, shift_row=0, scale_row=1, tn=pc.tile,
                         name="in_proj")

        qkv = _dn_prep(rows, p, dn_conv[l])
        par = jnp.zeros((2, 8, 128), F32)
        par = par.at[:, 0, :N_HEADS].set(dn_a_log[l]).at[:, 1, :N_HEADS].set(dn_dt_bias[l])
        o_f = _dn_scan(rows, qkv, p, par, small_col=pc.scalars, reverse=False)
        o_b = _dn_scan(rows, qkv, p, par, small_col=pc.scalars, reverse=True)

        lru_args = (lru_conv[l], lru_conv_b[l], lru_wa[l], lru_wx[l], lru_ba[l], lru_bx[l], lru_lambda[l])
        hs_f = _lru_scan(rows, p, *lru_args, x_col=pc.lx, reverse=False)
        hs_b = _lru_scan(rows, p, *lru_args, x_col=pc.lx, reverse=True)

        qn, kn = _qk_prep(rows, p, cos, sin, att_qn_g[l], att_kn_g[l], pc.aq, pc.ak)
        sink_b = jnp.broadcast_to(att_sink[l][:, None], (N_HEADS, 128))
        y_c = _attention(rows, qn, kn, p, pc.av, sink_b, with_ctx_queries=not last)

        merged = _merge(rows, o_f, o_b, p, hs_f, hs_b, p, y_c, dn_out[l].astype(BF16), lru_out[l].astype(BF16),
                        att_out[l].astype(BF16), dn_norm_g[l], z_col=pc.z, lg_col=pc.lg, gate_col=pc.gates,
                        n_rows=n_out)
        z = _mm_residual(rows, merged, w_o[l].astype(BF16), z, modtab, gate_row=2, n_rows=n_out)

        f = _hier_moe(rows, z, norm2_g[l], modtab, moe_w_grp[l], moe_b_grp[l], moe_w_exp[l], moe_b_exp[l],
                      moe_w_gate, moe_w_up, moe_w_down, l, n_rows=n_out)
        gate2 = jnp.concatenate([jnp.repeat(modtab[1:1 + batch, 5], seq, axis=0),
                                 jnp.broadcast_to(modtab[0, 5], (batch * n_ctx, d))], axis=0)[:n_out]
        z = z + gate2 * f

    return z[:rows.n_lat_rows].reshape(batch, seq, d)
```

```python
import functools

import jax
import jax.numpy as jnp
from jax import lax
from jax.experimental import pallas as pl
from jax.experimental.pallas import tpu as pltpu
from jax.experimental.pallas import tpu_sc as plsc

F32 = jnp.float32
BF16 = jnp.bfloat16

EPS = 1e-6
D_MODEL = 2048
N_HEADS = 8
HEAD_DIM = 128
MIX_W = N_HEADS * HEAD_DIM
KV_HEADS = 2
KV_W = KV_HEADS * HEAD_DIM
DN_CHUNK = 64
DN_UNIT = 128
CONV_W = 4
LRU_C = 8.0
ATT_WINDOW = 128
GRID_W = 64
ROPE_THETA = 10000.0
N_GROUPS = 8
EXPERTS_PER_GROUP = 8
N_EXPERTS = N_GROUPS * EXPERTS_PER_GROUP
TOP_K = 2
D_EXPERT = 512
MOE_BLOCK = 128
ROW_BLK = 256
NEG_BIG = -1e30
VMEM_LIMIT = 56 * 1024 * 1024


def _cparams(*sem):
    return pltpu.CompilerParams(dimension_semantics=sem, vmem_limit_bytes=VMEM_LIMIT)


def _bdot(a, b):
    return jnp.dot(a.astype(BF16), b.astype(BF16), preferred_element_type=F32)


def _bdot_nt(a, b):
    return lax.dot_general(a.astype(BF16), b.astype(BF16), (((1,), (1,)), ((), ())),
                           preferred_element_type=F32)


def _dot01(m01, x):
    m = m01.astype(BF16)
    x0 = x.astype(BF16)
    r1 = x - x0.astype(F32)
    x1 = r1.astype(BF16)
    x2 = (r1 - x1.astype(F32)).astype(BF16)
    dot = lambda t: jnp.dot(m, t, preferred_element_type=F32)
    return dot(x0) + dot(x1) + dot(x2)


def _silu(x):
    return x * jax.nn.sigmoid(x)


def _softplus(x):
    return jnp.maximum(x, 0.0) + jnp.log1p(jnp.exp(-jnp.abs(x)))


class _Rows:
    def __init__(self, batch, seq, n_ctx):
        assert seq % ROW_BLK == 0 and n_ctx % ROW_BLK == 0
        self.batch, self.seq, self.n_ctx = batch, seq, n_ctx
        self.nlat = seq // ROW_BLK
        self.nctx = n_ctx // ROW_BLK
        self.n_lat_rows = batch * seq
        self.n_rows = batch * (seq + n_ctx)
        self.steps = self.nlat + self.nctx

    def seq_block(self, b, t, reverse):
        if reverse:
            jc, jl = self.nctx - 1 - t, self.nlat - 1 - (t - self.nctx)
        else:
            jc, jl = t, t - self.nctx
        return jnp.where(t < self.nctx, self.batch * self.nlat + b * self.nctx + jc, b * self.nlat + jl)

    def mod_index(self, tile, tm):
        r0 = tile * tm
        return jnp.where(r0 < self.n_lat_rows, 1 + r0 // self.seq, 0)

    def tile_rows(self, cap):
        tm = cap
        while self.seq % tm or (self.batch * self.n_ctx) % tm:
            tm //= 2
        return tm


def _modtab_kernel(c_ref, w_ref, b_ref, o_ref):
    o_ref[0] = _bdot(_silu(c_ref[...]), w_ref[0]) + b_ref[0]


def _mod_tables(cc, mod_w, mod_b):
    depth, d, n = mod_w.shape
    tn = 1024
    return pl.pallas_call(
        _modtab_kernel,
        grid=(depth, n // tn),
        in_specs=[pl.BlockSpec((8, d), lambda l, j: (0, 0)),
                  pl.BlockSpec((1, d, tn), lambda l, j: (l, 0, j)),
                  pl.BlockSpec((1, 1, tn), lambda l, j: (l, 0, j))],
        out_specs=pl.BlockSpec((1, 8, tn), lambda l, j: (l, 0, j)),
        out_shape=jax.ShapeDtypeStruct((depth, 8, n), F32),
        compiler_params=_cparams("parallel", "parallel"),
        name="mod_tables",
    )(cc, mod_w, mod_b.reshape(depth, 1, n))


class _ProjCols:
    tile = 512
    small = 4 * N_HEADS
    qkvz = 4 * MIX_W
    rest = 3 * MIX_W + 2 * KV_W + 3 * D_MODEL
    z, lx, lg, aq = 3 * MIX_W, 4 * MIX_W, 5 * MIX_W, 6 * MIX_W
    ak = aq + MIX_W
    av = ak + KV_W
    gates = av + KV_W
    scalars = qkvz + rest
    total = scalars + 4 * 128


def _relayout_kernel(a_ref, b_ref, s_ref, o_ref, *, n_aligned, n_tiles, shift):
    j = pl.program_id(1)

    @pl.when(j < n_aligned)
    def _():
        o_ref[0] = a_ref[0].astype(BF16)

    @pl.when((j >= n_aligned) & (j < n_tiles - 1))
    def _():
        o_ref[0] = jnp.concatenate([a_ref[0][:, shift:], b_ref[0][:, :shift]], axis=1).astype(BF16)

    @pl.when(j == n_tiles - 1)
    def _():
        o_ref[0] = s_ref[0].astype(BF16)


def _relayout_w_in(w_in):
    depth, d, n_in = w_in.shape
    pc = _ProjCols
    assert n_in == pc.qkvz + pc.small + pc.rest and pc.qkvz % pc.tile == 0 and pc.rest % pc.tile == 0
    n_aligned = pc.qkvz // pc.tile
    n_tiles = pc.total // pc.tile
    per = pc.tile // 128
    w_small = w_in[:, :, pc.qkvz:pc.qkvz + pc.small].reshape(depth, d, 4, N_HEADS)
    w_small = jnp.pad(w_small, ((0, 0), (0, 0), (0, 0), (0, 128 - N_HEADS))).reshape(depth, d, 4 * 128)
    return pl.pallas_call(
        functools.partial(_relayout_kernel, n_aligned=n_aligned, n_tiles=n_tiles, shift=pc.small),
        grid=(depth, n_tiles),
        in_specs=[pl.BlockSpec((1, d, pc.tile), lambda l, j: (l, 0, jnp.minimum(j, n_tiles - 2))),
                  pl.BlockSpec((1, d, 128), lambda l, j: (l, 0, jnp.minimum(j + 1, n_tiles - 1) * per)),
                  pl.BlockSpec((1, d, pc.tile), lambda l, j: (l, 0, 0))],
        out_specs=pl.BlockSpec((1, d, pc.tile), lambda l, j: (l, 0, j)),
        out_shape=jax.ShapeDtypeStruct((depth, d, pc.total), BF16),
        compiler_params=_cparams("parallel", "parallel"),
        name="relayout_w_in",
    )(w_in, w_in, w_small)


def _norm_mm_kernel(z_ref, g_ref, mod_ref, w_ref, o_ref, *rest, shift_row, scale_row, emit_h):
    h_scr = rest[-1]
    tm = z_ref.shape[0]
    sub = min(tm, 128)

    @pl.when(pl.program_id(1) == 0)
    def _():
        g = g_ref[...]
        m = mod_ref[0]
        scale1 = 1.0 + m[scale_row:scale_row + 1]
        shift = m[shift_row:shift_row + 1]

        def body(i, carry):
            r0 = pl.multiple_of(i * sub, sub)
            z = z_ref[pl.ds(r0, sub), :]
            y = z * lax.rsqrt(jnp.mean(z * z, axis=-1, keepdims=True) + EPS) * g
            h_scr[pl.ds(r0, sub), :] = (y * scale1 + shift).astype(BF16)
            return carry

        lax.fori_loop(0, tm // sub, body, 0)

    h = h_scr[...]
    o_ref[...] = jnp.dot(h, w_ref[...].astype(BF16), preferred_element_type=F32)
    if emit_h:
        @pl.when(pl.program_id(1) == 0)
        def _():
            rest[0][...] = h


def _norm_matmul(rows, z, g, modtab, w, *, shift_row, scale_row, tn, layer=None, tm_cap=1024,
                 n_rows=None, emit_h=False, name="norm_matmul"):
    n_rows = rows.n_rows if n_rows is None else n_rows
    d = z.shape[1]
    n_cols = w.shape[-1]
    tm = rows.tile_rows(tm_cap)
    assert n_rows % tm == 0 and n_cols % tn == 0
    if layer is None:
        w_spec = pl.BlockSpec((d, tn), lambda i, j: (0, j))
    else:
        w_spec = pl.BlockSpec((None, d, tn), lambda i, j: (layer, 0, j))
    out_shape = [jax.ShapeDtypeStruct((n_rows, n_cols), F32)]
    out_specs = [pl.BlockSpec((tm, tn), lambda i, j: (i, j))]
    if emit_h:
        out_shape.append(jax.ShapeDtypeStruct((n_rows, d), BF16))
        out_specs.append(pl.BlockSpec((tm, d), lambda i, j: (i, 0)))
    res = pl.pallas_call(
        functools.partial(_norm_mm_kernel, shift_row=shift_row, scale_row=scale_row, emit_h=emit_h),
        grid=(n_rows // tm, n_cols // tn),
        in_specs=[pl.BlockSpec((tm, d), lambda i, j: (i, 0)),
                  pl.BlockSpec((1, d), lambda i, j: (0, 0)),
                  pl.BlockSpec((1, 8, d), lambda i, j: (rows.mod_index(i, tm), 0, 0)),
                  w_spec],
        out_specs=out_specs,
        out_shape=out_shape,
        scratch_shapes=[pltpu.VMEM((tm, d), BF16)],
        compiler_params=_cparams("parallel", "arbitrary"),
        name=name,
    )(z, g.reshape(1, d), modtab, w)
    return res if emit_h else res[0]


def _halo_flags(rows, blk):
    is_lat = blk < rows.batch * rows.nlat
    j = jnp.where(is_lat, blk % rows.nlat, (blk - rows.batch * rows.nlat) % rows.nctx)
    nseg = jnp.where(is_lat, rows.nlat, rows.nctx)
    return (j != 0).astype(F32), (j != nseg - 1).astype(F32)


def _conv4(prev8, cur, next8, w, use_prev, use_next):
    n = cur.shape[0]
    x = jnp.concatenate([prev8 * use_prev, cur, next8 * use_next], axis=0)
    tot = n + 16
    y = w[1:2] * cur
    y = y + w[0:1] * pltpu.roll(x, 1, 0)[8:8 + n]
    y = y + w[2:3] * pltpu.roll(x, tot - 1, 0)[8:8 + n]
    y = y + w[3:4] * pltpu.roll(x, tot - 2, 0)[8:8 + n]
    return y


def _halo_specs(rows, width, col_blk, blk_of):
    per = ROW_BLK // 8
    last8 = rows.n_rows // 8 - 1
    return [
        pl.BlockSpec((8, width), lambda *g: (jnp.maximum(blk_of(*g) * per - 1, 0), col_blk(*g))),
        pl.BlockSpec((ROW_BLK, width), lambda *g: (blk_of(*g), col_blk(*g))),
        pl.BlockSpec((8, width), lambda *g: (jnp.minimum((blk_of(*g) + 1) * per, last8), col_blk(*g))),
    ]


def _dn_prep_kernel(prev_ref, cur_ref, next_ref, w_ref, o_ref, *, rows):
    blk = pl.program_id(0)
    kind = pl.program_id(1)
    use_prev, use_next = _halo_flags(rows, blk)
    y = _silu(_conv4(prev_ref[...], cur_ref[...], next_ref[...], w_ref[...], use_prev, use_next))
    q_scale = jnp.where(kind == 0, HEAD_DIM ** -0.5, 1.0)
    for h in range(N_HEADS):
        sl = slice(h * HEAD_DIM, (h + 1) * HEAD_DIM)
        yh = y[:, sl]
        inv = lax.rsqrt(jnp.sum(yh * yh, axis=-1, keepdims=True) + EPS)
        o_ref[:, sl] = yh * (jnp.where(kind == 2, 1.0, inv) * q_scale)


def _dn_prep(rows, p_a, conv_w):
    return pl.pallas_call(
        functools.partial(_dn_prep_kernel, rows=rows),
        grid=(rows.n_rows // ROW_BLK, 3),
        in_specs=_halo_specs(rows, MIX_W, lambda i, j: j, lambda i, j: i)
        + [pl.BlockSpec((CONV_W, MIX_W), lambda i, j: (0, j))],
        out_specs=pl.BlockSpec((ROW_BLK, MIX_W), lambda i, j: (i, j)),
        out_shape=jax.ShapeDtypeStruct((rows.n_rows, 3 * MIX_W), F32),
        compiler_params=_cparams("parallel", "parallel"),
        name="dn_prep",
    )(p_a, p_a, p_a, conv_w)


def _dot_b(a, b):
    return jnp.dot(a, b, preferred_element_type=F32).astype(BF16)


def _dn_scan_kernel(q_ref, k_ref, v_ref, beta_ref, alpha_ref, par_ref, o_ref,
                    s_scr, u_scr, wq_scr, kdt_scr, aqk_scr, *, reverse):
    n = ROW_BLK
    nchunk = n // DN_CHUNK
    per_unit = DN_UNIT // DN_CHUNK

    @pl.when(pl.program_id(1) == 0)
    def _():
        s_scr[...] = jnp.zeros_like(s_scr)

    def tri(m):
        row = lax.broadcasted_iota(jnp.int32, (m, m), 0)
        col = lax.broadcasted_iota(jnp.int32, (m, m), 1)
        same = lambda s: (row >> (s.bit_length() - 1)) == (col >> (s.bit_length() - 1))
        ahead = (col >= row) if reverse else (col <= row)
        return row, col, same, same(DN_CHUNK) & ahead, same(DN_CHUNK) & ahead & (row != col)

    _, _, same_n, incl_n, _ = tri(n)
    par = par_ref[0]
    beta_all = jax.nn.sigmoid(beta_ref[...])
    g_all = -jnp.exp(par[0:1]) * _softplus(alpha_ref[...] + par[1:2])
    gc_all = _dot01(incl_n.astype(F32), g_all)
    gt_all = _dot01(same_n(DN_CHUNK).astype(F32), g_all)
    gc_t = gc_all.T
    egc_all = jnp.exp(gc_all)
    ekd_all = jnp.exp(gt_all - gc_all)
    egt_all = jnp.exp(gt_all)

    row_u, col_u, same_u, incl_u, strict_u = tri(DN_UNIT)
    one_b = lambda m: jnp.where(m, 1.0, 0.0).astype(BF16)
    eye_b = one_b(row_u == col_u)
    diag8_b = one_b(same_u(8))
    off_b = {s: one_b(same_u(2 * s) & jnp.logical_not(same_u(s))) for s in (8, 16, 32)}

    units = [(h, j) for h in range(N_HEADS) for j in range(n // DN_UNIT)]
    a_b, rhs_b = [], []
    for h, j in units:
        rs = slice(j * DN_UNIT, (j + 1) * DN_UNIT)
        sl = slice(h * HEAD_DIM, (h + 1) * HEAD_DIM)
        q, k, v = q_ref[rs, sl], k_ref[rs, sl], v_ref[rs, sl]
        bcol = beta_all[rs, h:h + 1]
        egc = egc_all[rs, h:h + 1]
        dec = jnp.exp(jnp.where(incl_u, gc_all[rs, h:h + 1] - gc_t[h:h + 1, rs], NEG_BIG))
        kb = k * bcol
        k_b = k.astype(BF16)
        a_b.append(jnp.where(strict_u, _bdot_nt(kb, k_b) * dec, 0.0).astype(BF16))
        aqk_scr[h, rs, :] = (_bdot_nt(q, k_b) * dec).astype(BF16)
        rhs_b.append(jnp.concatenate([v * bcol, kb * egc], axis=1).astype(BF16))
        qd = (q * egc).astype(BF16)
        kd = k * ekd_all[rs, h:h + 1]
        for c in range(per_unit):
            cs = slice(c * DN_CHUNK, (c + 1) * DN_CHUNK)
            wq_scr[h, j * per_unit + c, DN_CHUNK:, :] = qd[cs]
            kdt_scr[h, j * per_unit + c] = kd[cs].T.astype(BF16)

    d1 = [a * diag8_b for a in a_b]
    d2 = [_dot_b(d, d) for d in d1]
    d4 = [_dot_b(d, d) for d in d2]
    x = [_dot_b(eye_b - d, eye_b + e) for d, e in zip(d1, d2)]
    x = [_dot_b(xx, eye_b + e) for xx, e in zip(x, d4)]
    for s in (8, 16, 32):
        t = [_dot_b(a * off_b[s], xx) for a, xx in zip(a_b, x)]
        x = [xx - _dot_b(xx, tt) for xx, tt in zip(x, t)]
    for (h, j), xx, rhs in zip(units, x, rhs_b):
        uw = jnp.dot(xx, rhs, preferred_element_type=F32)
        u_scr[h, j * DN_UNIT:(j + 1) * DN_UNIT, :] = uw[:, :HEAD_DIM]
        for c in range(per_unit):
            wq_scr[h, j * per_unit + c, :DN_CHUNK, :] = uw[c * DN_CHUNK:(c + 1) * DN_CHUNK, HEAD_DIM:].astype(BF16)

    zeros_b = jnp.zeros((DN_CHUNK, HEAD_DIM), BF16)
    for ci in range(nchunk):
        c = nchunk - 1 - ci if reverse else ci
        rs = slice(c * DN_CHUNK, (c + 1) * DN_CHUNK)
        for h in range(N_HEADS):
            s_h = s_scr[h]
            r = jnp.dot(wq_scr[h, c], s_h.astype(BF16), preferred_element_type=F32)
            v_new = (u_scr[h, rs, :] - r[:DN_CHUNK]).astype(BF16)
            v_unit = jnp.concatenate([v_new, zeros_b] if c % per_unit == 0 else [zeros_b, v_new], axis=0)
            o = r[DN_CHUNK:] + jnp.dot(aqk_scr[h, rs, :], v_unit, preferred_element_type=F32)
            o_ref[rs, h * HEAD_DIM:(h + 1) * HEAD_DIM] = o
            s_scr[h] = s_h * egt_all[c * DN_CHUNK:c * DN_CHUNK + 1, h:h + 1] + jnp.dot(
                kdt_scr[h, c], v_new, preferred_element_type=F32)


def _dn_scan(rows, qkv, p, par, *, small_col, reverse):
    d = 1 if reverse else 0
    blk = lambda b, t: rows.seq_block(b, t, reverse)
    nchunk = ROW_BLK // DN_CHUNK
    sc = small_col // 128
    return pl.pallas_call(
        functools.partial(_dn_scan_kernel, reverse=reverse),
        grid=(rows.batch, rows.steps),
        in_specs=[pl.BlockSpec((ROW_BLK, MIX_W), lambda b, t: (blk(b, t), 0)),
                  pl.BlockSpec((ROW_BLK, MIX_W), lambda b, t: (blk(b, t), 1)),
                  pl.BlockSpec((ROW_BLK, MIX_W), lambda b, t: (blk(b, t), 2)),
                  pl.BlockSpec((ROW_BLK, 128), lambda b, t: (blk(b, t), sc + d)),
                  pl.BlockSpec((ROW_BLK, 128), lambda b, t: (blk(b, t), sc + 2 + d)),
                  pl.BlockSpec((1, 8, 128), lambda b, t: (d, 0, 0))],
        out_specs=pl.BlockSpec((ROW_BLK, MIX_W), lambda b, t: (blk(b, t), 0)),
        out_shape=jax.ShapeDtypeStruct((rows.n_rows, MIX_W), F32),
        scratch_shapes=[pltpu.VMEM((N_HEADS, HEAD_DIM, HEAD_DIM), F32),
                        pltpu.VMEM((N_HEADS, ROW_BLK, HEAD_DIM), F32),
                        pltpu.VMEM((N_HEADS, nchunk, 2 * DN_CHUNK, HEAD_DIM), BF16),
                        pltpu.VMEM((N_HEADS, nchunk, HEAD_DIM, DN_CHUNK), BF16),
                        pltpu.VMEM((N_HEADS, ROW_BLK, DN_UNIT), BF16)],
        compiler_params=_cparams("parallel", "arbitrary"),
        name="dn_scan_bwd" if reverse else "dn_scan_fwd",
    )(qkv, qkv, qkv, p, p, par)


def _lru_kernel(prev_ref, cur_ref, next_ref, cw_ref, cb_ref, wa_ref, wx_ref, ba_ref, bx_ref, lam_ref,
                o_ref, h_scr, a_scr, u_scr, *, rows, reverse):
    b, t = pl.program_id(0), pl.program_id(1)

    @pl.when(t == 0)
    def _():
        h_scr[...] = jnp.zeros_like(h_scr)

    use_prev, use_next = _halo_flags(rows, rows.seq_block(b, t, reverse))
    xc = _conv4(prev_ref[...], cur_ref[...], next_ref[...], cw_ref[...], use_prev, use_next) + cb_ref[...]
    for kb in range(N_HEADS):
        sl = slice(kb * HEAD_DIM, (kb + 1) * HEAD_DIM)
        xb = xc[:, sl]
        r = jax.nn.sigmoid(_bdot(xb, wa_ref[0, kb]) + ba_ref[0, :, sl])
        i = jax.nn.sigmoid(_bdot(xb, wx_ref[0, kb]) + bx_ref[0, :, sl])
        log_a = -LRU_C * r * _softplus(-lam_ref[0, :, sl])
        a_scr[:, sl] = jnp.exp(log_a)
        u_scr[:, sl] = jnp.sqrt(1.0 - jnp.exp(2.0 * log_a)) * (i * xb)

    ngrp = ROW_BLK // 8
    sub = lax.broadcasted_iota(jnp.int32, (8, MIX_W), 0)

    def group_step(gi, h_prev):
        g = ngrp - 1 - gi if reverse else gi
        r0 = pl.multiple_of(g * 8, 8)
        a = a_scr[pl.ds(r0, 8), :]
        u = u_scr[pl.ds(r0, 8), :]
        for dist in (1, 2, 4):
            if reverse:
                a_sh, u_sh, m = pltpu.roll(a, 8 - dist, 0), pltpu.roll(u, 8 - dist, 0), sub < 8 - dist
            else:
                a_sh, u_sh, m = pltpu.roll(a, dist, 0), pltpu.roll(u, dist, 0), sub >= dist
            u = jnp.where(m, a * u_sh + u, u)
            a = jnp.where(m, a * a_sh, a)
        h = u + a * h_prev
        o_ref[pl.ds(r0, 8), :] = h
        return h[0:1] if reverse else h[7:8]

    h_scr[...] = lax.fori_loop(0, ngrp, group_step, h_scr[...])


def _lru_scan(rows, p_b, cw, cb, wa, wx, ba, bx, lam, *, x_col, reverse):
    d = 1 if reverse else 0
    blk = lambda b, t: rows.seq_block(b, t, reverse)
    vec = lambda: pl.BlockSpec((1, 1, MIX_W), lambda b, t: (d, 0, 0))
    mat = lambda: pl.BlockSpec((1, N_HEADS, HEAD_DIM, HEAD_DIM), lambda b, t: (d, 0, 0, 0))
    return pl.pallas_call(
        functools.partial(_lru_kernel, rows=rows, reverse=reverse),
        grid=(rows.batch, rows.steps),
        in_specs=_halo_specs(rows, MIX_W, lambda b, t: x_col // MIX_W, blk)
        + [pl.BlockSpec((CONV_W, MIX_W), lambda b, t: (0, 0)),
           pl.BlockSpec((1, MIX_W), lambda b, t: (0, 0)),
           mat(), mat(), vec(), vec(), vec()],
        out_specs=pl.BlockSpec((ROW_BLK, MIX_W), lambda b, t: (blk(b, t), 0)),
        out_shape=jax.ShapeDtypeStruct((rows.n_rows, MIX_W), F32),
        scratch_shapes=[pltpu.VMEM((1, MIX_W), F32),
                        pltpu.VMEM((ROW_BLK, MIX_W), F32),
                        pltpu.VMEM((ROW_BLK, MIX_W), F32)],
        compiler_params=_cparams("parallel", "arbitrary"),
        name="lru_bwd" if reverse else "lru_fwd",
    )(p_b, p_b, p_b, cw, cb.reshape(1, MIX_W), wa, wx,
      ba.reshape(2, 1, MIX_W), bx.reshape(2, 1, MIX_W), lam.reshape(2, 1, MIX_W))


def _rope_tables(seq, n_ctx):
    half = HEAD_DIM // 2
    pos = jnp.arange(seq)
    inv = ROPE_THETA ** (-jnp.arange(0, half, 2, dtype=F32) / half)
    ang_r = (pos // GRID_W).astype(F32)[:, None] * inv
    ang_c = (pos % GRID_W).astype(F32)[:, None] * inv
    cos = jnp.concatenate([jnp.cos(ang_r)] * 2 + [jnp.cos(ang_c)] * 2, axis=-1)
    sin = jnp.concatenate([-jnp.sin(ang_r), jnp.sin(ang_r), -jnp.sin(ang_c), jnp.sin(ang_c)], axis=-1)
    cos = jnp.concatenate([cos, jnp.ones((n_ctx, HEAD_DIM), F32)], axis=0)
    sin = jnp.concatenate([sin, jnp.zeros((n_ctx, HEAD_DIM), F32)], axis=0)
    return cos, sin


def _qk_prep_kernel(q_ref, k_ref, cos_ref, sin_ref, qg_ref, kg_ref, qo_ref, ko_ref):
    cos, sin = cos_ref[...], sin_ref[...]
    lane = lax.broadcasted_iota(jnp.int32, cos.shape, 1)
    first = (lane & (HEAD_DIM // 2 - 1)) < (HEAD_DIM // 4)

    def norm_rope(x, g):
        y = x * lax.rsqrt(jnp.mean(x * x, axis=-1, keepdims=True) + EPS) * g
        partner = jnp.where(first, pltpu.roll(y, HEAD_DIM - HEAD_DIM // 4, 1), pltpu.roll(y, HEAD_DIM // 4, 1))
        return y * cos + partner * sin

    for h in range(N_HEADS):
        sl = slice(h * HEAD_DIM, (h + 1) * HEAD_DIM)
        qo_ref[:, sl] = (norm_rope(q_ref[:, sl], qg_ref[...]) * (HEAD_DIM ** -0.5)).astype(BF16)
    for h in range(KV_HEADS):
        sl = slice(h * HEAD_DIM, (h + 1) * HEAD_DIM)
        ko_ref[:, sl] = norm_rope(k_ref[:, sl], kg_ref[...]).astype(BF16)


def _qk_prep(rows, p_b, cos, sin, qn_g, kn_g, q_col, k_col):
    def tab(i):
        return jnp.where(i < rows.batch * rows.nlat, i % rows.nlat,
                         rows.nlat + (i - rows.batch * rows.nlat) % rows.nctx)

    return pl.pallas_call(
        _qk_prep_kernel,
        grid=(rows.n_rows // ROW_BLK,),
        in_specs=[pl.BlockSpec((ROW_BLK, MIX_W), lambda i: (i, q_col // MIX_W)),
                  pl.BlockSpec((ROW_BLK, KV_W), lambda i: (i, k_col // KV_W)),
                  pl.BlockSpec((ROW_BLK, HEAD_DIM), lambda i: (tab(i), 0)),
                  pl.BlockSpec((ROW_BLK, HEAD_DIM), lambda i: (tab(i), 0)),
                  pl.BlockSpec((1, HEAD_DIM), lambda i: (0, 0)),
                  pl.BlockSpec((1, HEAD_DIM), lambda i: (0, 0))],
        out_specs=[pl.BlockSpec((ROW_BLK, MIX_W), lambda i: (i, 0)),
                   pl.BlockSpec((ROW_BLK, KV_W), lambda i: (i, 0))],
        out_shape=[jax.ShapeDtypeStruct((rows.n_rows, MIX_W), BF16),
                   jax.ShapeDtypeStruct((rows.n_rows, KV_W), BF16)],
        compiler_params=_cparams("parallel"),
        name="qk_prep",
    )(p_b, p_b, cos, sin, qn_g.reshape(1, HEAD_DIM), kn_g.reshape(1, HEAD_DIM))


def _attn_core(q, keys, vals, sink_ref, valid, o_ref):
    grp = N_HEADS // KV_HEADS
    nq = q.shape[0]
    for kvh in range(KV_HEADS):
        sl = slice(kvh * HEAD_DIM, (kvh + 1) * HEAD_DIM)
        kk = jnp.concatenate([t[:, sl] for t in keys], axis=0)
        vv = jnp.concatenate([t[:, sl].astype(BF16) for t in vals], axis=0)
        q4 = jnp.concatenate([q[:, (kvh * grp + g) * HEAD_DIM:(kvh * grp + g + 1) * HEAD_DIM]
                              for g in range(grp)], axis=0)
        s = _bdot_nt(q4, kk)
        if valid is not None:
            s = jnp.where(valid, s, NEG_BIG)
        sink = jnp.concatenate([jnp.broadcast_to(sink_ref[kvh * grp + g:kvh * grp + g + 1, 0:1], (nq, 1))
                                for g in range(grp)], axis=0)
        m = jnp.maximum(jnp.max(s, axis=-1, keepdims=True), sink)
        e = jnp.exp(s - m)
        p = e / (jnp.sum(e, axis=-1, keepdims=True) + jnp.exp(sink - m))
        o = _bdot(p, vv)
        for g in range(grp):
            hq = kvh * grp + g
            o_ref[:, hq * HEAD_DIM:(hq + 1) * HEAD_DIM] = o[g * nq:(g + 1) * nq].astype(BF16)


def _attn_kernel(q_ref, k0_ref, k1_ref, k2_ref, kc_ref, v0_ref, v1_ref, v2_ref, vc_ref, sink_ref, o_ref,
                 *, seq, nb):
    n = pl.program_id(1)

    @pl.when(n < nb)
    def _():
        blk = q_ref.shape[0]
        n_loc = 3 * blk
        n_ctx = kc_ref.shape[0]
        grp = N_HEADS // KV_HEADS
        qi = lax.broadcasted_iota(jnp.int32, (grp * blk, n_loc + n_ctx), 0) & (blk - 1)
        kj = lax.broadcasted_iota(jnp.int32, (grp * blk, n_loc + n_ctx), 1)
        rel = kj - blk
        kpos = n * blk + rel
        local_ok = (jnp.abs(qi - rel) <= ATT_WINDOW) & (kpos >= 0) & (kpos < seq)
        valid = local_ok | (kj >= n_loc)
        _attn_core(q_ref[...], [k0_ref[...], k1_ref[...], k2_ref[...], kc_ref[...]],
                   [v0_ref[...], v1_ref[...], v2_ref[...], vc_ref[...]], sink_ref, valid, o_ref)

    @pl.when(n >= nb)
    def _():
        _attn_core(q_ref[...], [kc_ref[...]], [vc_ref[...]], sink_ref, None, o_ref)


def _attention(rows, qn, kn, p_b, v_col, sink_b, *, with_ctx_queries):
    blk = 128
    nb = rows.seq // blk
    vcb = v_col // KV_W
    ctx_blk0 = rows.n_lat_rows // rows.n_ctx
    assert rows.n_lat_rows % rows.n_ctx == 0
    kspec = lambda off: pl.BlockSpec((blk, KV_W), lambda b, n: (b * nb + jnp.clip(n + off, 0, nb - 1), 0))
    vspec = lambda off: pl.BlockSpec((blk, KV_W), lambda b, n: (b * nb + jnp.clip(n + off, 0, nb - 1), vcb))
    n_out = rows.n_rows if with_ctx_queries else rows.n_lat_rows
    ncb = rows.n_ctx // blk if with_ctx_queries else 0
    q0 = rows.n_lat_rows // blk
    qrow = lambda b, n: jnp.where(n < nb, b * nb + n, q0 + b * ncb + (n - nb))
    return pl.pallas_call(
        functools.partial(_attn_kernel, seq=rows.seq, nb=nb),
        grid=(rows.batch, nb + ncb),
        in_specs=[pl.BlockSpec((blk, MIX_W), lambda b, n: (qrow(b, n), 0)),
                  kspec(-1), kspec(0), kspec(1),
                  pl.BlockSpec((rows.n_ctx, KV_W), lambda b, n: (ctx_blk0 + b, 0)),
                  vspec(-1), vspec(0), vspec(1),
                  pl.BlockSpec((rows.n_ctx, KV_W), lambda b, n: (ctx_blk0 + b, vcb)),
                  pl.BlockSpec((8, 128), lambda b, n: (0, 0))],
        out_specs=pl.BlockSpec((blk, MIX_W), lambda b, n: (qrow(b, n), 0)),
        out_shape=jax.ShapeDtypeStruct((n_out, MIX_W), BF16),
        compiler_params=_cparams("parallel", "parallel"),
        name="attention",
    )(qn, kn, kn, kn, kn, p_b, p_b, p_b, p_b, sink_b)


def _merge_kernel(of_ref, ob_ref, z_ref, hf_ref, hb_ref, lg_ref, yc_ref, ga_ref, gb_ref, gc_ref,
                  wa_ref, wb_ref, wc_ref, ng_ref, o_ref, ya_scr, yb_scr):
    tm = of_ref.shape[0]
    sub = min(tm, 128)

    @pl.when(pl.program_id(1) == 0)
    def _():
        ng = ng_ref[...]

        def body(i, carry):
            r0 = pl.multiple_of(i * sub, sub)
            rs = pl.ds(r0, sub)
            for h in range(N_HEADS):
                sl = slice(h * HEAD_DIM, (h + 1) * HEAD_DIM)
                o = of_ref[rs, sl] + ob_ref[rs, sl]
                y = o * lax.rsqrt(jnp.mean(o * o, axis=-1, keepdims=True) + EPS) * ng
                ya_scr[rs, sl] = (y * _silu(z_ref[rs, sl])).astype(BF16)
            yb_scr[rs, :] = (jax.nn.gelu(lg_ref[rs, :]) * (hf_ref[rs, :] + hb_ref[rs, :])).astype(BF16)
            return carry

        lax.fori_loop(0, tm // sub, body, 0)

    acc = jax.nn.sigmoid(ga_ref[...]) * _bdot(ya_scr[...], wa_ref[...])
    acc = acc + jax.nn.sigmoid(gb_ref[...]) * _bdot(yb_scr[...], wb_ref[...])
    acc = acc + jax.nn.sigmoid(gc_ref[...]) * _bdot(yc_ref[...], wc_ref[...])
    o_ref[...] = acc.astype(BF16)


def _merge(rows, o_f, o_b, p_a, hs_f, hs_b, p_b, y_c, dn_out, lru_out, att_out, dn_norm_g, *,
           z_col, lg_col, gate_col, n_rows):
    tm = rows.tile_rows(512)
    tn = 512
    assert n_rows % tm == 0 and gate_col % tn == 0
    row = lambda cb: pl.BlockSpec((tm, MIX_W), lambda i, j: (i, cb))
    gate = lambda br: pl.BlockSpec((tm, tn), lambda i, j: (i, (gate_col + br * D_MODEL) // tn + j))
    wgt = lambda: pl.BlockSpec((MIX_W, tn), lambda i, j: (0, j))
    return pl.pallas_call(
        _merge_kernel,
        grid=(n_rows // tm, D_MODEL // tn),
        in_specs=[row(0), row(0), row(z_col // MIX_W), row(0), row(0), row(lg_col // MIX_W), row(0),
                  gate(0), gate(1), gate(2), wgt(), wgt(), wgt(),
                  pl.BlockSpec((1, HEAD_DIM), lambda i, j: (0, 0))],
        out_specs=pl.BlockSpec((tm, tn), lambda i, j: (i, j)),
        out_shape=jax.ShapeDtypeStruct((n_rows, D_MODEL), BF16),
        scratch_shapes=[pltpu.VMEM((tm, MIX_W), BF16), pltpu.VMEM((tm, MIX_W), BF16)],
        compiler_params=_cparams("parallel", "arbitrary"),
        name="merge",
    )(o_f, o_b, p_a, hs_f, hs_b, p_b, y_c, p_b, p_b, p_b, dn_out, lru_out, att_out,
      dn_norm_g.reshape(1, HEAD_DIM))


def _mm_residual_kernel(a_ref, w_ref, z_ref, mod_ref, o_ref, *, gate_row):
    acc = _bdot(a_ref[...], w_ref[...])
    o_ref[...] = z_ref[...] + mod_ref[0][gate_row:gate_row + 1] * acc


def _mm_residual(rows, a, w, z, modtab, *, gate_row, n_rows):
    tm = rows.tile_rows(1024)
    tn = 512
    k, n = w.shape
    assert n_rows % tm == 0
    return pl.pallas_call(
        functools.partial(_mm_residual_kernel, gate_row=gate_row),
        grid=(n_rows // tm, n // tn),
        in_specs=[pl.BlockSpec((tm, k), lambda i, j: (i, 0)),
                  pl.BlockSpec((k, tn), lambda i, j: (0, j)),
                  pl.BlockSpec((tm, tn), lambda i, j: (i, j)),
                  pl.BlockSpec((1, 8, tn), lambda i, j: (rows.mod_index(i, tm), 0, j))],
        out_specs=pl.BlockSpec((tm, tn), lambda i, j: (i, j)),
        out_shape=jax.ShapeDtypeStruct((n_rows, n), F32),
        compiler_params=_cparams("parallel", "parallel"),
        name="out_proj",
    )(a, w, z, modtab)


def _pack_halves(h):
    w = h.shape[1] // 2
    lo = lax.bitcast_convert_type(h[:, :w].astype(F32), jnp.uint32) >> 16
    hi = lax.bitcast_convert_type(h[:, w:].astype(F32), jnp.uint32) & jnp.uint32(0xFFFF0000)
    return lo | hi


def _unpack_halves(p):
    lo = lax.bitcast_convert_type(p << 16, F32).astype(BF16)
    hi = lax.bitcast_convert_type(p & jnp.uint32(0xFFFF0000), F32).astype(BF16)
    return jnp.concatenate([lo, hi], axis=1)


def _gather_rows(x, idx, *, window=32):
    m, w = idx.shape[0], x.shape[1]
    chunk = 128
    assert m % chunk == 0 and chunk % window == 0
    n_chunks = m // chunk
    mesh = plsc.VectorSubcoreMesh(core_axis_name="core", subcore_axis_name="subcore")
    n_workers = mesh.num_cores * mesh.num_subcores

    @pl.kernel(out_type=jax.ShapeDtypeStruct((m, w), x.dtype), mesh=mesh,
               scratch_types=[pltpu.VMEM((chunk,), jnp.int32), pltpu.VMEM((window, w), x.dtype)])
    def gather_kernel(x_hbm, i_hbm, o_hbm, idx_v, buf):
        wid = lax.axis_index("core") * mesh.num_subcores + lax.axis_index("subcore")

        def step(c, carry):
            r0 = (wid + c * n_workers) * chunk
            pltpu.sync_copy(i_hbm.at[pl.ds(r0, chunk)], idx_v)
            for k in range(chunk // window):
                pltpu.sync_copy(x_hbm.at[idx_v.at[pl.ds(k * window, window)]], buf)
                pltpu.sync_copy(buf, o_hbm.at[pl.ds(r0 + k * window, window)])
            return carry

        lax.fori_loop(0, (n_chunks - wid + n_workers - 1) // n_workers, step, 0)

    return gather_kernel(x, idx)


def _route_kernel(z_ref, g_ref, mod_ref, w_ref, b_ref, h_ref, r_ref):
    m = mod_ref[0]
    z = z_ref[...]
    y = z * lax.rsqrt(jnp.mean(z * z, axis=-1, keepdims=True) + EPS) * g_ref[...]
    h = (y * (1.0 + m[4:5]) + m[3:4]).astype(BF16)
    h_ref[...] = _pack_halves(h)
    logits = jnp.dot(h, w_ref[...].astype(BF16), preferred_element_type=F32) + b_ref[...]
    lane_i = lax.broadcasted_iota(jnp.int32, logits.shape, 1)
    lane = lane_i.astype(F32)
    far = float(2 * 128)

    def top(vals):
        best = jnp.max(vals, axis=-1, keepdims=True)
        return best, jnp.min(jnp.where(vals == best, lane, far), axis=-1, keepdims=True)

    is_grp = lane_i < N_GROUPS
    g_max, g_idx = top(jnp.where(is_grp, logits, NEG_BIG))
    p_grp = 1.0 / jnp.sum(jnp.where(is_grp, jnp.exp(logits - g_max), 0.0), axis=-1, keepdims=True)
    e_lane = lane_i - N_GROUPS
    in_grp = (e_lane >= 0) & ((e_lane >> 3).astype(F32) == g_idx)
    cand = jnp.where(in_grp, logits, NEG_BIG)
    t1, i1 = top(cand)
    t2, i2 = top(jnp.where(lane == i1, NEG_BIG, cand))
    e2 = jnp.exp(t2 - t1)
    w1 = p_grp / (1.0 + e2)
    w2 = w1 * e2
    r = jnp.where(lane_i == 0, i1 - N_GROUPS, jnp.where(lane_i == 1, i2 - N_GROUPS,
                  jnp.where(lane_i == 2, w1, jnp.where(lane_i == 3, w2, 0.0))))
    r_ref[...] = r


def _route(rows, z, norm_g, modtab, w_route, b_route, *, n_rows):
    d = z.shape[1]
    tm = rows.tile_rows(256)
    return pl.pallas_call(
        _route_kernel,
        grid=(n_rows // tm,),
        in_specs=[pl.BlockSpec((tm, d), lambda i: (i, 0)),
                  pl.BlockSpec((1, d), lambda i: (0, 0)),
                  pl.BlockSpec((1, 8, d), lambda i: (rows.mod_index(i, tm), 0, 0)),
                  pl.BlockSpec((d, 128), lambda i: (0, 0)),
                  pl.BlockSpec((1, 128), lambda i: (0, 0))],
        out_specs=[pl.BlockSpec((tm, d // 2), lambda i: (i, 0)),
                   pl.BlockSpec((tm, 128), lambda i: (i, 0))],
        out_shape=[jax.ShapeDtypeStruct((n_rows, d // 2), jnp.uint32),
                   jax.ShapeDtypeStruct((n_rows, 128), F32)],
        compiler_params=_cparams("parallel"),
        name="moe_route",
    )(z, norm_g.reshape(1, d), modtab, w_route, b_route)


def _rank_kernel(r_ref, rank_ref, cnt_ref, carry_scr):
    @pl.when(pl.program_id(0) == 0)
    def _():
        carry_scr[...] = jnp.zeros_like(carry_scr)

    r = r_ref[...]
    tm = r.shape[0]
    lane_i = lax.broadcasted_iota(jnp.int32, r.shape, 1)
    lane = lane_i.astype(F32)
    hot1 = lane == r[:, 0:1]
    hot2 = lane == r[:, 1:2]
    cnt = jnp.where(hot1 | hot2, 1.0, 0.0)
    row = lax.broadcasted_iota(jnp.int32, (tm, tm), 0)
    col = lax.broadcasted_iota(jnp.int32, (tm, tm), 1)
    before = jnp.where(col < row, 1.0, 0.0).astype(BF16)
    carry = carry_scr[0:1, :]
    prior = jnp.dot(before, cnt.astype(BF16), preferred_element_type=F32) + carry
    rank1 = jnp.sum(jnp.where(hot1, prior, 0.0), axis=-1, keepdims=True)
    rank2 = jnp.sum(jnp.where(hot2, prior, 0.0), axis=-1, keepdims=True)
    rank_ref[...] = jnp.where(lane_i == 0, rank1, jnp.where(lane_i == 1, rank2, 0.0))
    total = carry + jnp.sum(cnt, axis=0, keepdims=True)
    carry_scr[0:1, :] = total
    cnt_ref[...] = jnp.broadcast_to(total, cnt_ref.shape)


def _rank(r, *, tm):
    n = r.shape[0]
    return pl.pallas_call(
        _rank_kernel,
        grid=(n // tm,),
        in_specs=[pl.BlockSpec((tm, 128), lambda i: (i, 0))],
        out_specs=[pl.BlockSpec((tm, 128), lambda i: (i, 0)),
                   pl.BlockSpec((8, 128), lambda i: (0, 0))],
        out_shape=[jax.ShapeDtypeStruct((n, 128), F32), jax.ShapeDtypeStruct((8, 128), F32)],
        scratch_shapes=[pltpu.VMEM((8, 128), F32)],
        compiler_params=_cparams("arbitrary"),
        name="moe_rank",
    )(r)


def _expert_kernel(be_ref, fi_ref, nu_ref, x_ref, wg_ref, wu_ref, wd_ref, o_ref, wg_b, wu_b, wd_b):
    i = pl.program_id(0)

    @pl.when((i < nu_ref[0]) & (fi_ref[i] == 1))
    def _():
        wg_b[...] = wg_ref[...].astype(BF16)
        wu_b[...] = wu_ref[...].astype(BF16)
        wd_b[...] = wd_ref[...].astype(BF16)

    @pl.when(i < nu_ref[0])
    def _():
        x = _unpack_halves(x_ref[...])
        dot = lambda a, b: jnp.dot(a, b, preferred_element_type=F32)
        hid = _silu(dot(x, wg_b[...])) * dot(x, wu_b[...])
        o_ref[...] = dot(hid.astype(BF16), wd_b[...])

    @pl.when(i >= nu_ref[0])
    def _():
        o_ref[...] = jnp.zeros_like(o_ref)


def _expert_blocks(xs, blk_expert, first, n_used, w_gate, w_up, w_down, layer):
    n_rows, half = xs.shape
    d = 2 * half
    n_blocks = n_rows // MOE_BLOCK
    wspec = lambda a, b: pl.BlockSpec((None, None, a, b), lambda i, be, fi, nu: (layer, be[i], 0, 0))
    grid_spec = pltpu.PrefetchScalarGridSpec(
        num_scalar_prefetch=3,
        grid=(n_blocks,),
        in_specs=[pl.BlockSpec((MOE_BLOCK, half), lambda i, be, fi, nu: (i, 0)),
                  wspec(d, D_EXPERT), wspec(d, D_EXPERT), wspec(D_EXPERT, d)],
        out_specs=pl.BlockSpec((MOE_BLOCK, d), lambda i, be, fi, nu: (i, 0)),
        scratch_shapes=[pltpu.VMEM((d, D_EXPERT), BF16), pltpu.VMEM((d, D_EXPERT), BF16),
                        pltpu.VMEM((D_EXPERT, d), BF16)],
    )
    return pl.pallas_call(
        _expert_kernel,
        grid_spec=grid_spec,
        out_shape=jax.ShapeDtypeStruct((n_rows, d), F32),
        compiler_params=_cparams("arbitrary"),
        name="moe_experts",
    )(blk_expert, first, n_used, xs, w_gate, w_up, w_down)


def _hier_moe(rows, z, norm_g, modtab, w_grp, b_grp, w_exp, b_exp, w_gate, w_up, w_down, layer, *, n_rows):
    d = z.shape[1]
    n_logit = N_GROUPS + N_EXPERTS
    w_route = jnp.concatenate([w_grp, w_exp, jnp.zeros((d, 128 - n_logit), F32)], axis=1)
    b_route = jnp.concatenate([b_grp, b_exp, jnp.zeros((128 - n_logit,), F32)]).reshape(1, 128)
    h2, route = _route(rows, z, norm_g, modtab, w_route, b_route, n_rows=n_rows)
    rank, totals = _rank(route, tm=rows.tile_rows(512))
    eid = route[:, :TOP_K].astype(jnp.int32)
    wts = route[:, TOP_K:2 * TOP_K]
    counts = totals[0, :N_EXPERTS].astype(jnp.int32)
    padded = (counts + MOE_BLOCK - 1) // MOE_BLOCK * MOE_BLOCK
    pad_end = jnp.cumsum(padded)
    pad_start = pad_end - padded
    dest = pad_start[eid] + rank[:, :TOP_K].astype(jnp.int32)
    n_assign = n_rows * TOP_K
    n_blocks = (n_assign + N_EXPERTS * (MOE_BLOCK - 1) + MOE_BLOCK - 1) // MOE_BLOCK
    n_pad_rows = n_blocks * MOE_BLOCK
    tok = jnp.broadcast_to(jnp.arange(n_rows, dtype=jnp.int32)[:, None], (n_rows, TOP_K))
    filler = jnp.arange(n_pad_rows, dtype=jnp.int32) % n_rows
    row_tok = filler.at[dest.reshape(-1)].set(tok.reshape(-1))
    blk_start = jnp.arange(n_blocks, dtype=jnp.int32) * MOE_BLOCK
    blk_expert = jnp.minimum(jnp.sum((pad_end[None, :] <= blk_start[:, None]).astype(jnp.int32), axis=1),
                             N_EXPERTS - 1)
    first = jnp.concatenate([jnp.ones((1,), jnp.int32), (blk_expert[1:] != blk_expert[:-1]).astype(jnp.int32)])
    n_used = (pad_end[-1] // MOE_BLOCK).astype(jnp.int32).reshape(1)
    xs = _gather_rows(h2, row_tok)
    ys = _expert_blocks(xs, blk_expert, first, n_used, w_gate, w_up, w_down, layer)
    picked = _gather_rows(ys, dest.T.reshape(-1), window=16)
    return picked[:n_rows] * wts[:, 0:1] + picked[n_rows:] * wts[:, 1:2]


def kernel(x, c, ctx, c_ctx, mod_w, mod_b, norm1_g, norm2_g, w_in, dn_conv, dn_a_log, dn_dt_bias, dn_norm_g, dn_out, lru_conv, lru_conv_b, lru_wa, lru_ba, lru_wx, lru_bx, lru_lambda, lru_out, att_qn_g, att_kn_g, att_sink, att_out, w_o, moe_w_grp, moe_b_grp, moe_w_exp, moe_b_exp, moe_w_gate, moe_w_up, moe_w_down):
    batch, seq, d = x.shape
    n_ctx = ctx.shape[1]
    depth = mod_w.shape[0]
    rows = _Rows(batch, seq, n_ctx)
    assert d == D_MODEL and batch + 1 <= 8

    z = jnp.concatenate([x.reshape(batch * seq, d), ctx.reshape(batch * n_ctx, d)], axis=0)

    cc = jnp.concatenate([c_ctx[None], c, jnp.zeros((7 - batch, d), F32)], axis=0)
    mods = _mod_tables(cc, mod_w, mod_b).reshape(depth, 8, 6, d)
    mods = jnp.concatenate([mods, jnp.zeros((depth, 8, 2, d), F32)], axis=2)

    cos, sin = _rope_tables(seq, n_ctx)
    pc = _ProjCols
    w_proj = _relayout_w_in(w_in)

    for l in range(depth):
        last = l == depth - 1
        modtab = mods[l]
        n_out = rows.n_lat_rows if last else rows.n_rows

        p = _norm_matmul(rows, z, norm1_g[l], modtab, w_proj, layer=l, shift_row=0, scale_row=1, tn=pc.tile,
                         name="in_proj")

        qkv = _dn_prep(rows, p, dn_conv[l])
        par = jnp.zeros((2, 8, 128), F32)
        par = par.at[:, 0, :N_HEADS].set(dn_a_log[l]).at[:, 1, :N_HEADS].set(dn_dt_bias[l])
        o_f = _dn_scan(rows, qkv, p, par, small_col=pc.scalars, reverse=False)
        o_b = _dn_scan(rows, qkv, p, par, small_col=pc.scalars, reverse=True)

        lru_args = (lru_conv[l], lru_conv_b[l], lru_wa[l], lru_wx[l], lru_ba[l], lru_bx[l], lru_lambda[l])
        hs_f = _lru_scan(rows, p, *lru_args, x_col=pc.lx, reverse=False)
        hs_b = _lru_scan(rows, p, *lru_args, x_col=pc.lx, reverse=True)

        qn, kn = _qk_prep(rows, p, cos, sin, att_qn_g[l], att_kn_g[l], pc.aq, pc.ak)
        sink_b = jnp.broadcast_to(att_sink[l][:, None], (N_HEADS, 128))
        y_c = _attention(rows, qn, kn, p, pc.av, sink_b, with_ctx_queries=not last)

        merged = _merge(rows, o_f, o_b, p, hs_f, hs_b, p, y_c, dn_out[l].astype(BF16), lru_out[l].astype(BF16),
                        att_out[l].astype(BF16), dn_norm_g[l], z_col=pc.z, lg_col=pc.lg, gate_col=pc.gates,
                        n_rows=n_out)
        z = _mm_residual(rows, merged, w_o[l].astype(BF16), z, modtab, gate_row=2, n_rows=n_out)

        f = _hier_moe(rows, z, norm2_g[l], modtab, moe_w_grp[l], moe_b_grp[l], moe_w_exp[l], moe_b_exp[l],
                      moe_w_gate, moe_w_up, moe_w_down, l, n_rows=n_out)
        gate2 = jnp.concatenate([jnp.repeat(modtab[1:1 + batch, 5], seq, axis=0),
                                 jnp.broadcast_to(modtab[0, 5], (batch * n_ctx, d))], axis=0)[:n_out]
        z = z + gate2 * f

    return z[:rows.n_lat_rows].reshape(batch, seq, d)
```

```python
import functools

import jax
import jax.numpy as jnp
from jax import lax
from jax.experimental import pallas as pl
from jax.experimental.pallas import tpu as pltpu
from jax.experimental.pallas import tpu_sc as plsc

F32 = jnp.float32
BF16 = jnp.bfloat16

EPS = 1e-6
D_MODEL = 2048
N_HEADS = 8
HEAD_DIM = 128
MIX_W = N_HEADS * HEAD_DIM
KV_HEADS = 2
KV_W = KV_HEADS * HEAD_DIM
DN_CHUNK = 64
DN_UNIT = 128
CONV_W = 4
LRU_C = 8.0
ATT_WINDOW = 128
GRID_W = 64
ROPE_THETA = 10000.0
N_GROUPS = 8
EXPERTS_PER_GROUP = 8
N_EXPERTS = N_GROUPS * EXPERTS_PER_GROUP
TOP_K = 2
D_EXPERT = 512
MOE_BLOCK = 128
ROW_BLK = 256
NEG_BIG = -1e30
VMEM_LIMIT = 56 * 1024 * 1024


def _cparams(*sem):
    return pltpu.CompilerParams(dimension_semantics=sem, vmem_limit_bytes=VMEM_LIMIT)


def _bdot(a, b):
    return jnp.dot(a.astype(BF16), b.astype(BF16), preferred_element_type=F32)


def _bdot_nt(a, b):
    return lax.dot_general(a.astype(BF16), b.astype(BF16), (((1,), (1,)), ((), ())),
                           preferred_element_type=F32)


def _dot01(m01, x):
    m = m01.astype(BF16)
    x0 = x.astype(BF16)
    r1 = x - x0.astype(F32)
    x1 = r1.astype(BF16)
    x2 = (r1 - x1.astype(F32)).astype(BF16)
    dot = lambda t: jnp.dot(m, t, preferred_element_type=F32)
    return dot(x0) + dot(x1) + dot(x2)


def _silu(x):
    return x * jax.nn.sigmoid(x)


def _softplus(x):
    return jnp.maximum(x, 0.0) + jnp.log1p(jnp.exp(-jnp.abs(x)))


class _Rows:
    def __init__(self, batch, seq, n_ctx):
        assert seq % ROW_BLK == 0 and n_ctx % ROW_BLK == 0
        self.batch, self.seq, self.n_ctx = batch, seq, n_ctx
        self.nlat = seq // ROW_BLK
        self.nctx = n_ctx // ROW_BLK
        self.n_lat_rows = batch * seq
        self.n_rows = batch * (seq + n_ctx)
        self.steps = self.nlat + self.nctx

    def seq_block(self, b, t, reverse):
        if reverse:
            jc, jl = self.nctx - 1 - t, self.nlat - 1 - (t - self.nctx)
        else:
            jc, jl = t, t - self.nctx
        return jnp.where(t < self.nctx, self.batch * self.nlat + b * self.nctx + jc, b * self.nlat + jl)

    def mod_index(self, tile, tm):
        r0 = tile * tm
        return jnp.where(r0 < self.n_lat_rows, 1 + r0 // self.seq, 0)

    def tile_rows(self, cap):
        tm = cap
        while self.seq % tm or (self.batch * self.n_ctx) % tm:
            tm //= 2
        return tm


def _modtab_kernel(c_ref, w_ref, b_ref, o_ref):
    o_ref[0] = _bdot(_silu(c_ref[...]), w_ref[0]) + b_ref[0]


def _mod_tables(cc, mod_w, mod_b):
    depth, d, n = mod_w.shape
    tn = 1024
    return pl.pallas_call(
        _modtab_kernel,
        grid=(depth, n // tn),
        in_specs=[pl.BlockSpec((8, d), lambda l, j: (0, 0)),
                  pl.BlockSpec((1, d, tn), lambda l, j: (l, 0, j)),
                  pl.BlockSpec((1, 1, tn), lambda l, j: (l, 0, j))],
        out_specs=pl.BlockSpec((1, 8, tn), lambda l, j: (l, 0, j)),
        out_shape=jax.ShapeDtypeStruct((depth, 8, n), F32),
        compiler_params=_cparams("parallel", "parallel"),
        name="mod_tables",
    )(cc, mod_w, mod_b.reshape(depth, 1, n))


class _ProjCols:
    tile = 512
    small = 4 * N_HEADS
    qkvz = 4 * MIX_W
    rest = 3 * MIX_W + 2 * KV_W + 3 * D_MODEL
    z, lx, lg, aq = 3 * MIX_W, 4 * MIX_W, 5 * MIX_W, 6 * MIX_W
    ak = aq + MIX_W
    av = ak + KV_W
    gates = av + KV_W
    scalars = qkvz + rest
    total = scalars + 4 * 128


def _relayout_kernel(a_ref, b_ref, s_ref, o_ref, *, n_aligned, n_tiles, shift):
    j = pl.program_id(1)

    @pl.when(j < n_aligned)
    def _():
        o_ref[0] = a_ref[0].astype(BF16)

    @pl.when((j >= n_aligned) & (j < n_tiles - 1))
    def _():
        o_ref[0] = jnp.concatenate([a_ref[0][:, shift:], b_ref[0][:, :shift]], axis=1).astype(BF16)

    @pl.when(j == n_tiles - 1)
    def _():
        o_ref[0] = s_ref[0].astype(BF16)


def _relayout_w_in(w_in):
    depth, d, n_in = w_in.shape
    pc = _ProjCols
    assert n_in == pc.qkvz + pc.small + pc.rest and pc.qkvz % pc.tile == 0 and pc.rest % pc.tile == 0
    n_aligned = pc.qkvz // pc.tile
    n_tiles = pc.total // pc.tile
    per = pc.tile // 128
    w_small = w_in[:, :, pc.qkvz:pc.qkvz + pc.small].reshape(depth, d, 4, N_HEADS)
    w_small = jnp.pad(w_small, ((0, 0), (0, 0), (0, 0), (0, 128 - N_HEADS))).reshape(depth, d, 4 * 128)
    return pl.pallas_call(
        functools.partial(_relayout_kernel, n_aligned=n_aligned, n_tiles=n_tiles, shift=pc.small),
        grid=(depth, n_tiles),
        in_specs=[pl.BlockSpec((1, d, pc.tile), lambda l, j: (l, 0, jnp.minimum(j, n_tiles - 2))),
                  pl.BlockSpec((1, d, 128), lambda l, j: (l, 0, jnp.minimum(j + 1, n_tiles - 1) * per)),
                  pl.BlockSpec((1, d, pc.tile), lambda l, j: (l, 0, 0))],
        out_specs=pl.BlockSpec((1, d, pc.tile), lambda l, j: (l, 0, j)),
        out_shape=jax.ShapeDtypeStruct((depth, d, pc.total), BF16),
        compiler_params=_cparams("parallel", "parallel"),
        name="relayout_w_in",
    )(w_in, w_in, w_small)


def _norm_mm_kernel(z_ref, g_ref, mod_ref, w_ref, o_ref, *rest, shift_row, scale_row, emit_h):
    h_scr = rest[-1]
    tm = z_ref.shape[0]
    sub = min(tm, 128)

    @pl.when(pl.program_id(1) == 0)
    def _():
        g = g_ref[...]
        m = mod_ref[0]
        scale1 = 1.0 + m[scale_row:scale_row + 1]
        shift = m[shift_row:shift_row + 1]

        def body(i, carry):
            r0 = pl.multiple_of(i * sub, sub)
            z = z_ref[pl.ds(r0, sub), :]
            y = z * lax.rsqrt(jnp.mean(z * z, axis=-1, keepdims=True) + EPS) * g
            h_scr[pl.ds(r0, sub), :] = (y * scale1 + shift).astype(BF16)
            return carry

        lax.fori_loop(0, tm // sub, body, 0)

    h = h_scr[...]
    o_ref[...] = jnp.dot(h, w_ref[...].astype(BF16), preferred_element_type=F32)
    if emit_h:
        @pl.when(pl.program_id(1) == 0)
        def _():
            rest[0][...] = h


def _norm_matmul(rows, z, g, modtab, w, *, shift_row, scale_row, tn, layer=None, tm_cap=1024,
                 n_rows=None, emit_h=False, name="norm_matmul"):
    n_rows = rows.n_rows if n_rows is None else n_rows
    d = z.shape[1]
    n_cols = w.shape[-1]
    tm = rows.tile_rows(tm_cap)
    assert n_rows % tm == 0 and n_cols % tn == 0
    if layer is None:
        w_spec = pl.BlockSpec((d, tn), lambda i, j: (0, j))
    else:
        w_spec = pl.BlockSpec((None, d, tn), lambda i, j: (layer, 0, j))
    out_shape = [jax.ShapeDtypeStruct((n_rows, n_cols), F32)]
    out_specs = [pl.BlockSpec((tm, tn), lambda i, j: (i, j))]
    if emit_h:
        out_shape.append(jax.ShapeDtypeStruct((n_rows, d), BF16))
        out_specs.append(pl.BlockSpec((tm, d), lambda i, j: (i, 0)))
    res = pl.pallas_call(
        functools.partial(_norm_mm_kernel, shift_row=shift_row, scale_row=scale_row, emit_h=emit_h),
        grid=(n_rows // tm, n_cols // tn),
        in_specs=[pl.BlockSpec((tm, d), lambda i, j: (i, 0)),
                  pl.BlockSpec((1, d), lambda i, j: (0, 0)),
                  pl.BlockSpec((1, 8, d), lambda i, j: (rows.mod_index(i, tm), 0, 0)),
                  w_spec],
        out_specs=out_specs,
        out_shape=out_shape,
        scratch_shapes=[pltpu.VMEM((tm, d), BF16)],
        compiler_params=_cparams("parallel", "arbitrary"),
        name=name,
    )(z, g.reshape(1, d), modtab, w)
    return res if emit_h else res[0]


def _halo_flags(rows, blk):
    is_lat = blk < rows.batch * rows.nlat
    j = jnp.where(is_lat, blk % rows.nlat, (blk - rows.batch * rows.nlat) % rows.nctx)
    nseg = jnp.where(is_lat, rows.nlat, rows.nctx)
    return (j != 0).astype(F32), (j != nseg - 1).astype(F32)


def _conv4(prev8, cur, next8, w, use_prev, use_next):
    n = cur.shape[0]
    x = jnp.concatenate([prev8 * use_prev, cur, next8 * use_next], axis=0)
    tot = n + 16
    y = w[1:2] * cur
    y = y + w[0:1] * pltpu.roll(x, 1, 0)[8:8 + n]
    y = y + w[2:3] * pltpu.roll(x, tot - 1, 0)[8:8 + n]
    y = y + w[3:4] * pltpu.roll(x, tot - 2, 0)[8:8 + n]
    return y


def _halo_specs(rows, width, col_blk, blk_of):
    per = ROW_BLK // 8
    last8 = rows.n_rows // 8 - 1
    return [
        pl.BlockSpec((8, width), lambda *g: (jnp.maximum(blk_of(*g) * per - 1, 0), col_blk(*g))),
        pl.BlockSpec((ROW_BLK, width), lambda *g: (blk_of(*g), col_blk(*g))),
        pl.BlockSpec((8, width), lambda *g: (jnp.minimum((blk_of(*g) + 1) * per, last8), col_blk(*g))),
    ]


def _dn_prep_kernel(prev_ref, cur_ref, next_ref, w_ref, o_ref, *, rows):
    blk = pl.program_id(0)
    kind = pl.program_id(1)
    use_prev, use_next = _halo_flags(rows, blk)
    y = _silu(_conv4(prev_ref[...], cur_ref[...], next_ref[...], w_ref[...], use_prev, use_next))
    q_scale = jnp.where(kind == 0, HEAD_DIM ** -0.5, 1.0)
    for h in range(N_HEADS):
        sl = slice(h * HEAD_DIM, (h + 1) * HEAD_DIM)
        yh = y[:, sl]
        inv = lax.rsqrt(jnp.sum(yh * yh, axis=-1, keepdims=True) + EPS)
        o_ref[:, sl] = yh * (jnp.where(kind == 2, 1.0, inv) * q_scale)


def _dn_prep(rows, p_a, conv_w):
    return pl.pallas_call(
        functools.partial(_dn_prep_kernel, rows=rows),
        grid=(rows.n_rows // ROW_BLK, 3),
        in_specs=_halo_specs(rows, MIX_W, lambda i, j: j, lambda i, j: i)
        + [pl.BlockSpec((CONV_W, MIX_W), lambda i, j: (0, j))],
        out_specs=pl.BlockSpec((ROW_BLK, MIX_W), lambda i, j: (i, j)),
        out_shape=jax.ShapeDtypeStruct((rows.n_rows, 3 * MIX_W), F32),
        compiler_params=_cparams("parallel", "parallel"),
        name="dn_prep",
    )(p_a, p_a, p_a, conv_w)


def _dot_b(a, b):
    return jnp.dot(a, b, preferred_element_type=F32).astype(BF16)


def _dn_scan_kernel(q_ref, k_ref, v_ref, beta_ref, alpha_ref, par_ref, o_ref,
                    s_scr, u_scr, wq_scr, kdt_scr, aqk_scr, *, reverse):
    n = ROW_BLK
    nchunk = n // DN_CHUNK
    per_unit = DN_UNIT // DN_CHUNK

    @pl.when(pl.program_id(1) == 0)
    def _():
        s_scr[...] = jnp.zeros_like(s_scr)

    def tri(m):
        row = lax.broadcasted_iota(jnp.int32, (m, m), 0)
        col = lax.broadcasted_iota(jnp.int32, (m, m), 1)
        same = lambda s: (row >> (s.bit_length() - 1)) == (col >> (s.bit_length() - 1))
        ahead = (col >= row) if reverse else (col <= row)
        return row, col, same, same(DN_CHUNK) & ahead, same(DN_CHUNK) & ahead & (row != col)

    _, _, same_n, incl_n, _ = tri(n)
    par = par_ref[0]
    beta_all = jax.nn.sigmoid(beta_ref[...])
    g_all = -jnp.exp(par[0:1]) * _softplus(alpha_ref[...] + par[1:2])
    gc_all = _dot01(incl_n.astype(F32), g_all)
    gt_all = _dot01(same_n(DN_CHUNK).astype(F32), g_all)
    gc_t = gc_all.T
    egc_all = jnp.exp(gc_all)
    ekd_all = jnp.exp(gt_all - gc_all)
    egt_all = jnp.exp(gt_all)

    row_u, col_u, same_u, incl_u, strict_u = tri(DN_UNIT)
    one_b = lambda m: jnp.where(m, 1.0, 0.0).astype(BF16)
    eye_b = one_b(row_u == col_u)
    diag8_b = one_b(same_u(8))
    off_b = {s: one_b(same_u(2 * s) & jnp.logical_not(same_u(s))) for s in (8, 16, 32)}

    units = [(h, j) for h in range(N_HEADS) for j in range(n // DN_UNIT)]
    a_b, rhs_b = [], []
    for h, j in units:
        rs = slice(j * DN_UNIT, (j + 1) * DN_UNIT)
        sl = slice(h * HEAD_DIM, (h + 1) * HEAD_DIM)
        q, k, v = q_ref[rs, sl], k_ref[rs, sl], v_ref[rs, sl]
        bcol = beta_all[rs, h:h + 1]
        egc = egc_all[rs, h:h + 1]
        dec = jnp.exp(jnp.where(incl_u, gc_all[rs, h:h + 1] - gc_t[h:h + 1, rs], NEG_BIG))
        kb = k * bcol
        k_b = k.astype(BF16)
        a_b.append(jnp.where(strict_u, _bdot_nt(kb, k_b) * dec, 0.0).astype(BF16))
        aqk_scr[h, rs, :] = (_bdot_nt(q, k_b) * dec).astype(BF16)
        rhs_b.append(jnp.concatenate([v * bcol, kb * egc], axis=1).astype(BF16))
        qd = (q * egc).astype(BF16)
        kd = k * ekd_all[rs, h:h + 1]
        for c in range(per_unit):
            cs = slice(c * DN_CHUNK, (c + 1) * DN_CHUNK)
            wq_scr[h, j * per_unit + c, DN_CHUNK:, :] = qd[cs]
            kdt_scr[h, j * per_unit + c] = kd[cs].T.astype(BF16)

    d1 = [a * diag8_b for a in a_b]
    d2 = [_dot_b(d, d) for d in d1]
    d4 = [_dot_b(d, d) for d in d2]
    x = [_dot_b(eye_b - d, eye_b + e) for d, e in zip(d1, d2)]
    x = [_dot_b(xx, eye_b + e) for xx, e in zip(x, d4)]
    for s in (8, 16, 32):
        t = [_dot_b(a * off_b[s], xx) for a, xx in zip(a_b, x)]
        x = [xx - _dot_b(xx, tt) for xx, tt in zip(x, t)]
    for (h, j), xx, rhs in zip(units, x, rhs_b):
        uw = jnp.dot(xx, rhs, preferred_element_type=F32)
        u_scr[h, j * DN_UNIT:(j + 1) * DN_UNIT, :] = uw[:, :HEAD_DIM]
        for c in range(per_unit):
            wq_scr[h, j * per_unit + c, :DN_CHUNK, :] = uw[c * DN_CHUNK:(c + 1) * DN_CHUNK, HEAD_DIM:].astype(BF16)

    zeros_b = jnp.zeros((DN_CHUNK, HEAD_DIM), BF16)
    for ci in range(nchunk):
        c = nchunk - 1 - ci if reverse else ci
        rs = slice(c * DN_CHUNK, (c + 1) * DN_CHUNK)
        for h in range(N_HEADS):
            s_h = s_scr[h]
            r = jnp.dot(wq_scr[h, c], s_h.astype(BF16), preferred_element_type=F32)
            v_new = (u_scr[h, rs, :] - r[:DN_CHUNK]).astype(BF16)
            v_unit = jnp.concatenate([v_new, zeros_b] if c % per_unit == 0 else [zeros_b, v_new], axis=0)
            o = r[DN_CHUNK:] + jnp.dot(aqk_scr[h, rs, :], v_unit, preferred_element_type=F32)
            o_ref[rs, h * HEAD_DIM:(h + 1) * HEAD_DIM] = o
            s_scr[h] = s_h * egt_all[c * DN_CHUNK:c * DN_CHUNK + 1, h:h + 1] + jnp.dot(
                kdt_scr[h, c], v_new, preferred_element_type=F32)


def _dn_scan(rows, qkv, p, par, *, small_col, reverse):
    d = 1 if reverse else 0
    blk = lambda b, t: rows.seq_block(b, t, reverse)
    nchunk = ROW_BLK // DN_CHUNK
    sc = small_col // 128
    return pl.pallas_call(
        functools.partial(_dn_scan_kernel, reverse=reverse),
        grid=(rows.batch, rows.steps),
        in_specs=[pl.BlockSpec((ROW_BLK, MIX_W), lambda b, t: (blk(b, t), 0)),
                  pl.BlockSpec((ROW_BLK, MIX_W), lambda b, t: (blk(b, t), 1)),
                  pl.BlockSpec((ROW_BLK, MIX_W), lambda b, t: (blk(b, t), 2)),
                  pl.BlockSpec((ROW_BLK, 128), lambda b, t: (blk(b, t), sc + d)),
                  pl.BlockSpec((ROW_BLK, 128), lambda b, t: (blk(b, t), sc + 2 + d)),
                  pl.BlockSpec((1, 8, 128), lambda b, t: (d, 0, 0))],
        out_specs=pl.BlockSpec((ROW_BLK, MIX_W), lambda b, t: (blk(b, t), 0)),
        out_shape=jax.ShapeDtypeStruct((rows.n_rows, MIX_W), F32),
        scratch_shapes=[pltpu.VMEM((N_HEADS, HEAD_DIM, HEAD_DIM), F32),
                        pltpu.VMEM((N_HEADS, ROW_BLK, HEAD_DIM), F32),
                        pltpu.VMEM((N_HEADS, nchunk, 2 * DN_CHUNK, HEAD_DIM), BF16),
                        pltpu.VMEM((N_HEADS, nchunk, HEAD_DIM, DN_CHUNK), BF16),
                        pltpu.VMEM((N_HEADS, ROW_BLK, DN_UNIT), BF16)],
        compiler_params=_cparams("parallel", "arbitrary"),
        name="dn_scan_bwd" if reverse else "dn_scan_fwd",
    )(qkv, qkv, qkv, p, p, par)


def _lru_kernel(prev_ref, cur_ref, next_ref, cw_ref, cb_ref, wa_ref, wx_ref, ba_ref, bx_ref, lam_ref,
                o_ref, h_scr, a_scr, u_scr, *, rows, reverse):
    b, t = pl.program_id(0), pl.program_id(1)

    @pl.when(t == 0)
    def _():
        h_scr[...] = jnp.zeros_like(h_scr)

    use_prev, use_next = _halo_flags(rows, rows.seq_block(b, t, reverse))
    xc = _conv4(prev_ref[...], cur_ref[...], next_ref[...], cw_ref[...], use_prev, use_next) + cb_ref[...]
    for kb in range(N_HEADS):
        sl = slice(kb * HEAD_DIM, (kb + 1) * HEAD_DIM)
        xb = xc[:, sl]
        r = jax.nn.sigmoid(_bdot(xb, wa_ref[0, kb]) + ba_ref[0, :, sl])
        i = jax.nn.sigmoid(_bdot(xb, wx_ref[0, kb]) + bx_ref[0, :, sl])
        log_a = -LRU_C * r * _softplus(-lam_ref[0, :, sl])
        a_scr[:, sl] = jnp.exp(log_a)
        u_scr[:, sl] = jnp.sqrt(1.0 - jnp.exp(2.0 * log_a)) * (i * xb)

    ngrp = ROW_BLK // 8
    sub = lax.broadcasted_iota(jnp.int32, (8, MIX_W), 0)

    def group_step(gi, h_prev):
        g = ngrp - 1 - gi if reverse else gi
        r0 = pl.multiple_of(g * 8, 8)
        a = a_scr[pl.ds(r0, 8), :]
        u = u_scr[pl.ds(r0, 8), :]
        for dist in (1, 2, 4):
            if reverse:
                a_sh, u_sh, m = pltpu.roll(a, 8 - dist, 0), pltpu.roll(u, 8 - dist, 0), sub < 8 - dist
            else:
                a_sh, u_sh, m = pltpu.roll(a, dist, 0), pltpu.roll(u, dist, 0), sub >= dist
            u = jnp.where(m, a * u_sh + u, u)
            a = jnp.where(m, a * a_sh, a)
        h = u + a * h_prev
        o_ref[pl.ds(r0, 8), :] = h
        return h[0:1] if reverse else h[7:8]

    h_scr[...] = lax.fori_loop(0, ngrp, group_step, h_scr[...])


def _lru_scan(rows, p_b, cw, cb, wa, wx, ba, bx, lam, *, x_col, reverse):
    d = 1 if reverse else 0
    blk = lambda b, t: rows.seq_block(b, t, reverse)
    vec = lambda: pl.BlockSpec((1, 1, MIX_W), lambda b, t: (d, 0, 0))
    mat = lambda: pl.BlockSpec((1, N_HEADS, HEAD_DIM, HEAD_DIM), lambda b, t: (d, 0, 0, 0))
    return pl.pallas_call(
        functools.partial(_lru_kernel, rows=rows, reverse=reverse),
        grid=(rows.batch, rows.steps),
        in_specs=_halo_specs(rows, MIX_W, lambda b, t: x_col // MIX_W, blk)
        + [pl.BlockSpec((CONV_W, MIX_W), lambda b, t: (0, 0)),
           pl.BlockSpec((1, MIX_W), lambda b, t: (0, 0)),
           mat(), mat(), vec(), vec(), vec()],
        out_specs=pl.BlockSpec((ROW_BLK, MIX_W), lambda b, t: (blk(b, t), 0)),
        out_shape=jax.ShapeDtypeStruct((rows.n_rows, MIX_W), F32),
        scratch_shapes=[pltpu.VMEM((1, MIX_W), F32),
                        pltpu.VMEM((ROW_BLK, MIX_W), F32),
                        pltpu.VMEM((ROW_BLK, MIX_W), F32)],
        compiler_params=_cparams("parallel", "arbitrary"),
        name="lru_bwd" if reverse else "lru_fwd",
    )(p_b, p_b, p_b, cw, cb.reshape(1, MIX_W), wa, wx,
      ba.reshape(2, 1, MIX_W), bx.reshape(2, 1, MIX_W), lam.reshape(2, 1, MIX_W))


def _rope_tables(seq, n_ctx):
    half = HEAD_DIM // 2
    pos = jnp.arange(seq)
    inv = ROPE_THETA ** (-jnp.arange(0, half, 2, dtype=F32) / half)
    ang_r = (pos // GRID_W).astype(F32)[:, None] * inv
    ang_c = (pos % GRID_W).astype(F32)[:, None] * inv
    cos = jnp.concatenate([jnp.cos(ang_r)] * 2 + [jnp.cos(ang_c)] * 2, axis=-1)
    sin = jnp.concatenate([-jnp.sin(ang_r), jnp.sin(ang_r), -jnp.sin(ang_c), jnp.sin(ang_c)], axis=-1)
    cos = jnp.concatenate([cos, jnp.ones((n_ctx, HEAD_DIM), F32)], axis=0)
    sin = jnp.concatenate([sin, jnp.zeros((n_ctx, HEAD_DIM), F32)], axis=0)
    return cos, sin


def _qk_prep_kernel(q_ref, k_ref, cos_ref, sin_ref, qg_ref, kg_ref, qo_ref, ko_ref):
    cos, sin = cos_ref[...], sin_ref[...]
    lane = lax.broadcasted_iota(jnp.int32, cos.shape, 1)
    first = (lane & (HEAD_DIM // 2 - 1)) < (HEAD_DIM // 4)

    def norm_rope(x, g):
        y = x * lax.rsqrt(jnp.mean(x * x, axis=-1, keepdims=True) + EPS) * g
        partner = jnp.where(first, pltpu.roll(y, HEAD_DIM - HEAD_DIM // 4, 1), pltpu.roll(y, HEAD_DIM // 4, 1))
        return y * cos + partner * sin

    for h in range(N_HEADS):
        sl = slice(h * HEAD_DIM, (h + 1) * HEAD_DIM)
        qo_ref[:, sl] = (norm_rope(q_ref[:, sl], qg_ref[...]) * (HEAD_DIM ** -0.5)).astype(BF16)
    for h in range(KV_HEADS):
        sl = slice(h * HEAD_DIM, (h + 1) * HEAD_DIM)
        ko_ref[:, sl] = norm_rope(k_ref[:, sl], kg_ref[...]).astype(BF16)


def _qk_prep(rows, p_b, cos, sin, qn_g, kn_g, q_col, k_col):
    def tab(i):
        return jnp.where(i < rows.batch * rows.nlat, i % rows.nlat,
                         rows.nlat + (i - rows.batch * rows.nlat) % rows.nctx)

    return pl.pallas_call(
        _qk_prep_kernel,
        grid=(rows.n_rows // ROW_BLK,),
        in_specs=[pl.BlockSpec((ROW_BLK, MIX_W), lambda i: (i, q_col // MIX_W)),
                  pl.BlockSpec((ROW_BLK, KV_W), lambda i: (i, k_col // KV_W)),
                  pl.BlockSpec((ROW_BLK, HEAD_DIM), lambda i: (tab(i), 0)),
                  pl.BlockSpec((ROW_BLK, HEAD_DIM), lambda i: (tab(i), 0)),
                  pl.BlockSpec((1, HEAD_DIM), lambda i: (0, 0)),
                  pl.BlockSpec((1, HEAD_DIM), lambda i: (0, 0))],
        out_specs=[pl.BlockSpec((ROW_BLK, MIX_W), lambda i: (i, 0)),
                   pl.BlockSpec((ROW_BLK, KV_W), lambda i: (i, 0))],
        out_shape=[jax.ShapeDtypeStruct((rows.n_rows, MIX_W), BF16),
                   jax.ShapeDtypeStruct((rows.n_rows, KV_W), BF16)],
        compiler_params=_cparams("parallel"),
        name="qk_prep",
    )(p_b, p_b, cos, sin, qn_g.reshape(1, HEAD_DIM), kn_g.reshape(1, HEAD_DIM))


def _attn_core(q, keys, vals, sink_ref, valid, o_ref):
    grp = N_HEADS // KV_HEADS
    nq = q.shape[0]
    for kvh in range(KV_HEADS):
        sl = slice(kvh * HEAD_DIM, (kvh + 1) * HEAD_DIM)
        kk = jnp.concatenate([t[:, sl] for t in keys], axis=0)
        vv = jnp.concatenate([t[:, sl].astype(BF16) for t in vals], axis=0)
        q4 = jnp.concatenate([q[:, (kvh * grp + g) * HEAD_DIM:(kvh * grp + g + 1) * HEAD_DIM]
                              for g in range(grp)], axis=0)
        s = _bdot_nt(q4, kk)
        if valid is not None:
            s = jnp.where(valid, s, NEG_BIG)
        sink = jnp.concatenate([jnp.broadcast_to(sink_ref[kvh * grp + g:kvh * grp + g + 1, 0:1], (nq, 1))
                                for g in range(grp)], axis=0)
        m = jnp.maximum(jnp.max(s, axis=-1, keepdims=True), sink)
        e = jnp.exp(s - m)
        p = e / (jnp.sum(e, axis=-1, keepdims=True) + jnp.exp(sink - m))
        o = _bdot(p, vv)
        for g in range(grp):
            hq = kvh * grp + g
            o_ref[:, hq * HEAD_DIM:(hq + 1) * HEAD_DIM] = o[g * nq:(g + 1) * nq].astype(BF16)


def _attn_kernel(q_ref, k0_ref, k1_ref, k2_ref, kc_ref, v0_ref, v1_ref, v2_ref, vc_ref, sink_ref, o_ref,
                 *, seq, nb):
    n = pl.program_id(1)

    @pl.when(n < nb)
    def _():
        blk = q_ref.shape[0]
        n_loc = 3 * blk
        n_ctx = kc_ref.shape[0]
        grp = N_HEADS // KV_HEADS
        qi = lax.broadcasted_iota(jnp.int32, (grp * blk, n_loc + n_ctx), 0) & (blk - 1)
        kj = lax.broadcasted_iota(jnp.int32, (grp * blk, n_loc + n_ctx), 1)
        rel = kj - blk
        kpos = n * blk + rel
        local_ok = (jnp.abs(qi - rel) <= ATT_WINDOW) & (kpos >= 0) & (kpos < seq)
        valid = local_ok | (kj >= n_loc)
        _attn_core(q_ref[...], [k0_ref[...], k1_ref[...], k2_ref[...], kc_ref[...]],
                   [v0_ref[...], v1_ref[...], v2_ref[...], vc_ref[...]], sink_ref, valid, o_ref)

    @pl.when(n >= nb)
    def _():
        _attn_core(q_ref[...], [kc_ref[...]], [vc_ref[...]], sink_ref, None, o_ref)


def _attention(rows, qn, kn, p_b, v_col, sink_b, *, with_ctx_queries):
    blk = 128
    nb = rows.seq // blk
    vcb = v_col // KV_W
    ctx_blk0 = rows.n_lat_rows // rows.n_ctx
    assert rows.n_lat_rows % rows.n_ctx == 0
    kspec = lambda off: pl.BlockSpec((blk, KV_W), lambda b, n: (b * nb + jnp.clip(n + off, 0, nb - 1), 0))
    vspec = lambda off: pl.BlockSpec((blk, KV_W), lambda b, n: (b * nb + jnp.clip(n + off, 0, nb - 1), vcb))
    n_out = rows.n_rows if with_ctx_queries else rows.n_lat_rows
    ncb = rows.n_ctx // blk if with_ctx_queries else 0
    q0 = rows.n_lat_rows // blk
    qrow = lambda b, n: jnp.where(n < nb, b * nb + n, q0 + b * ncb + (n - nb))
    return pl.pallas_call(
        functools.partial(_attn_kernel, seq=rows.seq, nb=nb),
        grid=(rows.batch, nb + ncb),
        in_specs=[pl.BlockSpec((blk, MIX_W), lambda b, n: (qrow(b, n), 0)),
                  kspec(-1), kspec(0), kspec(1),
                  pl.BlockSpec((rows.n_ctx, KV_W), lambda b, n: (ctx_blk0 + b, 0)),
                  vspec(-1), vspec(0), vspec(1),
                  pl.BlockSpec((rows.n_ctx, KV_W), lambda b, n: (ctx_blk0 + b, vcb)),
                  pl.BlockSpec((8, 128), lambda b, n: (0, 0))],
        out_specs=pl.BlockSpec((blk, MIX_W), lambda b, n: (qrow(b, n), 0)),
        out_shape=jax.ShapeDtypeStruct((n_out, MIX_W), BF16),
        compiler_params=_cparams("parallel", "parallel"),
        name="attention",
    )(qn, kn, kn, kn, kn, p_b, p_b, p_b, p_b, sink_b)


def _merge_kernel(of_ref, ob_ref, z_ref, hf_ref, hb_ref, lg_ref, yc_ref, ga_ref, gb_ref, gc_ref,
                  wa_ref, wb_ref, wc_ref, ng_ref, o_ref, ya_scr, yb_scr):
    tm = of_ref.shape[0]
    sub = min(tm, 128)

    @pl.when(pl.program_id(1) == 0)
    def _():
        ng = ng_ref[...]

        def body(i, carry):
            r0 = pl.multiple_of(i * sub, sub)
            rs = pl.ds(r0, sub)
            for h in range(N_HEADS):
                sl = slice(h * HEAD_DIM, (h + 1) * HEAD_DIM)
                o = of_ref[rs, sl] + ob_ref[rs, sl]
                y = o * lax.rsqrt(jnp.mean(o * o, axis=-1, keepdims=True) + EPS) * ng
                ya_scr[rs, sl] = (y * _silu(z_ref[rs, sl])).astype(BF16)
            yb_scr[rs, :] = (jax.nn.gelu(lg_ref[rs, :]) * (hf_ref[rs, :] + hb_ref[rs, :])).astype(BF16)
            return carry

        lax.fori_loop(0, tm // sub, body, 0)

    acc = jax.nn.sigmoid(ga_ref[...]) * _bdot(ya_scr[...], wa_ref[...])
    acc = acc + jax.nn.sigmoid(gb_ref[...]) * _bdot(yb_scr[...], wb_ref[...])
    acc = acc + jax.nn.sigmoid(gc_ref[...]) * _bdot(yc_ref[...], wc_ref[...])
    o_ref[...] = acc.astype(BF16)


def _merge(rows, o_f, o_b, p_a, hs_f, hs_b, p_b, y_c, dn_out, lru_out, att_out, dn_norm_g, *,
           z_col, lg_col, gate_col, n_rows):
    tm = rows.tile_rows(512)
    tn = 512
    assert n_rows % tm == 0 and gate_col % tn == 0
    row = lambda cb: pl.BlockSpec((tm, MIX_W), lambda i, j: (i, cb))
    gate = lambda br: pl.BlockSpec((tm, tn), lambda i, j: (i, (gate_col + br * D_MODEL) // tn + j))
    wgt = lambda: pl.BlockSpec((MIX_W, tn), lambda i, j: (0, j))
    return pl.pallas_call(
        _merge_kernel,
        grid=(n_rows // tm, D_MODEL // tn),
        in_specs=[row(0), row(0), row(z_col // MIX_W), row(0), row(0), row(lg_col // MIX_W), row(0),
                  gate(0), gate(1), gate(2), wgt(), wgt(), wgt(),
                  pl.BlockSpec((1, HEAD_DIM), lambda i, j: (0, 0))],
        out_specs=pl.BlockSpec((tm, tn), lambda i, j: (i, j)),
        out_shape=jax.ShapeDtypeStruct((n_rows, D_MODEL), BF16),
        scratch_shapes=[pltpu.VMEM((tm, MIX_W), BF16), pltpu.VMEM((tm, MIX_W), BF16)],
        compiler_params=_cparams("parallel", "arbitrary"),
        name="merge",
    )(o_f, o_b, p_a, hs_f, hs_b, p_b, y_c, p_b, p_b, p_b, dn_out, lru_out, att_out,
      dn_norm_g.reshape(1, HEAD_DIM))


def _mm_residual_kernel(a_ref, w_ref, z_ref, mod_ref, o_ref, *, gate_row):
    acc = _bdot(a_ref[...], w_ref[...])
    o_ref[...] = z_ref[...] + mod_ref[0][gate_row:gate_row + 1] * acc


def _mm_residual(rows, a, w, z, modtab, *, gate_row, n_rows):
    tm = rows.tile_rows(1024)
    tn = 512
    k, n = w.shape
    assert n_rows % tm == 0
    return pl.pallas_call(
        functools.partial(_mm_residual_kernel, gate_row=gate_row),
        grid=(n_rows // tm, n // tn),
        in_specs=[pl.BlockSpec((tm, k), lambda i, j: (i, 0)),
                  pl.BlockSpec((k, tn), lambda i, j: (0, j)),
                  pl.BlockSpec((tm, tn), lambda i, j: (i, j)),
                  pl.BlockSpec((1, 8, tn), lambda i, j: (rows.mod_index(i, tm), 0, j))],
        out_specs=pl.BlockSpec((tm, tn), lambda i, j: (i, j)),
        out_shape=jax.ShapeDtypeStruct((n_rows, n), F32),
        compiler_params=_cparams("parallel", "parallel"),
        name="out_proj",
    )(a, w, z, modtab)


def _pack_halves(h):
    w = h.shape[1] // 2
    lo = lax.bitcast_convert_type(h[:, :w].astype(F32), jnp.uint32) >> 16
    hi = lax.bitcast_convert_type(h[:, w:].astype(F32), jnp.uint32) & jnp.uint32(0xFFFF0000)
    return lo | hi


def _unpack_halves(p):
    lo = lax.bitcast_convert_type(p << 16, F32).astype(BF16)
    hi = lax.bitcast_convert_type(p & jnp.uint32(0xFFFF0000), F32).astype(BF16)
    return jnp.concatenate([lo, hi], axis=1)


def _gather_rows(x, idx, *, window=32):
    m, w = idx.shape[0], x.shape[1]
    chunk = 128
    assert m % chunk == 0 and chunk % window == 0
    n_chunks = m // chunk
    mesh = plsc.VectorSubcoreMesh(core_axis_name="core", subcore_axis_name="subcore")
    n_workers = mesh.num_cores * mesh.num_subcores

    @pl.kernel(out_type=jax.ShapeDtypeStruct((m, w), x.dtype), mesh=mesh,
               scratch_types=[pltpu.VMEM((chunk,), jnp.int32), pltpu.VMEM((window, w), x.dtype)])
    def gather_kernel(x_hbm, i_hbm, o_hbm, idx_v, buf):
        wid = lax.axis_index("core") * mesh.num_subcores + lax.axis_index("subcore")

        def step(c, carry):
            r0 = (wid + c * n_workers) * chunk
            pltpu.sync_copy(i_hbm.at[pl.ds(r0, chunk)], idx_v)
            for k in range(chunk // window):
                pltpu.sync_copy(x_hbm.at[idx_v.at[pl.ds(k * window, window)]], buf)
                pltpu.sync_copy(buf, o_hbm.at[pl.ds(r0 + k * window, window)])
            return carry

        lax.fori_loop(0, (n_chunks - wid + n_workers - 1) // n_workers, step, 0)

    return gather_kernel(x, idx)


def _route_kernel(z_ref, g_ref, mod_ref, w_ref, b_ref, h_ref, r_ref):
    m = mod_ref[0]
    z = z_ref[...]
    y = z * lax.rsqrt(jnp.mean(z * z, axis=-1, keepdims=True) + EPS) * g_ref[...]
    h = (y * (1.0 + m[4:5]) + m[3:4]).astype(BF16)
    h_ref[...] = _pack_halves(h)
    logits = jnp.dot(h, w_ref[...].astype(BF16), preferred_element_type=F32) + b_ref[...]
    lane_i = lax.broadcasted_iota(jnp.int32, logits.shape, 1)
    lane = lane_i.astype(F32)
    far = float(2 * 128)

    def top(vals):
        best = jnp.max(vals, axis=-1, keepdims=True)
        return best, jnp.min(jnp.where(vals == best, lane, far), axis=-1, keepdims=True)

    is_grp = lane_i < N_GROUPS
    g_max, g_idx = top(jnp.where(is_grp, logits, NEG_BIG))
    p_grp = 1.0 / jnp.sum(jnp.where(is_grp, jnp.exp(logits - g_max), 0.0), axis=-1, keepdims=True)
    e_lane = lane_i - N_GROUPS
    in_grp = (e_lane >= 0) & ((e_lane >> 3).astype(F32) == g_idx)
    cand = jnp.where(in_grp, logits, NEG_BIG)
    t1, i1 = top(cand)
    t2, i2 = top(jnp.where(lane == i1, NEG_BIG, cand))
    e2 = jnp.exp(t2 - t1)
    w1 = p_grp / (1.0 + e2)
    w2 = w1 * e2
    r = jnp.where(lane_i == 0, i1 - N_GROUPS, jnp.where(lane_i == 1, i2 - N_GROUPS,
                  jnp.where(lane_i == 2, w1, jnp.where(lane_i == 3, w2, 0.0))))
    r_ref[...] = r


def _route(rows, z, norm_g, modtab, w_route, b_route, *, n_rows):
    d = z.shape[1]
    tm = rows.tile_rows(256)
    return pl.pallas_call(
        _route_kernel,
        grid=(n_rows // tm,),
        in_specs=[pl.BlockSpec((tm, d), lambda i: (i, 0)),
                  pl.BlockSpec((1, d), lambda i: (0, 0)),
                  pl.BlockSpec((1, 8, d), lambda i: (rows.mod_index(i, tm), 0, 0)),
                  pl.BlockSpec((d, 128), lambda i: (0, 0)),
                  pl.BlockSpec((1, 128), lambda i: (0, 0))],
        out_specs=[pl.BlockSpec((tm, d // 2), lambda i: (i, 0)),
                   pl.BlockSpec((tm, 128), lambda i: (i, 0))],
        out_shape=[jax.ShapeDtypeStruct((n_rows, d // 2), jnp.uint32),
                   jax.ShapeDtypeStruct((n_rows, 128), F32)],
        compiler_params=_cparams("parallel"),
        name="moe_route",
    )(z, norm_g.reshape(1, d), modtab, w_route, b_route)


def _rank_kernel(r_ref, rank_ref, cnt_ref, carry_scr):
    @pl.when(pl.program_id(0) == 0)
    def _():
        carry_scr[...] = jnp.zeros_like(carry_scr)

    r = r_ref[...]
    tm = r.shape[0]
    lane_i = lax.broadcasted_iota(jnp.int32, r.shape, 1)
    lane = lane_i.astype(F32)
    hot1 = lane == r[:, 0:1]
    hot2 = lane == r[:, 1:2]
    cnt = jnp.where(hot1 | hot2, 1.0, 0.0)
    row = lax.broadcasted_iota(jnp.int32, (tm, tm), 0)
    col = lax.broadcasted_iota(jnp.int32, (tm, tm), 1)
    before = jnp.where(col < row, 1.0, 0.0).astype(BF16)
    carry = carry_scr[0:1, :]
    prior = jnp.dot(before, cnt.astype(BF16), preferred_element_type=F32) + carry
    rank1 = jnp.sum(jnp.where(hot1, prior, 0.0), axis=-1, keepdims=True)
    rank2 = jnp.sum(jnp.where(hot2, prior, 0.0), axis=-1, keepdims=True)
    rank_ref[...] = jnp.where(lane_i == 0, rank1, jnp.where(lane_i == 1, rank2, 0.0))
    total = carry + jnp.sum(cnt, axis=0, keepdims=True)
    carry_scr[0:1, :] = total
    cnt_ref[...] = jnp.broadcast_to(total, cnt_ref.shape)


def _rank(r, *, tm):
    n = r.shape[0]
    return pl.pallas_call(
        _rank_kernel,
        grid=(n // tm,),
        in_specs=[pl.BlockSpec((tm, 128), lambda i: (i, 0))],
        out_specs=[pl.BlockSpec((tm, 128), lambda i: (i, 0)),
                   pl.BlockSpec((8, 128), lambda i: (0, 0))],
        out_shape=[jax.ShapeDtypeStruct((n, 128), F32), jax.ShapeDtypeStruct((8, 128), F32)],
        scratch_shapes=[pltpu.VMEM((8, 128), F32)],
        compiler_params=_cparams("arbitrary"),
        name="moe_rank",
    )(r)


def _expert_kernel(be_ref, fi_ref, nu_ref, x_ref, wg_ref, wu_ref, wd_ref, o_ref, wg_b, wu_b, wd_b):
    i = pl.program_id(0)

    @pl.when((i < nu_ref[0]) & (fi_ref[i] == 1))
    def _():
        wg_b[...] = wg_ref[...].astype(BF16)
        wu_b[...] = wu_ref[...].astype(BF16)
        wd_b[...] = wd_ref[...].astype(BF16)

    @pl.when(i < nu_ref[0])
    def _():
        x = _unpack_halves(x_ref[...])
        dot = lambda a, b: jnp.dot(a, b, preferred_element_type=F32)
        hid = _silu(dot(x, wg_b[...])) * dot(x, wu_b[...])
        o_ref[...] = dot(hid.astype(BF16), wd_b[...])

    @pl.when(i >= nu_ref[0])
    def _():
        o_ref[...] = jnp.zeros_like(o_ref)


def _expert_blocks(xs, blk_expert, first, n_used, w_gate, w_up, w_down, layer):
    n_rows, half = xs.shape
    d = 2 * half
    n_blocks = n_rows // MOE_BLOCK
    wspec = lambda a, b: pl.BlockSpec((None, None, a, b), lambda i, be, fi, nu: (layer, be[i], 0, 0))
    grid_spec = pltpu.PrefetchScalarGridSpec(
        num_scalar_prefetch=3,
        grid=(n_blocks,),
        in_specs=[pl.BlockSpec((MOE_BLOCK, half), lambda i, be, fi, nu: (i, 0)),
                  wspec(d, D_EXPERT), wspec(d, D_EXPERT), wspec(D_EXPERT, d)],
        out_specs=pl.BlockSpec((MOE_BLOCK, d), lambda i, be, fi, nu: (i, 0)),
        scratch_shapes=[pltpu.VMEM((d, D_EXPERT), BF16), pltpu.VMEM((d, D_EXPERT), BF16),
                        pltpu.VMEM((D_EXPERT, d), BF16)],
    )
    return pl.pallas_call(
        _expert_kernel,
        grid_spec=grid_spec,
        out_shape=jax.ShapeDtypeStruct((n_rows, d), F32),
        compiler_params=_cparams("arbitrary"),
        name="moe_experts",
    )(blk_expert, first, n_used, xs, w_gate, w_up, w_down)


def _hier_moe(rows, z, norm_g, modtab, w_grp, b_grp, w_exp, b_exp, w_gate, w_up, w_down, layer, *, n_rows):
    d = z.shape[1]
    n_logit = N_GROUPS + N_EXPERTS
    w_route = jnp.concatenate([w_grp, w_exp, jnp.zeros((d, 128 - n_logit), F32)], axis=1)
    b_route = jnp.concatenate([b_grp, b_exp, jnp.zeros((128 - n_logit,), F32)]).reshape(1, 128)
    h2, route = _route(rows, z, norm_g, modtab, w_route, b_route, n_rows=n_rows)
    rank, totals = _rank(route, tm=rows.tile_rows(512))
    eid = route[:, :TOP_K].astype(jnp.int32)
    wts = route[:, TOP_K:2 * TOP_K]
    counts = totals[0, :N_EXPERTS].astype(jnp.int32)
    padded = (counts + MOE_BLOCK - 1) // MOE_BLOCK * MOE_BLOCK
    pad_end = jnp.cumsum(padded)
    pad_start = pad_end - padded
    dest = pad_start[eid] + rank[:, :TOP_K].astype(jnp.int32)
    n_assign = n_rows * TOP_K
    n_blocks = (n_assign + N_EXPERTS * (MOE_BLOCK - 1) + MOE_BLOCK - 1) // MOE_BLOCK
    n_pad_rows = n_blocks * MOE_BLOCK
    tok = jnp.broadcast_to(jnp.arange(n_rows, dtype=jnp.int32)[:, None], (n_rows, TOP_K))
    filler = jnp.arange(n_pad_rows, dtype=jnp.int32) % n_rows
    row_tok = filler.at[dest.reshape(-1)].set(tok.reshape(-1))
    blk_start = jnp.arange(n_blocks, dtype=jnp.int32) * MOE_BLOCK
    blk_expert = jnp.minimum(jnp.sum((pad_end[None, :] <= blk_start[:, None]).astype(jnp.int32), axis=1),
                             N_EXPERTS - 1)
    first = jnp.concatenate([jnp.ones((1,), jnp.int32), (blk_expert[1:] != blk_expert[:-1]).astype(jnp.int32)])
    n_used = (pad_end[-1] // MOE_BLOCK).astype(jnp.int32).reshape(1)
    xs = _gather_rows(h2, row_tok, window=64)
    ys = _expert_blocks(xs, blk_expert, first, n_used, w_gate, w_up, w_down, layer)
    picked = _gather_rows(ys, dest.T.reshape(-1), window=32)
    return picked[:n_rows] * wts[:, 0:1] + picked[n_rows:] * wts[:, 1:2]


def kernel(x, c, ctx, c_ctx, mod_w, mod_b, norm1_g, norm2_g, w_in, dn_conv, dn_a_log, dn_dt_bias, dn_norm_g, dn_out, lru_conv, lru_conv_b, lru_wa, lru_ba, lru_wx, lru_bx, lru_lambda, lru_out, att_qn_g, att_kn_g, att_sink, att_out, w_o, moe_w_grp, moe_b_grp, moe_w_exp, moe_b_exp, moe_w_gate, moe_w_up, moe_w_down):
    batch, seq, d = x.shape
    n_ctx = ctx.shape[1]
    depth = mod_w.shape[0]
    rows = _Rows(batch, seq, n_ctx)
    assert d == D_MODEL and batch + 1 <= 8

    z = jnp.concatenate([x.reshape(batch * seq, d), ctx.reshape(batch * n_ctx, d)], axis=0)

    cc = jnp.concatenate([c_ctx[None], c, jnp.zeros((7 - batch, d), F32)], axis=0)
    mods = _mod_tables(cc, mod_w, mod_b).reshape(depth, 8, 6, d)
    mods = jnp.concatenate([mods, jnp.zeros((depth, 8, 2, d), F32)], axis=2)

    cos, sin = _rope_tables(seq, n_ctx)
    pc = _ProjCols
    w_proj = _relayout_w_in(w_in)

    for l in range(depth):
        last = l == depth - 1
        modtab = mods[l]
        n_out = rows.n_lat_rows if last else rows.n_rows

        p = _norm_matmul(rows, z, norm1_g[l], modtab, w_proj, layer=l, shift_row=0, scale_row=1, tn=pc.tile,
                         name="in_proj")

        qkv = _dn_prep(rows, p, dn_conv[l])
        par = jnp.zeros((2, 8, 128), F32)
        par = par.at[:, 0, :N_HEADS].set(dn_a_log[l]).at[:, 1, :N_HEADS].set(dn_dt_bias[l])
        o_f = _dn_scan(rows, qkv, p, par, small_col=pc.scalars, reverse=False)
        o_b = _dn_scan(rows, qkv, p, par, small_col=pc.scalars, reverse=True)

        lru_args = (lru_conv[l], lru_conv_b[l], lru_wa[l], lru_wx[l], lru_ba[l], lru_bx[l], lru_lambda[l])
        hs_f = _lru_scan(rows, p, *lru_args, x_col=pc.lx, reverse=False)
        hs_b = _lru_scan(rows, p, *lru_args, x_col=pc.lx, reverse=True)

        qn, kn = _qk_prep(rows, p, cos, sin, att_qn_g[l], att_kn_g[l], pc.aq, pc.ak)
        sink_b = jnp.broadcast_to(att_sink[l][:, None], (N_HEADS, 128))
        y_c = _attention(rows, qn, kn, p, pc.av, sink_b, with_ctx_queries=not last)

        merged = _merge(rows, o_f, o_b, p, hs_f, hs_b, p, y_c, dn_out[l].astype(BF16), lru_out[l].astype(BF16),
                        att_out[l].astype(BF16), dn_norm_g[l], z_col=pc.z, lg_col=pc.lg, gate_col=pc.gates,
                        n_rows=n_out)
        z = _mm_residual(rows, merged, w_o[l].astype(BF16), z, modtab, gate_row=2, n_rows=n_out)

        f = _hier_moe(rows, z, norm2_g[l], modtab, moe_w_grp[l], moe_b_grp[l], moe_w_exp[l], moe_b_exp[l],
                      moe_w_gate, moe_w_up, moe_w_down, l, n_rows=n_out)
        gate2 = jnp.concatenate([jnp.repeat(modtab[1:1 + batch, 5], seq, axis=0),
                                 jnp.broadcast_to(modtab[0, 5], (batch * n_ctx, d))], axis=0)[:n_out]
        z = z + gate2 * f

    return z[:rows.n_lat_rows].reshape(batch, seq, d)
```

```python
import functools

import jax
import jax.numpy as jnp
from jax import lax
from jax.experimental import pallas as pl
from jax.experimental.pallas import tpu as pltpu
from jax.experimental.pallas import tpu_sc as plsc

F32 = jnp.float32
BF16 = jnp.bfloat16

EPS = 1e-6
D_MODEL = 2048
N_HEADS = 8
HEAD_DIM = 128
MIX_W = N_HEADS * HEAD_DIM
KV_HEADS = 2
KV_W = KV_HEADS * HEAD_DIM
DN_CHUNK = 64
DN_UNIT = 128
CONV_W = 4
LRU_C = 8.0
ATT_WINDOW = 128
GRID_W = 64
ROPE_THETA = 10000.0
N_GROUPS = 8
EXPERTS_PER_GROUP = 8
N_EXPERTS = N_GROUPS * EXPERTS_PER_GROUP
TOP_K = 2
D_EXPERT = 512
MOE_BLOCK = 128
ROW_BLK = 256
NEG_BIG = -1e30
VMEM_LIMIT = 56 * 1024 * 1024


def _cparams(*sem):
    return pltpu.CompilerParams(dimension_semantics=sem, vmem_limit_bytes=VMEM_LIMIT)


def _bdot(a, b):
    return jnp.dot(a.astype(BF16), b.astype(BF16), preferred_element_type=F32)


def _bdot_nt(a, b):
    return lax.dot_general(a.astype(BF16), b.astype(BF16), (((1,), (1,)), ((), ())),
                           preferred_element_type=F32)


def _dot01(m01, x):
    m = m01.astype(BF16)
    x0 = x.astype(BF16)
    r1 = x - x0.astype(F32)
    x1 = r1.astype(BF16)
    x2 = (r1 - x1.astype(F32)).astype(BF16)
    dot = lambda t: jnp.dot(m, t, preferred_element_type=F32)
    return dot(x0) + dot(x1) + dot(x2)


def _silu(x):
    return x * jax.nn.sigmoid(x)


def _softplus(x):
    return jnp.maximum(x, 0.0) + jnp.log1p(jnp.exp(-jnp.abs(x)))


class _Rows:
    def __init__(self, batch, seq, n_ctx):
        assert seq % ROW_BLK == 0 and n_ctx % ROW_BLK == 0
        self.batch, self.seq, self.n_ctx = batch, seq, n_ctx
        self.nlat = seq // ROW_BLK
        self.nctx = n_ctx // ROW_BLK
        self.n_lat_rows = batch * seq
        self.n_rows = batch * (seq + n_ctx)
        self.steps = self.nlat + self.nctx

    def seq_block(self, b, t, reverse):
        if reverse:
            jc, jl = self.nctx - 1 - t, self.nlat - 1 - (t - self.nctx)
        else:
            jc, jl = t, t - self.nctx
        return jnp.where(t < self.nctx, self.batch * self.nlat + b * self.nctx + jc, b * self.nlat + jl)

    def mod_index(self, tile, tm):
        r0 = tile * tm
        return jnp.where(r0 < self.n_lat_rows, 1 + r0 // self.seq, 0)

    def tile_rows(self, cap):
        tm = cap
        while self.seq % tm or (self.batch * self.n_ctx) % tm:
            tm //= 2
        return tm


def _modtab_kernel(c_ref, w_ref, b_ref, o_ref):
    o_ref[0] = _bdot(_silu(c_ref[...]), w_ref[0]) + b_ref[0]


def _mod_tables(cc, mod_w, mod_b):
    depth, d, n = mod_w.shape
    tn = 1024
    return pl.pallas_call(
        _modtab_kernel,
        grid=(depth, n // tn),
        in_specs=[pl.BlockSpec((8, d), lambda l, j: (0, 0)),
                  pl.BlockSpec((1, d, tn), lambda l, j: (l, 0, j)),
                  pl.BlockSpec((1, 1, tn), lambda l, j: (l, 0, j))],
        out_specs=pl.BlockSpec((1, 8, tn), lambda l, j: (l, 0, j)),
        out_shape=jax.ShapeDtypeStruct((depth, 8, n), F32),
        compiler_params=_cparams("parallel", "parallel"),
        name="mod_tables",
    )(cc, mod_w, mod_b.reshape(depth, 1, n))


class _ProjCols:
    tile = 512
    small = 4 * N_HEADS
    qkvz = 4 * MIX_W
    rest = 3 * MIX_W + 2 * KV_W + 3 * D_MODEL
    z, lx, lg, aq = 3 * MIX_W, 4 * MIX_W, 5 * MIX_W, 6 * MIX_W
    ak = aq + MIX_W
    av = ak + KV_W
    gates = av + KV_W
    scalars = qkvz + rest
    total = scalars + 4 * 128


def _relayout_kernel(a_ref, b_ref, s_ref, o_ref, *, n_aligned, n_tiles, shift):
    j = pl.program_id(1)

    @pl.when(j < n_aligned)
    def _():
        o_ref[0] = a_ref[0].astype(BF16)

    @pl.when((j >= n_aligned) & (j < n_tiles - 1))
    def _():
        o_ref[0] = jnp.concatenate([a_ref[0][:, shift:], b_ref[0][:, :shift]], axis=1).astype(BF16)

    @pl.when(j == n_tiles - 1)
    def _():
        o_ref[0] = s_ref[0].astype(BF16)


def _relayout_w_in(w_in):
    depth, d, n_in = w_in.shape
    pc = _ProjCols
    assert n_in == pc.qkvz + pc.small + pc.rest and pc.qkvz % pc.tile == 0 and pc.rest % pc.tile == 0
    n_aligned = pc.qkvz // pc.tile
    n_tiles = pc.total // pc.tile
    per = pc.tile // 128
    w_small = w_in[:, :, pc.qkvz:pc.qkvz + pc.small].reshape(depth, d, 4, N_HEADS)
    w_small = jnp.pad(w_small, ((0, 0), (0, 0), (0, 0), (0, 128 - N_HEADS))).reshape(depth, d, 4 * 128)
    return pl.pallas_call(
        functools.partial(_relayout_kernel, n_aligned=n_aligned, n_tiles=n_tiles, shift=pc.small),
        grid=(depth, n_tiles),
        in_specs=[pl.BlockSpec((1, d, pc.tile), lambda l, j: (l, 0, jnp.minimum(j, n_tiles - 2))),
                  pl.BlockSpec((1, d, 128), lambda l, j: (l, 0, jnp.minimum(j + 1, n_tiles - 1) * per)),
                  pl.BlockSpec((1, d, pc.tile), lambda l, j: (l, 0, 0))],
        out_specs=pl.BlockSpec((1, d, pc.tile), lambda l, j: (l, 0, j)),
        out_shape=jax.ShapeDtypeStruct((depth, d, pc.total), BF16),
        compiler_params=_cparams("parallel", "parallel"),
        name="relayout_w_in",
    )(w_in, w_in, w_small)


def _norm_mm_kernel(z_ref, g_ref, mod_ref, w_ref, o_ref, *rest, shift_row, scale_row, emit_h):
    h_scr = rest[-1]
    tm = z_ref.shape[0]
    sub = min(tm, 128)

    @pl.when(pl.program_id(1) == 0)
    def _():
        g = g_ref[...]
        m = mod_ref[0]
        scale1 = 1.0 + m[scale_row:scale_row + 1]
        shift = m[shift_row:shift_row + 1]

        def body(i, carry):
            r0 = pl.multiple_of(i * sub, sub)
            z = z_ref[pl.ds(r0, sub), :]
            y = z * lax.rsqrt(jnp.mean(z * z, axis=-1, keepdims=True) + EPS) * g
            h_scr[pl.ds(r0, sub), :] = (y * scale1 + shift).astype(BF16)
            return carry

        lax.fori_loop(0, tm // sub, body, 0)

    h = h_scr[...]
    o_ref[...] = jnp.dot(h, w_ref[...].astype(BF16), preferred_element_type=F32)
    if emit_h:
        @pl.when(pl.program_id(1) == 0)
        def _():
            rest[0][...] = h


def _norm_matmul(rows, z, g, modtab, w, *, shift_row, scale_row, tn, layer=None, tm_cap=1024,
                 n_rows=None, emit_h=False, name="norm_matmul"):
    n_rows = rows.n_rows if n_rows is None else n_rows
    d = z.shape[1]
    n_cols = w.shape[-1]
    tm = rows.tile_rows(tm_cap)
    assert n_rows % tm == 0 and n_cols % tn == 0
    if layer is None:
        w_spec = pl.BlockSpec((d, tn), lambda i, j: (0, j))
    else:
        w_spec = pl.BlockSpec((None, d, tn), lambda i, j: (layer, 0, j))
    out_shape = [jax.ShapeDtypeStruct((n_rows, n_cols), F32)]
    out_specs = [pl.BlockSpec((tm, tn), lambda i, j: (i, j))]
    if emit_h:
        out_shape.append(jax.ShapeDtypeStruct((n_rows, d), BF16))
        out_specs.append(pl.BlockSpec((tm, d), lambda i, j: (i, 0)))
    res = pl.pallas_call(
        functools.partial(_norm_mm_kernel, shift_row=shift_row, scale_row=scale_row, emit_h=emit_h),
        grid=(n_rows // tm, n_cols // tn),
        in_specs=[pl.BlockSpec((tm, d), lambda i, j: (i, 0)),
                  pl.BlockSpec((1, d), lambda i, j: (0, 0)),
                  pl.BlockSpec((1, 8, d), lambda i, j: (rows.mod_index(i, tm), 0, 0)),
                  w_spec],
        out_specs=out_specs,
        out_shape=out_shape,
        scratch_shapes=[pltpu.VMEM((tm, d), BF16)],
        compiler_params=_cparams("parallel", "arbitrary"),
        name=name,
    )(z, g.reshape(1, d), modtab, w)
    return res if emit_h else res[0]


def _halo_flags(rows, blk):
    is_lat = blk < rows.batch * rows.nlat
    j = jnp.where(is_lat, blk % rows.nlat, (blk - rows.batch * rows.nlat) % rows.nctx)
    nseg = jnp.where(is_lat, rows.nlat, rows.nctx)
    return (j != 0).astype(F32), (j != nseg - 1).astype(F32)


def _conv4(prev8, cur, next8, w, use_prev, use_next):
    n = cur.shape[0]
    x = jnp.concatenate([prev8 * use_prev, cur, next8 * use_next], axis=0)
    tot = n + 16
    y = w[1:2] * cur
    y = y + w[0:1] * pltpu.roll(x, 1, 0)[8:8 + n]
    y = y + w[2:3] * pltpu.roll(x, tot - 1, 0)[8:8 + n]
    y = y + w[3:4] * pltpu.roll(x, tot - 2, 0)[8:8 + n]
    return y


def _halo_specs(rows, width, col_blk, blk_of):
    per = ROW_BLK // 8
    last8 = rows.n_rows // 8 - 1
    return [
        pl.BlockSpec((8, width), lambda *g: (jnp.maximum(blk_of(*g) * per - 1, 0), col_blk(*g))),
        pl.BlockSpec((ROW_BLK, width), lambda *g: (blk_of(*g), col_blk(*g))),
        pl.BlockSpec((8, width), lambda *g: (jnp.minimum((blk_of(*g) + 1) * per, last8), col_blk(*g))),
    ]


def _dn_prep_kernel(prev_ref, cur_ref, next_ref, w_ref, o_ref, *, rows):
    blk = pl.program_id(0)
    kind = pl.program_id(1)
    use_prev, use_next = _halo_flags(rows, blk)
    y = _silu(_conv4(prev_ref[...], cur_ref[...], next_ref[...], w_ref[...], use_prev, use_next))
    q_scale = jnp.where(kind == 0, HEAD_DIM ** -0.5, 1.0)
    for h in range(N_HEADS):
        sl = slice(h * HEAD_DIM, (h + 1) * HEAD_DIM)
        yh = y[:, sl]
        inv = lax.rsqrt(jnp.sum(yh * yh, axis=-1, keepdims=True) + EPS)
        o_ref[:, sl] = yh * (jnp.where(kind == 2, 1.0, inv) * q_scale)


def _dn_prep(rows, p_a, conv_w):
    return pl.pallas_call(
        functools.partial(_dn_prep_kernel, rows=rows),
        grid=(rows.n_rows // ROW_BLK, 3),
        in_specs=_halo_specs(rows, MIX_W, lambda i, j: j, lambda i, j: i)
        + [pl.BlockSpec((CONV_W, MIX_W), lambda i, j: (0, j))],
        out_specs=pl.BlockSpec((ROW_BLK, MIX_W), lambda i, j: (i, j)),
        out_shape=jax.ShapeDtypeStruct((rows.n_rows, 3 * MIX_W), F32),
        compiler_params=_cparams("parallel", "parallel"),
        name="dn_prep",
    )(p_a, p_a, p_a, conv_w)


def _dot_b(a, b):
    return jnp.dot(a, b, preferred_element_type=F32).astype(BF16)


def _dn_scan_kernel(q_ref, k_ref, v_ref, beta_ref, alpha_ref, par_ref, o_ref,
                    s_scr, u_scr, wq_scr, kdt_scr, aqk_scr, *, reverse):
    n = ROW_BLK
    nchunk = n // DN_CHUNK
    per_unit = DN_UNIT // DN_CHUNK

    @pl.when(pl.program_id(1) == 0)
    def _():
        s_scr[...] = jnp.zeros_like(s_scr)

    def tri(m):
        row = lax.broadcasted_iota(jnp.int32, (m, m), 0)
        col = lax.broadcasted_iota(jnp.int32, (m, m), 1)
        same = lambda s: (row >> (s.bit_length() - 1)) == (col >> (s.bit_length() - 1))
        ahead = (col >= row) if reverse else (col <= row)
        return row, col, same, same(DN_CHUNK) & ahead, same(DN_CHUNK) & ahead & (row != col)

    _, _, same_n, incl_n, _ = tri(n)
    par = par_ref[0]
    beta_all = jax.nn.sigmoid(beta_ref[...])
    g_all = -jnp.exp(par[0:1]) * _softplus(alpha_ref[...] + par[1:2])
    gc_all = _dot01(incl_n.astype(F32), g_all)
    gt_all = _dot01(same_n(DN_CHUNK).astype(F32), g_all)
    gc_t = gc_all.T
    egc_all = jnp.exp(gc_all)
    ekd_all = jnp.exp(gt_all - gc_all)
    egt_all = jnp.exp(gt_all)

    row_u, col_u, same_u, incl_u, strict_u = tri(DN_UNIT)
    one_b = lambda m: jnp.where(m, 1.0, 0.0).astype(BF16)
    eye_b = one_b(row_u == col_u)
    diag8_b = one_b(same_u(8))
    off_b = {s: one_b(same_u(2 * s) & jnp.logical_not(same_u(s))) for s in (8, 16, 32)}

    units = [(h, j) for h in range(N_HEADS) for j in range(n // DN_UNIT)]
    a_b, rhs_b = [], []
    for h, j in units:
        rs = slice(j * DN_UNIT, (j + 1) * DN_UNIT)
        sl = slice(h * HEAD_DIM, (h + 1) * HEAD_DIM)
        q, k, v = q_ref[rs, sl], k_ref[rs, sl], v_ref[rs, sl]
        bcol = beta_all[rs, h:h + 1]
        egc = egc_all[rs, h:h + 1]
        dec = jnp.exp(jnp.where(incl_u, gc_all[rs, h:h + 1] - gc_t[h:h + 1, rs], NEG_BIG))
        kb = k * bcol
        k_b = k.astype(BF16)
        a_b.append(jnp.where(strict_u, _bdot_nt(kb, k_b) * dec, 0.0).astype(BF16))
        aqk_scr[h, rs, :] = (_bdot_nt(q, k_b) * dec).astype(BF16)
        rhs_b.append(jnp.concatenate([v * bcol, kb * egc], axis=1).astype(BF16))
        qd = (q * egc).astype(BF16)
        kd = k * ekd_all[rs, h:h + 1]
        for c in range(per_unit):
            cs = slice(c * DN_CHUNK, (c + 1) * DN_CHUNK)
            wq_scr[h, j * per_unit + c, DN_CHUNK:, :] = qd[cs]
            kdt_scr[h, j * per_unit + c] = kd[cs].T.astype(BF16)

    d1 = [a * diag8_b for a in a_b]
    d2 = [_dot_b(d, d) for d in d1]
    d4 = [_dot_b(d, d) for d in d2]
    x = [_dot_b(eye_b - d, eye_b + e) for d, e in zip(d1, d2)]
    x = [_dot_b(xx, eye_b + e) for xx, e in zip(x, d4)]
    for s in (8, 16, 32):
        t = [_dot_b(a * off_b[s], xx) for a, xx in zip(a_b, x)]
        x = [xx - _dot_b(xx, tt) for xx, tt in zip(x, t)]
    for (h, j), xx, rhs in zip(units, x, rhs_b):
        uw = jnp.dot(xx, rhs, preferred_element_type=F32)
        u_scr[h, j * DN_UNIT:(j + 1) * DN_UNIT, :] = uw[:, :HEAD_DIM]
        for c in range(per_unit):
            wq_scr[h, j * per_unit + c, :DN_CHUNK, :] = uw[c * DN_CHUNK:(c + 1) * DN_CHUNK, HEAD_DIM:].astype(BF16)

    zeros_b = jnp.zeros((DN_CHUNK, HEAD_DIM), BF16)
    for ci in range(nchunk):
        c = nchunk - 1 - ci if reverse else ci
        rs = slice(c * DN_CHUNK, (c + 1) * DN_CHUNK)
        for h in range(N_HEADS):
            s_h = s_scr[h]
            r = jnp.dot(wq_scr[h, c], s_h.astype(BF16), preferred_element_type=F32)
            v_new = (u_scr[h, rs, :] - r[:DN_CHUNK]).astype(BF16)
            v_unit = jnp.concatenate([v_new, zeros_b] if c % per_unit == 0 else [zeros_b, v_new], axis=0)
            o = r[DN_CHUNK:] + jnp.dot(aqk_scr[h, rs, :], v_unit, preferred_element_type=F32)
            o_ref[rs, h * HEAD_DIM:(h + 1) * HEAD_DIM] = o
            s_scr[h] = s_h * egt_all[c * DN_CHUNK:c * DN_CHUNK + 1, h:h + 1] + jnp.dot(
                kdt_scr[h, c], v_new, preferred_element_type=F32)


def _dn_scan(rows, qkv, p, par, *, small_col, reverse):
    d = 1 if reverse else 0
    blk = lambda b, t: rows.seq_block(b, t, reverse)
    nchunk = ROW_BLK // DN_CHUNK
    sc = small_col // 128
    return pl.pallas_call(
        functools.partial(_dn_scan_kernel, reverse=reverse),
        grid=(rows.batch, rows.steps),
        in_specs=[pl.BlockSpec((ROW_BLK, MIX_W), lambda b, t: (blk(b, t), 0)),
                  pl.BlockSpec((ROW_BLK, MIX_W), lambda b, t: (blk(b, t), 1)),
                  pl.BlockSpec((ROW_BLK, MIX_W), lambda b, t: (blk(b, t), 2)),
                  pl.BlockSpec((ROW_BLK, 128), lambda b, t: (blk(b, t), sc + d)),
                  pl.BlockSpec((ROW_BLK, 128), lambda b, t: (blk(b, t), sc + 2 + d)),
                  pl.BlockSpec((1, 8, 128), lambda b, t: (d, 0, 0))],
        out_specs=pl.BlockSpec((ROW_BLK, MIX_W), lambda b, t: (blk(b, t), 0)),
        out_shape=jax.ShapeDtypeStruct((rows.n_rows, MIX_W), F32),
        scratch_shapes=[pltpu.VMEM((N_HEADS, HEAD_DIM, HEAD_DIM), F32),
                        pltpu.VMEM((N_HEADS, ROW_BLK, HEAD_DIM), F32),
                        pltpu.VMEM((N_HEADS, nchunk, 2 * DN_CHUNK, HEAD_DIM), BF16),
                        pltpu.VMEM((N_HEADS, nchunk, HEAD_DIM, DN_CHUNK), BF16),
                        pltpu.VMEM((N_HEADS, ROW_BLK, DN_UNIT), BF16)],
        compiler_params=_cparams("parallel", "arbitrary"),
        name="dn_scan_bwd" if reverse else "dn_scan_fwd",
    )(qkv, qkv, qkv, p, p, par)


def _lru_kernel(prev_ref, cur_ref, next_ref, cw_ref, cb_ref, wa_ref, wx_ref, ba_ref, bx_ref, lam_ref,
                o_ref, h_scr, a_scr, u_scr, *, rows, reverse):
    b, t = pl.program_id(0), pl.program_id(1)

    @pl.when(t == 0)
    def _():
        h_scr[...] = jnp.zeros_like(h_scr)

    use_prev, use_next = _halo_flags(rows, rows.seq_block(b, t, reverse))
    xc = _conv4(prev_ref[...], cur_ref[...], next_ref[...], cw_ref[...], use_prev, use_next) + cb_ref[...]
    for kb in range(N_HEADS):
        sl = slice(kb * HEAD_DIM, (kb + 1) * HEAD_DIM)
        xb = xc[:, sl]
        r = jax.nn.sigmoid(_bdot(xb, wa_ref[0, kb]) + ba_ref[0, :, sl])
        i = jax.nn.sigmoid(_bdot(xb, wx_ref[0, kb]) + bx_ref[0, :, sl])
        log_a = -LRU_C * r * _softplus(-lam_ref[0, :, sl])
        a_scr[:, sl] = jnp.exp(log_a)
        u_scr[:, sl] = jnp.sqrt(1.0 - jnp.exp(2.0 * log_a)) * (i * xb)

    ngrp = ROW_BLK // 8
    sub = lax.broadcasted_iota(jnp.int32, (8, MIX_W), 0)

    def group_step(gi, h_prev):
        g = ngrp - 1 - gi if reverse else gi
        r0 = pl.multiple_of(g * 8, 8)
        a = a_scr[pl.ds(r0, 8), :]
        u = u_scr[pl.ds(r0, 8), :]
        for dist in (1, 2, 4):
            if reverse:
                a_sh, u_sh, m = pltpu.roll(a, 8 - dist, 0), pltpu.roll(u, 8 - dist, 0), sub < 8 - dist
            else:
                a_sh, u_sh, m = pltpu.roll(a, dist, 0), pltpu.roll(u, dist, 0), sub >= dist
            u = jnp.where(m, a * u_sh + u, u)
            a = jnp.where(m, a * a_sh, a)
        h = u + a * h_prev
        o_ref[pl.ds(r0, 8), :] = h
        return h[0:1] if reverse else h[7:8]

    h_scr[...] = lax.fori_loop(0, ngrp, group_step, h_scr[...])


def _lru_scan(rows, p_b, cw, cb, wa, wx, ba, bx, lam, *, x_col, reverse):
    d = 1 if reverse else 0
    blk = lambda b, t: rows.seq_block(b, t, reverse)
    vec = lambda: pl.BlockSpec((1, 1, MIX_W), lambda b, t: (d, 0, 0))
    mat = lambda: pl.BlockSpec((1, N_HEADS, HEAD_DIM, HEAD_DIM), lambda b, t: (d, 0, 0, 0))
    return pl.pallas_call(
        functools.partial(_lru_kernel, rows=rows, reverse=reverse),
        grid=(rows.batch, rows.steps),
        in_specs=_halo_specs(rows, MIX_W, lambda b, t: x_col // MIX_W, blk)
        + [pl.BlockSpec((CONV_W, MIX_W), lambda b, t: (0, 0)),
           pl.BlockSpec((1, MIX_W), lambda b, t: (0, 0)),
           mat(), mat(), vec(), vec(), vec()],
        out_specs=pl.BlockSpec((ROW_BLK, MIX_W), lambda b, t: (blk(b, t), 0)),
        out_shape=jax.ShapeDtypeStruct((rows.n_rows, MIX_W), F32),
        scratch_shapes=[pltpu.VMEM((1, MIX_W), F32),
                        pltpu.VMEM((ROW_BLK, MIX_W), F32),
                        pltpu.VMEM((ROW_BLK, MIX_W), F32)],
        compiler_params=_cparams("parallel", "arbitrary"),
        name="lru_bwd" if reverse else "lru_fwd",
    )(p_b, p_b, p_b, cw, cb.reshape(1, MIX_W), wa, wx,
      ba.reshape(2, 1, MIX_W), bx.reshape(2, 1, MIX_W), lam.reshape(2, 1, MIX_W))


def _rope_tables(seq, n_ctx):
    half = HEAD_DIM // 2
    pos = jnp.arange(seq)
    inv = ROPE_THETA ** (-jnp.arange(0, half, 2, dtype=F32) / half)
    ang_r = (pos // GRID_W).astype(F32)[:, None] * inv
    ang_c = (pos % GRID_W).astype(F32)[:, None] * inv
    cos = jnp.concatenate([jnp.cos(ang_r)] * 2 + [jnp.cos(ang_c)] * 2, axis=-1)
    sin = jnp.concatenate([-jnp.sin(ang_r), jnp.sin(ang_r), -jnp.sin(ang_c), jnp.sin(ang_c)], axis=-1)
    cos = jnp.concatenate([cos, jnp.ones((n_ctx, HEAD_DIM), F32)], axis=0)
    sin = jnp.concatenate([sin, jnp.zeros((n_ctx, HEAD_DIM), F32)], axis=0)
    return cos, sin


def _qk_prep_kernel(q_ref, k_ref, cos_ref, sin_ref, qg_ref, kg_ref, qo_ref, ko_ref):
    cos, sin = cos_ref[...], sin_ref[...]
    lane = lax.broadcasted_iota(jnp.int32, cos.shape, 1)
    first = (lane & (HEAD_DIM // 2 - 1)) < (HEAD_DIM // 4)

    def norm_rope(x, g):
        y = x * lax.rsqrt(jnp.mean(x * x, axis=-1, keepdims=True) + EPS) * g
        partner = jnp.where(first, pltpu.roll(y, HEAD_DIM - HEAD_DIM // 4, 1), pltpu.roll(y, HEAD_DIM // 4, 1))
        return y * cos + partner * sin

    for h in range(N_HEADS):
        sl = slice(h * HEAD_DIM, (h + 1) * HEAD_DIM)
        qo_ref[:, sl] = (norm_rope(q_ref[:, sl], qg_ref[...]) * (HEAD_DIM ** -0.5)).astype(BF16)
    for h in range(KV_HEADS):
        sl = slice(h * HEAD_DIM, (h + 1) * HEAD_DIM)
        ko_ref[:, sl] = norm_rope(k_ref[:, sl], kg_ref[...]).astype(BF16)


def _qk_prep(rows, p_b, cos, sin, qn_g, kn_g, q_col, k_col):
    def tab(i):
        return jnp.where(i < rows.batch * rows.nlat, i % rows.nlat,
                         rows.nlat + (i - rows.batch * rows.nlat) % rows.nctx)

    return pl.pallas_call(
        _qk_prep_kernel,
        grid=(rows.n_rows // ROW_BLK,),
        in_specs=[pl.BlockSpec((ROW_BLK, MIX_W), lambda i: (i, q_col // MIX_W)),
                  pl.BlockSpec((ROW_BLK, KV_W), lambda i: (i, k_col // KV_W)),
                  pl.BlockSpec((ROW_BLK, HEAD_DIM), lambda i: (tab(i), 0)),
                  pl.BlockSpec((ROW_BLK, HEAD_DIM), lambda i: (tab(i), 0)),
                  pl.BlockSpec((1, HEAD_DIM), lambda i: (0, 0)),
                  pl.BlockSpec((1, HEAD_DIM), lambda i: (0, 0))],
        out_specs=[pl.BlockSpec((ROW_BLK, MIX_W), lambda i: (i, 0)),
                   pl.BlockSpec((ROW_BLK, KV_W), lambda i: (i, 0))],
        out_shape=[jax.ShapeDtypeStruct((rows.n_rows, MIX_W), BF16),
                   jax.ShapeDtypeStruct((rows.n_rows, KV_W), BF16)],
        compiler_params=_cparams("parallel"),
        name="qk_prep",
    )(p_b, p_b, cos, sin, qn_g.reshape(1, HEAD_DIM), kn_g.reshape(1, HEAD_DIM))


def _attn_core(q, keys, vals, sink_ref, valid, o_ref):
    grp = N_HEADS // KV_HEADS
    nq = q.shape[0]
    for kvh in range(KV_HEADS):
        sl = slice(kvh * HEAD_DIM, (kvh + 1) * HEAD_DIM)
        kk = jnp.concatenate([t[:, sl] for t in keys], axis=0)
        vv = jnp.concatenate([t[:, sl].astype(BF16) for t in vals], axis=0)
        q4 = jnp.concatenate([q[:, (kvh * grp + g) * HEAD_DIM:(kvh * grp + g + 1) * HEAD_DIM]
                              for g in range(grp)], axis=0)
        s = _bdot_nt(q4, kk)
        if valid is not None:
            s = jnp.where(valid, s, NEG_BIG)
        sink = jnp.concatenate([jnp.broadcast_to(sink_ref[kvh * grp + g:kvh * grp + g + 1, 0:1], (nq, 1))
                                for g in range(grp)], axis=0)
        m = jnp.maximum(jnp.max(s, axis=-1, keepdims=True), sink)
        e = jnp.exp(s - m)
        p = e / (jnp.sum(e, axis=-1, keepdims=True) + jnp.exp(sink - m))
        o = _bdot(p, vv)
        for g in range(grp):
            hq = kvh * grp + g
            o_ref[:, hq * HEAD_DIM:(hq + 1) * HEAD_DIM] = o[g * nq:(g + 1) * nq].astype(BF16)


def _attn_kernel(q_ref, k0_ref, k1_ref, k2_ref, kc_ref, v0_ref, v1_ref, v2_ref, vc_ref, sink_ref, o_ref,
                 *, seq, nb):
    n = pl.program_id(1)

    @pl.when(n < nb)
    def _():
        blk = q_ref.shape[0]
        n_loc = 3 * blk
        n_ctx = kc_ref.shape[0]
        grp = N_HEADS // KV_HEADS
        qi = lax.broadcasted_iota(jnp.int32, (grp * blk, n_loc + n_ctx), 0) & (blk - 1)
        kj = lax.broadcasted_iota(jnp.int32, (grp * blk, n_loc + n_ctx), 1)
        rel = kj - blk
        kpos = n * blk + rel
        local_ok = (jnp.abs(qi - rel) <= ATT_WINDOW) & (kpos >= 0) & (kpos < seq)
        valid = local_ok | (kj >= n_loc)
        _attn_core(q_ref[...], [k0_ref[...], k1_ref[...], k2_ref[...], kc_ref[...]],
                   [v0_ref[...], v1_ref[...], v2_ref[...], vc_ref[...]], sink_ref, valid, o_ref)

    @pl.when(n >= nb)
    def _():
        _attn_core(q_ref[...], [kc_ref[...]], [vc_ref[...]], sink_ref, None, o_ref)


def _attention(rows, qn, kn, p_b, v_col, sink_b, *, with_ctx_queries):
    blk = 128
    nb = rows.seq // blk
    vcb = v_col // KV_W
    ctx_blk0 = rows.n_lat_rows // rows.n_ctx
    assert rows.n_lat_rows % rows.n_ctx == 0
    kspec = lambda off: pl.BlockSpec((blk, KV_W), lambda b, n: (b * nb + jnp.clip(n + off, 0, nb - 1), 0))
    vspec = lambda off: pl.BlockSpec((blk, KV_W), lambda b, n: (b * nb + jnp.clip(n + off, 0, nb - 1), vcb))
    n_out = rows.n_rows if with_ctx_queries else rows.n_lat_rows
    ncb = rows.n_ctx // blk if with_ctx_queries else 0
    q0 = rows.n_lat_rows // blk
    qrow = lambda b, n: jnp.where(n < nb, b * nb + n, q0 + b * ncb + (n - nb))
    return pl.pallas_call(
        functools.partial(_attn_kernel, seq=rows.seq, nb=nb),
        grid=(rows.batch, nb + ncb),
        in_specs=[pl.BlockSpec((blk, MIX_W), lambda b, n: (qrow(b, n), 0)),
                  kspec(-1), kspec(0), kspec(1),
                  pl.BlockSpec((rows.n_ctx, KV_W), lambda b, n: (ctx_blk0 + b, 0)),
                  vspec(-1), vspec(0), vspec(1),
                  pl.BlockSpec((rows.n_ctx, KV_W), lambda b, n: (ctx_blk0 + b, vcb)),
                  pl.BlockSpec((8, 128), lambda b, n: (0, 0))],
        out_specs=pl.BlockSpec((blk, MIX_W), lambda b, n: (qrow(b, n), 0)),
        out_shape=jax.ShapeDtypeStruct((n_out, MIX_W), BF16),
        compiler_params=_cparams("parallel", "parallel"),
        name="attention",
    )(qn, kn, kn, kn, kn, p_b, p_b, p_b, p_b, sink_b)


def _merge_kernel(of_ref, ob_ref, z_ref, hf_ref, hb_ref, lg_ref, yc_ref, ga_ref, gb_ref, gc_ref,
                  wa_ref, wb_ref, wc_ref, ng_ref, o_ref, ya_scr, yb_scr):
    tm = of_ref.shape[0]
    sub = min(tm, 128)

    @pl.when(pl.program_id(1) == 0)
    def _():
        ng = ng_ref[...]

        def body(i, carry):
            r0 = pl.multiple_of(i * sub, sub)
            rs = pl.ds(r0, sub)
            for h in range(N_HEADS):
                sl = slice(h * HEAD_DIM, (h + 1) * HEAD_DIM)
                o = of_ref[rs, sl] + ob_ref[rs, sl]
                y = o * lax.rsqrt(jnp.mean(o * o, axis=-1, keepdims=True) + EPS) * ng
                ya_scr[rs, sl] = (y * _silu(z_ref[rs, sl])).astype(BF16)
            yb_scr[rs, :] = (jax.nn.gelu(lg_ref[rs, :]) * (hf_ref[rs, :] + hb_ref[rs, :])).astype(BF16)
            return carry

        lax.fori_loop(0, tm // sub, body, 0)

    acc = jax.nn.sigmoid(ga_ref[...]) * _bdot(ya_scr[...], wa_ref[...])
    acc = acc + jax.nn.sigmoid(gb_ref[...]) * _bdot(yb_scr[...], wb_ref[...])
    acc = acc + jax.nn.sigmoid(gc_ref[...]) * _bdot(yc_ref[...], wc_ref[...])
    o_ref[...] = acc.astype(BF16)


def _merge(rows, o_f, o_b, p_a, hs_f, hs_b, p_b, y_c, dn_out, lru_out, att_out, dn_norm_g, *,
           z_col, lg_col, gate_col, n_rows):
    tm = rows.tile_rows(512)
    tn = 512
    assert n_rows % tm == 0 and gate_col % tn == 0
    row = lambda cb: pl.BlockSpec((tm, MIX_W), lambda i, j: (i, cb))
    gate = lambda br: pl.BlockSpec((tm, tn), lambda i, j: (i, (gate_col + br * D_MODEL) // tn + j))
    wgt = lambda: pl.BlockSpec((MIX_W, tn), lambda i, j: (0, j))
    return pl.pallas_call(
        _merge_kernel,
        grid=(n_rows // tm, D_MODEL // tn),
        in_specs=[row(0), row(0), row(z_col // MIX_W), row(0), row(0), row(lg_col // MIX_W), row(0),
                  gate(0), gate(1), gate(2), wgt(), wgt(), wgt(),
                  pl.BlockSpec((1, HEAD_DIM), lambda i, j: (0, 0))],
        out_specs=pl.BlockSpec((tm, tn), lambda i, j: (i, j)),
        out_shape=jax.ShapeDtypeStruct((n_rows, D_MODEL), BF16),
        scratch_shapes=[pltpu.VMEM((tm, MIX_W), BF16), pltpu.VMEM((tm, MIX_W), BF16)],
        compiler_params=_cparams("parallel", "arbitrary"),
        name="merge",
    )(o_f, o_b, p_a, hs_f, hs_b, p_b, y_c, p_b, p_b, p_b, dn_out, lru_out, att_out,
      dn_norm_g.reshape(1, HEAD_DIM))


def _mm_residual_kernel(a_ref, w_ref, z_ref, mod_ref, o_ref, *, gate_row):
    acc = _bdot(a_ref[...], w_ref[...])
    o_ref[...] = z_ref[...] + mod_ref[0][gate_row:gate_row + 1] * acc


def _mm_residual(rows, a, w, z, modtab, *, gate_row, n_rows):
    tm = rows.tile_rows(1024)
    tn = 512
    k, n = w.shape
    assert n_rows % tm == 0
    return pl.pallas_call(
        functools.partial(_mm_residual_kernel, gate_row=gate_row),
        grid=(n_rows // tm, n // tn),
        in_specs=[pl.BlockSpec((tm, k), lambda i, j: (i, 0)),
                  pl.BlockSpec((k, tn), lambda i, j: (0, j)),
                  pl.BlockSpec((tm, tn), lambda i, j: (i, j)),
                  pl.BlockSpec((1, 8, tn), lambda i, j: (rows.mod_index(i, tm), 0, j))],
        out_specs=pl.BlockSpec((tm, tn), lambda i, j: (i, j)),
        out_shape=jax.ShapeDtypeStruct((n_rows, n), F32),
        compiler_params=_cparams("parallel", "parallel"),
        name="out_proj",
    )(a, w, z, modtab)


def _pack_halves(h):
    w = h.shape[1] // 2
    lo = lax.bitcast_convert_type(h[:, :w].astype(F32), jnp.uint32) >> 16
    hi = lax.bitcast_convert_type(h[:, w:].astype(F32), jnp.uint32) & jnp.uint32(0xFFFF0000)
    return lo | hi


def _unpack_halves(p):
    lo = lax.bitcast_convert_type(p << 16, F32).astype(BF16)
    hi = lax.bitcast_convert_type(p & jnp.uint32(0xFFFF0000), F32).astype(BF16)
    return jnp.concatenate([lo, hi], axis=1)


def _gather_rows(x, idx, *, window=32):
    m, w = idx.shape[0], x.shape[1]
    chunk = 128
    assert m % chunk == 0 and chunk % window == 0
    n_chunks = m // chunk
    mesh = plsc.VectorSubcoreMesh(core_axis_name="core", subcore_axis_name="subcore")
    n_workers = mesh.num_cores * mesh.num_subcores

    @pl.kernel(out_type=jax.ShapeDtypeStruct((m, w), x.dtype), mesh=mesh,
               scratch_types=[pltpu.VMEM((chunk,), jnp.int32), pltpu.VMEM((window, w), x.dtype)])
    def gather_kernel(x_hbm, i_hbm, o_hbm, idx_v, buf):
        wid = lax.axis_index("core") * mesh.num_subcores + lax.axis_index("subcore")

        def step(c, carry):
            r0 = (wid + c * n_workers) * chunk
            pltpu.sync_copy(i_hbm.at[pl.ds(r0, chunk)], idx_v)
            for k in range(chunk // window):
                pltpu.sync_copy(x_hbm.at[idx_v.at[pl.ds(k * window, window)]], buf)
                pltpu.sync_copy(buf, o_hbm.at[pl.ds(r0 + k * window, window)])
            return carry

        lax.fori_loop(0, (n_chunks - wid + n_workers - 1) // n_workers, step, 0)

    return gather_kernel(x, idx)


def _route_kernel(z_ref, g_ref, mod_ref, w_ref, b_ref, h_ref, r_ref):
    m = mod_ref[0]
    z = z_ref[...]
    y = z * lax.rsqrt(jnp.mean(z * z, axis=-1, keepdims=True) + EPS) * g_ref[...]
    h = (y * (1.0 + m[4:5]) + m[3:4]).astype(BF16)
    h_ref[...] = _pack_halves(h)
    logits = jnp.dot(h, w_ref[...].astype(BF16), preferred_element_type=F32) + b_ref[...]
    lane_i = lax.broadcasted_iota(jnp.int32, logits.shape, 1)
    lane = lane_i.astype(F32)
    far = float(2 * 128)

    def top(vals):
        best = jnp.max(vals, axis=-1, keepdims=True)
        return best, jnp.min(jnp.where(vals == best, lane, far), axis=-1, keepdims=True)

    is_grp = lane_i < N_GROUPS
    g_max, g_idx = top(jnp.where(is_grp, logits, NEG_BIG))
    p_grp = 1.0 / jnp.sum(jnp.where(is_grp, jnp.exp(logits - g_max), 0.0), axis=-1, keepdims=True)
    e_lane = lane_i - N_GROUPS
    in_grp = (e_lane >= 0) & ((e_lane >> 3).astype(F32) == g_idx)
    cand = jnp.where(in_grp, logits, NEG_BIG)
    t1, i1 = top(cand)
    t2, i2 = top(jnp.where(lane == i1, NEG_BIG, cand))
    e2 = jnp.exp(t2 - t1)
    w1 = p_grp / (1.0 + e2)
    w2 = w1 * e2
    r = jnp.where(lane_i == 0, i1 - N_GROUPS, jnp.where(lane_i == 1, i2 - N_GROUPS,
                  jnp.where(lane_i == 2, w1, jnp.where(lane_i == 3, w2, 0.0))))
    r_ref[...] = r


def _route(rows, z, norm_g, modtab, w_route, b_route, *, n_rows):
    d = z.shape[1]
    tm = rows.tile_rows(256)
    return pl.pallas_call(
        _route_kernel,
        grid=(n_rows // tm,),
        in_specs=[pl.BlockSpec((tm, d), lambda i: (i, 0)),
                  pl.BlockSpec((1, d), lambda i: (0, 0)),
                  pl.BlockSpec((1, 8, d), lambda i: (rows.mod_index(i, tm), 0, 0)),
                  pl.BlockSpec((d, 128), lambda i: (0, 0)),
                  pl.BlockSpec((1, 128), lambda i: (0, 0))],
        out_specs=[pl.BlockSpec((tm, d // 2), lambda i: (i, 0)),
                   pl.BlockSpec((tm, 128), lambda i: (i, 0))],
        out_shape=[jax.ShapeDtypeStruct((n_rows, d // 2), jnp.uint32),
                   jax.ShapeDtypeStruct((n_rows, 128), F32)],
        compiler_params=_cparams("parallel"),
        name="moe_route",
    )(z, norm_g.reshape(1, d), modtab, w_route, b_route)


def _rank_kernel(r_ref, rank_ref, cnt_ref, carry_scr):
    @pl.when(pl.program_id(0) == 0)
    def _():
        carry_scr[...] = jnp.zeros_like(carry_scr)

    r = r_ref[...]
    tm = r.shape[0]
    lane_i = lax.broadcasted_iota(jnp.int32, r.shape, 1)
    lane = lane_i.astype(F32)
    hot1 = lane == r[:, 0:1]
    hot2 = lane == r[:, 1:2]
    cnt = jnp.where(hot1 | hot2, 1.0, 0.0)
    row = lax.broadcasted_iota(jnp.int32, (tm, tm), 0)
    col = lax.broadcasted_iota(jnp.int32, (tm, tm), 1)
    before = jnp.where(col < row, 1.0, 0.0).astype(BF16)
    carry = carry_scr[0:1, :]
    prior = jnp.dot(before, cnt.astype(BF16), preferred_element_type=F32) + carry
    rank1 = jnp.sum(jnp.where(hot1, prior, 0.0), axis=-1, keepdims=True)
    rank2 = jnp.sum(jnp.where(hot2, prior, 0.0), axis=-1, keepdims=True)
    rank_ref[...] = jnp.where(lane_i == 0, rank1, jnp.where(lane_i == 1, rank2, 0.0))
    total = carry + jnp.sum(cnt, axis=0, keepdims=True)
    carry_scr[0:1, :] = total
    cnt_ref[...] = jnp.broadcast_to(total, cnt_ref.shape)


def _rank(r, *, tm):
    n = r.shape[0]
    return pl.pallas_call(
        _rank_kernel,
        grid=(n // tm,),
        in_specs=[pl.BlockSpec((tm, 128), lambda i: (i, 0))],
        out_specs=[pl.BlockSpec((tm, 128), lambda i: (i, 0)),
                   pl.BlockSpec((8, 128), lambda i: (0, 0))],
        out_shape=[jax.ShapeDtypeStruct((n, 128), F32), jax.ShapeDtypeStruct((8, 128), F32)],
        scratch_shapes=[pltpu.VMEM((8, 128), F32)],
        compiler_params=_cparams("arbitrary"),
        name="moe_rank",
    )(r)


def _expert_kernel(be_ref, fi_ref, nu_ref, x_ref, wg_ref, wu_ref, wd_ref, o_ref, wg_b, wu_b, wd_b):
    i = pl.program_id(0)

    @pl.when((i < nu_ref[0]) & (fi_ref[i] == 1))
    def _():
        wg_b[...] = wg_ref[...].astype(BF16)
        wu_b[...] = wu_ref[...].astype(BF16)
        wd_b[...] = wd_ref[...].astype(BF16)

    @pl.when(i < nu_ref[0])
    def _():
        x = _unpack_halves(x_ref[...])
        dot = lambda a, b: jnp.dot(a, b, preferred_element_type=F32)
        hid = _silu(dot(x, wg_b[...])) * dot(x, wu_b[...])
        o_ref[...] = dot(hid.astype(BF16), wd_b[...])

    @pl.when(i >= nu_ref[0])
    def _():
        o_ref[...] = jnp.zeros_like(o_ref)


def _expert_blocks(xs, blk_expert, first, n_used, w_gate, w_up, w_down, layer):
    n_rows, half = xs.shape
    d = 2 * half
    n_blocks = n_rows // MOE_BLOCK
    wspec = lambda a, b: pl.BlockSpec((None, None, a, b), lambda i, be, fi, nu: (layer, be[i], 0, 0))
    grid_spec = pltpu.PrefetchScalarGridSpec(
        num_scalar_prefetch=3,
        grid=(n_blocks,),
        in_specs=[pl.BlockSpec((MOE_BLOCK, half), lambda i, be, fi, nu: (i, 0)),
                  wspec(d, D_EXPERT), wspec(d, D_EXPERT), wspec(D_EXPERT, d)],
        out_specs=pl.BlockSpec((MOE_BLOCK, d), lambda i, be, fi, nu: (i, 0)),
        scratch_shapes=[pltpu.VMEM((d, D_EXPERT), BF16), pltpu.VMEM((d, D_EXPERT), BF16),
                        pltpu.VMEM((D_EXPERT, d), BF16)],
    )
    return pl.pallas_call(
        _expert_kernel,
        grid_spec=grid_spec,
        out_shape=jax.ShapeDtypeStruct((n_rows, d), F32),
        compiler_params=_cparams("arbitrary"),
        name="moe_experts",
    )(blk_expert, first, n_used, xs, w_gate, w_up, w_down)


def _hier_moe(rows, z, norm_g, modtab, w_grp, b_grp, w_exp, b_exp, w_gate, w_up, w_down, layer, *, n_rows):
    d = z.shape[1]
    n_logit = N_GROUPS + N_EXPERTS
    w_route = jnp.concatenate([w_grp, w_exp, jnp.zeros((d, 128 - n_logit), F32)], axis=1)
    b_route = jnp.concatenate([b_grp, b_exp, jnp.zeros((128 - n_logit,), F32)]).reshape(1, 128)
    h2, route = _route(rows, z, norm_g, modtab, w_route, b_route, n_rows=n_rows)
    rank, totals = _rank(route, tm=rows.tile_rows(512))
    eid = route[:, :TOP_K].astype(jnp.int32)
    wts = route[:, TOP_K:2 * TOP_K]
    counts = totals[0, :N_EXPERTS].astype(jnp.int32)
    padded = (counts + MOE_BLOCK - 1) // MOE_BLOCK * MOE_BLOCK
    pad_end = jnp.cumsum(padded)
    pad_start = pad_end - padded
    dest = pad_start[eid] + rank[:, :TOP_K].astype(jnp.int32)
    n_assign = n_rows * TOP_K
    n_blocks = (n_assign + N_EXPERTS * (MOE_BLOCK - 1) + MOE_BLOCK - 1) // MOE_BLOCK
    n_pad_rows = n_blocks * MOE_BLOCK
    tok = jnp.broadcast_to(jnp.arange(n_rows, dtype=jnp.int32)[:, None], (n_rows, TOP_K))
    filler = jnp.arange(n_pad_rows, dtype=jnp.int32) % n_rows
    row_tok = filler.at[dest.reshape(-1)].set(tok.reshape(-1))
    blk_start = jnp.arange(n_blocks, dtype=jnp.int32) * MOE_BLOCK
    blk_expert = jnp.minimum(jnp.sum((pad_end[None, :] <= blk_start[:, None]).astype(jnp.int32), axis=1),
                             N_EXPERTS - 1)
    first = jnp.concatenate([jnp.ones((1,), jnp.int32), (blk_expert[1:] != blk_expert[:-1]).astype(jnp.int32)])
    n_used = (pad_end[-1] // MOE_BLOCK).astype(jnp.int32).reshape(1)
    xs = _gather_rows(h2, row_tok, window=64)
    ys = _expert_blocks(xs, blk_expert, first, n_used, w_gate, w_up, w_down, layer)
    picked = _gather_rows(ys, dest.T.reshape(-1), window=32)
    return picked[:n_rows] * wts[:, 0:1] + picked[n_rows:] * wts[:, 1:2]


def kernel(x, c, ctx, c_ctx, mod_w, mod_b, norm1_g, norm2_g, w_in, dn_conv, dn_a_log, dn_dt_bias, dn_norm_g, dn_out, lru_conv, lru_conv_b, lru_wa, lru_ba, lru_wx, lru_bx, lru_lambda, lru_out, att_qn_g, att_kn_g, att_sink, att_out, w_o, moe_w_grp, moe_b_grp, moe_w_exp, moe_b_exp, moe_w_gate, moe_w_up, moe_w_down):
    batch, seq, d = x.shape
    n_ctx = ctx.shape[1]
    depth = mod_w.shape[0]
    rows = _Rows(batch, seq, n_ctx)
    assert d == D_MODEL and batch + 1 <= 8

    z = jnp.concatenate([x.reshape(batch * seq, d), ctx.reshape(batch * n_ctx, d)], axis=0)

    cc = jnp.concatenate([c_ctx[None], c, jnp.zeros((7 - batch, d), F32)], axis=0)
    mods = _mod_tables(cc, mod_w, mod_b).reshape(depth, 8, 6, d)
    mods = jnp.concatenate([mods, jnp.zeros((depth, 8, 2, d), F32)], axis=2)

    cos, sin = _rope_tables(seq, n_ctx)
    pc = _ProjCols
    w_proj = _relayout_w_in(w_in)

    for l in range(depth):
        last = l == depth - 1
        modtab = mods[l]
        n_out = rows.n_lat_rows if last else rows.n_rows

        p = _norm_matmul(rows, z, norm1_g[l], modtab, w_proj, layer=l, shift_row=0, scale_row=1, tn=2 * pc.tile,
                         name="in_proj")

        qkv = _dn_prep(rows, p, dn_conv[l])
        par = jnp.zeros((2, 8, 128), F32)
        par = par.at[:, 0, :N_HEADS].set(dn_a_log[l]).at[:, 1, :N_HEADS].set(dn_dt_bias[l])
        o_f = _dn_scan(rows, qkv, p, par, small_col=pc.scalars, reverse=False)
        o_b = _dn_scan(rows, qkv, p, par, small_col=pc.scalars, reverse=True)

        lru_args = (lru_conv[l], lru_conv_b[l], lru_wa[l], lru_wx[l], lru_ba[l], lru_bx[l], lru_lambda[l])
        hs_f = _lru_scan(rows, p, *lru_args, x_col=pc.lx, reverse=False)
        hs_b = _lru_scan(rows, p, *lru_args, x_col=pc.lx, reverse=True)

        qn, kn = _qk_prep(rows, p, cos, sin, att_qn_g[l], att_kn_g[l], pc.aq, pc.ak)
        sink_b = jnp.broadcast_to(att_sink[l][:, None], (N_HEADS, 128))
        y_c = _attention(rows, qn, kn, p, pc.av, sink_b, with_ctx_queries=not last)

        merged = _merge(rows, o_f, o_b, p, hs_f, hs_b, p, y_c, dn_out[l].astype(BF16), lru_out[l].astype(BF16),
                        att_out[l].astype(BF16), dn_norm_g[l], z_col=pc.z, lg_col=pc.lg, gate_col=pc.gates,
                        n_rows=n_out)
        z = _mm_residual(rows, merged, w_o[l].astype(BF16), z, modtab, gate_row=2, n_rows=n_out)

        f = _hier_moe(rows, z, norm2_g[l], modtab, moe_w_grp[l], moe_b_grp[l], moe_w_exp[l], moe_b_exp[l],
                      moe_w_gate, moe_w_up, moe_w_down, l, n_rows=n_out)
        gate2 = jnp.concatenate([jnp.repeat(modtab[1:1 + batch, 5], seq, axis=0),
                                 jnp.broadcast_to(modtab[0, 5], (batch * n_ctx, d))], axis=0)[:n_out]
        z = z + gate2 * f

    return z[:rows.n_lat_rows].reshape(batch, seq, d)
```
